```python
import jax
import jax.numpy as jnp
from jax import lax
import numpy as np

D_MODEL = 1024
BATCH = 16
SEQ = 256
DEPTH = 2
DEC_BATCH = 2
DEC_SEQ = 4096
PAST_LEN = 512

GRID_W = 64
HEAD_DIM = 64
N_GROUPS = 4
GROUP_W = D_MODEL // N_GROUPS
MIX_W = N_GROUPS * GROUP_W
GROUP_HEADS = GROUP_W // HEAD_DIM
CHUNK = 128
NA_KH = 8
NA_KW = 16
NA_QB_W = 16
NA_KB_W = 2 * NA_KW
MLA_Q_RANK = D_MODEL // 4
MLA_KV_RANK = D_MODEL // 8
MLA_NOPE = 64
MLA_ROPE = 32
MLA_V = 64
MLA_SCALE = (MLA_NOPE + MLA_ROPE) ** -0.5
SWA_KV_HEADS = GROUP_HEADS // 2
SWA_GROUP = GROUP_HEADS // SWA_KV_HEADS
SWA_WIN = 128
PEER_HEADS = 8
PEER_N_KEYS = 128
PEER_TOPK = 16
PEER_HALF = 128
N_EXPERTS = PEER_N_KEYS * PEER_N_KEYS
Q_BLOCK = 128
TOKEN_BLOCK = 128
ROPE_BASE = 10000.0
LN_EPS = 1e-5
NEG_INF = -1e30
ALPHA = (2 * DEPTH) ** 0.25
BETA = (8 * DEPTH) ** -0.25
A_COLS = 2 * GROUP_W
B_COLS = 3 * GROUP_W
C_COLS = MLA_Q_RANK + MLA_KV_RANK + MLA_ROPE
D_COLS = (GROUP_HEADS + 2 * SWA_KV_HEADS) * HEAD_DIM
IN_COLS = A_COLS + B_COLS + C_COLS + D_COLS

kernel_name = 'hybrid_flow_prefix_peer_step'


def layer_norm(x, g, b):
    xf = x.astype(jnp.float32)
    mu = jnp.mean(xf, -1, keepdims=True)
    var = jnp.mean(jnp.square(xf - mu), -1, keepdims=True)
    return ((xf - mu) * lax.rsqrt(var + LN_EPS) * g + b).astype(x.dtype)


def rms_norm(x, g):
    xf = x.astype(jnp.float32)
    return (xf * lax.rsqrt(jnp.mean(jnp.square(xf), -1, keepdims=True) + LN_EPS) * g).astype(x.dtype)


def axial_rope(n_tok, rot_dim):
    t = jnp.arange(n_tok)
    row = (t // GRID_W).astype(jnp.float32)
    col = (t % GRID_W).astype(jnp.float32)
    nf = rot_dim // 4
    freqs = ROPE_BASE ** (-jnp.arange(nf, dtype=jnp.float32) / nf)
    ang = jnp.concatenate([row[:, None] * freqs, col[:, None] * freqs], -1)
    return jnp.cos(ang), jnp.sin(ang)


def apply_rope(x, cos, sin):
    half = x.shape[-1] // 2
    shape = (x.shape[1],) + (1,) * (x.ndim - 3) + (half,)
    cos = cos.reshape(shape).astype(x.dtype)
    sin = sin.reshape(shape).astype(x.dtype)
    x1, x2 = x[..., :half], x[..., half:]
    return jnp.concatenate([x1 * cos - x2 * sin, x2 * cos + x1 * sin], -1)


def heads(t, n):
    return t.reshape(t.shape[0], t.shape[1], n, -1)


def modulation(cond, w_ada, b_ada):
    m = (jax.nn.silu(cond) @ w_ada + b_ada).reshape(cond.shape[0], 1, 6, D_MODEL)
    return [m[:, :, i] for i in range(6)]


def split_proj(h, w_in):
    z = h @ w_in
    return jnp.split(z, [A_COLS, A_COLS + B_COLS, A_COLS + B_COLS + C_COLS], axis=-1)


def split_mla(hC):
    return (hC[..., :MLA_Q_RANK], hC[..., MLA_Q_RANK:MLA_Q_RANK + MLA_KV_RANK],
            hC[..., MLA_Q_RANK + MLA_KV_RANK:])


def split_swa(hD):
    nq = GROUP_HEADS * HEAD_DIM
    nk = SWA_KV_HEADS * HEAD_DIM
    return (heads(hD[..., :nq], GROUP_HEADS), heads(hD[..., nq:nq + nk], SWA_KV_HEADS),
            heads(hD[..., nq + nk:], SWA_KV_HEADS))


def chunk_mlp(hA, p):
    B, L, _ = hA.shape
    u, v = jnp.split(jax.nn.gelu(hA), 2, axis=-1)
    v = layer_norm(v, p['a_norm_g'], p['a_norm_b'])
    vc = v.reshape(B, L // CHUNK, CHUNK, GROUP_HEADS, GROUP_W // GROUP_HEADS)
    mixed = jnp.einsum('hpq,bcqhd->bcphd', p['a_w_s'], vc) + p['a_b_s'].T[None, None, :, :, None]
    return u * mixed.reshape(B, L, GROUP_W)


def dense_attention(q, k, v, scale, sinks=None):
    B, Lq, H, dq = q.shape
    qb = jnp.moveaxis(q.reshape(B, Lq // Q_BLOCK, Q_BLOCK, H, dq), 1, 0)

    def one_block(qblk):
        s = jnp.einsum('bqhd,bkhd->bhqk', qblk, k, preferred_element_type=jnp.float32) * scale
        if sinks is not None:
            sk = jnp.broadcast_to(sinks.astype(jnp.float32)[None, :, None, None], s.shape[:-1] + (1,))
            pr = jax.nn.softmax(jnp.concatenate([s, sk], -1), -1)[..., :-1]
        else:
            pr = jax.nn.softmax(s, -1)
        return jnp.einsum('bhqk,bkhd->bqhd', pr.astype(v.dtype), v)

    out = lax.map(one_block, qb)
    return jnp.moveaxis(out, 0, 1).reshape(B, Lq, H * v.shape[-1])


def neighbourhood_attention(q, k, v, ck, cv, rpb):
    B, L, H, dh = q.shape
    rows = L // GRID_W
    kh = min(NA_KH, rows)
    ncb = GRID_W // NA_QB_W
    nk = kh * NA_KB_W
    r = jnp.arange(rows)
    row_idx = jnp.clip(r - kh // 2, 0, rows - kh)[:, None] + jnp.arange(kh)[None, :]
    cb = jnp.arange(ncb)
    col_idx = (jnp.clip(cb * NA_QB_W - NA_KW // 2, 0, GRID_W - NA_KB_W)[:, None]
               + jnp.arange(NA_KB_W)[None, :])
    qcol = cb[:, None] * NA_QB_W + jnp.arange(NA_QB_W)[None, :]
    cstart = jnp.clip(qcol - NA_KW // 2, 0, GRID_W - NA_KW)
    kc = col_idx[:, None, :]
    col_ok = (kc >= cstart[..., None]) & (kc < cstart[..., None] + NA_KW)
    mask = jnp.broadcast_to(col_ok[:, :, None, :], (ncb, NA_QB_W, kh, NA_KB_W)).reshape(ncb, NA_QB_W, nk)
    dr = row_idx - r[:, None] + NA_KH - 1
    dc = jnp.clip(kc - qcol[..., None] + NA_KW - 1, 0, 2 * NA_KW - 2)
    bias = rpb[:, dr[:, None, None, :, None], dc[None, :, :, None, :]].reshape(H, rows, ncb, NA_QB_W, nk)
    ri = row_idx[:, None, :, None]
    ci = col_idx[None, :, None, :]
    k_loc = k.reshape(B, rows, GRID_W, H, dh)[:, ri, ci].reshape(B, rows, ncb, nk, H, dh)
    v_loc = v.reshape(B, rows, GRID_W, H, dh)[:, ri, ci].reshape(B, rows, ncb, nk, H, dh)
    qb = q.reshape(B, rows, ncb, NA_QB_W, H, dh)
    scale = dh ** -0.5
    s_loc = (jnp.einsum('brcqhd,brckhd->bhrcqk', qb, k_loc, preferred_element_type=jnp.float32) * scale
             + bias.astype(jnp.float32)[None])
    s_loc = jnp.where(mask, s_loc, NEG_INF)
    s_ctx = jnp.einsum('brcqhd,bhkd->bhrcqk', qb, ck, preferred_element_type=jnp.float32) * scale
    pr = jax.nn.softmax(jnp.concatenate([s_loc, s_ctx], -1), -1).astype(v.dtype)
    out = (jnp.einsum('bhrcqk,brckhd->brcqhd', pr[..., :nk], v_loc)
           + jnp.einsum('bhrcqk,bhkd->brcqhd', pr[..., nk:], cv))
    return out.reshape(B, L, H * dh)


def sliding_window_attention(q, k, v, ck, cv, sinks):
    B, L, Hq, dh = q.shape
    nb = L // SWA_WIN
    pad = ((0, 0), (SWA_WIN, SWA_WIN), (0, 0), (0, 0))
    kp = jnp.pad(k, pad).reshape(B, nb + 2, SWA_WIN, SWA_KV_HEADS, dh)
    vp = jnp.pad(v, pad).reshape(B, nb + 2, SWA_WIN, SWA_KV_HEADS, dh)
    kband = jnp.concatenate([kp[:, :-2], kp[:, 1:-1], kp[:, 2:]], axis=2)
    vband = jnp.concatenate([vp[:, :-2], vp[:, 1:-1], vp[:, 2:]], axis=2)
    qg = q.reshape(B, nb, SWA_WIN, SWA_KV_HEADS, SWA_GROUP, dh)
    scale = dh ** -0.5
    nband = 3 * SWA_WIN
    s = jnp.einsum('bnqkgd,bnjkd->bkgnqj', qg, kband, preferred_element_type=jnp.float32) * scale
    qi = jnp.arange(SWA_WIN)
    kj = jnp.arange(nband)
    in_win = jnp.abs(kj[None, :] - SWA_WIN - qi[:, None]) <= SWA_WIN
    kpos = jnp.arange(nb)[:, None] * SWA_WIN - SWA_WIN + kj[None, :]
    in_seq = (kpos >= 0) & (kpos < L)
    s = jnp.where(in_win[None] & in_seq[:, None, :], s, NEG_INF)
    s_ctx = jnp.einsum('bnqkgd,bkcd->bkgnqc', qg, ck, preferred_element_type=jnp.float32) * scale
    sk = jnp.broadcast_to(sinks.astype(jnp.float32).reshape(SWA_KV_HEADS, SWA_GROUP)[None, :, :, None, None, None],
                          s.shape[:-1] + (1,))
    pr = jax.nn.softmax(jnp.concatenate([s, s_ctx, sk], -1), -1).astype(v.dtype)
    nctx = ck.shape[2]
    out = (jnp.einsum('bkgnqj,bnjkd->bnqkgd', pr[..., :nband], vband)
           + jnp.einsum('bkgnqc,bkcd->bnqkgd', pr[..., nband:nband + nctx], cv))
    return out.reshape(B, L, Hq * dh)


def mla_query(cq, p):
    q = rms_norm(cq, p['mla_q_norm_g']) @ p['mla_w_uq']
    return heads(q, GROUP_HEADS)


def mla_expand_kv(ckv_n, krope, w_ukv):
    B, L, _ = ckv_n.shape
    kv = (ckv_n @ w_ukv).reshape(B, L, GROUP_HEADS, MLA_NOPE + MLA_V)
    k = jnp.concatenate([kv[..., :MLA_NOPE],
                         jnp.broadcast_to(krope[:, :, None, :], (B, L, GROUP_HEADS, MLA_ROPE))], -1)
    return k, kv[..., MLA_NOPE:]


def merge_groups(outs, g, w_out):
    o = jnp.concatenate(outs, -1)
    B, L, _ = o.shape
    o = rms_norm(o.reshape(B, L, N_GROUPS, GROUP_W), g.reshape(N_GROUPS, GROUP_W)).reshape(B, L, MIX_W)
    return o @ w_out


def peer_ffn(h, p):
    B, L, D = h.shape
    xt = h.reshape(B * L // TOKEN_BLOCK, TOKEN_BLOCK, D)

    def one_block(xb):
        q = (xb @ p['peer_w_q']).reshape(TOKEN_BLOCK, PEER_HEADS, 2, PEER_HALF)
        s = jnp.einsum('thpd,pnd->thpn', q, p['peer_sub_keys'], preferred_element_type=jnp.float32)
        sv, si = lax.top_k(s, PEER_TOPK)
        cand = (sv[:, :, 0, :, None] + sv[:, :, 1, None, :]).reshape(TOKEN_BLOCK, PEER_HEADS, PEER_TOPK * PEER_TOPK)
        cid = (si[:, :, 0, :, None] * PEER_N_KEYS + si[:, :, 1, None, :]).reshape(TOKEN_BLOCK, PEER_HEADS, PEER_TOPK * PEER_TOPK)
        fv, fi = lax.top_k(cand, PEER_TOPK)
        eid = jnp.take_along_axis(cid, fi, -1)
        gate = jax.nn.softmax(fv, -1)
        act = jax.nn.gelu(jnp.einsum('thkd,td->thk', p['peer_u'][eid], xb, preferred_element_type=jnp.float32))
        return jnp.einsum('thk,thkd->td', (gate * act).astype(xb.dtype), p['peer_v'][eid])

    return lax.map(one_block, xt).reshape(B, L, D)


def context_mixer(h, p):
    hA, hB, hC, hD = split_proj(h, p['w_in'])
    oA = chunk_mlp(hA, p)
    qB, kB, vB = [heads(t, GROUP_HEADS) for t in jnp.split(hB, 3, axis=-1)]
    oB = dense_attention(qB, kB, vB, HEAD_DIM ** -0.5)
    cq, ckv, krope = split_mla(hC)
    qC = mla_query(cq, p)
    ckv_n = rms_norm(ckv, p['mla_kv_norm_g'])
    kC, vC = mla_expand_kv(ckv_n, krope, p['mla_w_ukv'])
    oC = dense_attention(qC, kC, vC, MLA_SCALE)
    qD, kD, vD = split_swa(hD)
    oD = dense_attention(qD, jnp.repeat(kD, SWA_GROUP, axis=2), jnp.repeat(vD, SWA_GROUP, axis=2),
                         HEAD_DIM ** -0.5, p['swa_sinks'])
    state = (kB.transpose(0, 2, 1, 3), vB.transpose(0, 2, 1, 3), ckv_n, krope,
             kD.transpose(0, 2, 1, 3), vD.transpose(0, 2, 1, 3))
    return [oA, oB, oC, oD], state


def latent_mixer(h, p, ctx, rope):
    nat_k, nat_v, mla_ckv, mla_krope, swa_k, swa_v = ctx
    cos_c, sin_c, cos_d, sin_d = rope
    hA, hB, hC, hD = split_proj(h, p['w_in'])
    oA = chunk_mlp(hA, p)
    qB, kB, vB = [heads(t, GROUP_HEADS) for t in jnp.split(hB, 3, axis=-1)]
    oB = neighbourhood_attention(qB, kB, vB, nat_k, nat_v, p['nat_rpb'])
    cq, ckv, krope = split_mla(hC)
    qC = mla_query(cq, p)
    qC = jnp.concatenate([qC[..., :MLA_NOPE], apply_rope(qC[..., MLA_NOPE:], cos_c, sin_c)], -1)
    kC, vC = mla_expand_kv(rms_norm(ckv, p['mla_kv_norm_g']), apply_rope(krope, cos_c, sin_c), p['mla_w_ukv'])
    kX, vX = mla_expand_kv(mla_ckv, mla_krope, p['mla_w_ukv'])
    oC = dense_attention(qC, jnp.concatenate([kX, kC], 1), jnp.concatenate([vX, vC], 1), MLA_SCALE)
    qD, kD, vD = split_swa(hD)
    oD = sliding_window_attention(apply_rope(qD, cos_d, sin_d), apply_rope(kD, cos_d, sin_d), vD,
                                  swa_k, swa_v, p['swa_sinks'])
    return [oA, oB, oC, oD], None


def trunk_layer(x, cond, p, token_mixer):
    sh1, sc1, g1, sh2, sc2, g2 = modulation(cond, p['w_ada'], p['b_ada'])
    outs, state = token_mixer(x * (1 + sc1) + sh1, p)
    x = layer_norm(ALPHA * x + g1 * merge_groups(outs, p['out_norm_g'], p['w_out']), p['ln1_g'], p['ln1_b'])
    x = layer_norm(ALPHA * x + g2 * peer_ffn(x * (1 + sc2) + sh2, p), p['ln2_g'], p['ln2_b'])
    return x, state


def setup_inputs(seed: int = 0) -> dict:
    key = jax.random.key(seed)
    ks = iter(jax.random.split(key, 40))

    def nrm(shape, s=1.0):
        return jax.random.normal(next(ks), shape, jnp.float32) * s

    def gain(shape):
        return 1.0 + nrm(shape, 0.02)

    return {
        'x_prompt': nrm((BATCH, SEQ, D_MODEL)),
        'x_sample': nrm((DEC_BATCH, DEC_SEQ, D_MODEL)),
        'cache_nat_k': nrm((DEC_BATCH, DEPTH, GROUP_HEADS, PAST_LEN, HEAD_DIM)),
        'cache_nat_v': nrm((DEC_BATCH, DEPTH, GROUP_HEADS, PAST_LEN, HEAD_DIM)),
        'cache_mla_ckv': nrm((DEC_BATCH, DEPTH, PAST_LEN, MLA_KV_RANK)),
        'cache_mla_krope': nrm((DEC_BATCH, DEPTH, PAST_LEN, MLA_ROPE)),
        'cache_swa_k': nrm((DEC_BATCH, DEPTH, SWA_KV_HEADS, PAST_LEN, HEAD_DIM)),
        'cache_swa_v': nrm((DEC_BATCH, DEPTH, SWA_KV_HEADS, PAST_LEN, HEAD_DIM)),
        'c': nrm((DEC_BATCH, D_MODEL)),
        'c_ctx': nrm((D_MODEL,)),
        'w_in': nrm((DEPTH, D_MODEL, IN_COLS), D_MODEL ** -0.5),
        'w_out': nrm((DEPTH, MIX_W, D_MODEL), BETA * MIX_W ** -0.5),
        'out_norm_g': gain((DEPTH, MIX_W)),
        'w_ada': nrm((DEPTH, D_MODEL, 6 * D_MODEL), 0.5 * D_MODEL ** -0.5),
        'b_ada': nrm((DEPTH, 6 * D_MODEL), 0.02),
        'ln1_g': gain((DEPTH, D_MODEL)),
        'ln1_b': nrm((DEPTH, D_MODEL), 0.02),
        'ln2_g': gain((DEPTH, D_MODEL)),
        'ln2_b': nrm((DEPTH, D_MODEL), 0.02),
        'a_norm_g': gain((DEPTH, GROUP_W)),
        'a_norm_b': nrm((DEPTH, GROUP_W), 0.02),
        'a_w_s': nrm((DEPTH, GROUP_HEADS, CHUNK, CHUNK), CHUNK ** -0.5),
        'a_b_s': gain((DEPTH, GROUP_HEADS, CHUNK)),
        'nat_rpb': nrm((DEPTH, GROUP_HEADS, 2 * NA_KH - 1, 2 * NA_KW - 1), 0.1),
        'mla_q_norm_g': gain((DEPTH, MLA_Q_RANK)),
        'mla_w_uq': nrm((DEPTH, MLA_Q_RANK, GROUP_HEADS * (MLA_NOPE + MLA_ROPE)), MLA_Q_RANK ** -0.5),
        'mla_kv_norm_g': gain((DEPTH, MLA_KV_RANK)),
        'mla_w_ukv': nrm((DEPTH, MLA_KV_RANK, GROUP_HEADS * (MLA_NOPE + MLA_V)), MLA_KV_RANK ** -0.5),
        'swa_sinks': nrm((DEPTH, GROUP_HEADS), 0.5),
        'peer_w_q': nrm((DEPTH, D_MODEL, PEER_HEADS * 2 * PEER_HALF), D_MODEL ** -0.5),
        'peer_sub_keys': nrm((DEPTH, 2, PEER_N_KEYS, PEER_HALF), PEER_HALF ** -0.5),
        'peer_u': nrm((DEPTH, N_EXPERTS, D_MODEL), D_MODEL ** -0.5),
        'peer_v': nrm((DEPTH, N_EXPERTS, D_MODEL), BETA),
    }


def reference(x_prompt, x_sample, cache_nat_k, cache_nat_v, cache_mla_ckv, cache_mla_krope,
              cache_swa_k, cache_swa_v, c, c_ctx, w_in, w_out, out_norm_g, w_ada, b_ada,
              ln1_g, ln1_b, ln2_g, ln2_b, a_norm_g, a_norm_b, a_w_s, a_b_s, nat_rpb,
              mla_q_norm_g, mla_w_uq, mla_kv_norm_g, mla_w_ukv, swa_sinks,
              peer_w_q, peer_sub_keys, peer_u, peer_v):
    lat_len = x_sample.shape[1]
    cos_c, sin_c = axial_rope(lat_len, MLA_ROPE)
    cos_d, sin_d = axial_rope(lat_len, HEAD_DIM)
    rope = (cos_c, sin_c, cos_d, sin_d)
    y_prompt = x_prompt
    y_sample = x_sample
    st_nat_k, st_nat_v, st_ckv, st_krope, st_swa_k, st_swa_v = [], [], [], [], [], []
    for l in range(DEPTH):
        p = {
            'w_in': w_in[l], 'w_out': w_out[l], 'out_norm_g': out_norm_g[l],
            'w_ada': w_ada[l], 'b_ada': b_ada[l],
            'ln1_g': ln1_g[l], 'ln1_b': ln1_b[l], 'ln2_g': ln2_g[l], 'ln2_b': ln2_b[l],
            'a_norm_g': a_norm_g[l], 'a_norm_b': a_norm_b[l], 'a_w_s': a_w_s[l], 'a_b_s': a_b_s[l],
            'nat_rpb': nat_rpb[l],
            'mla_q_norm_g': mla_q_norm_g[l], 'mla_w_uq': mla_w_uq[l],
            'mla_kv_norm_g': mla_kv_norm_g[l], 'mla_w_ukv': mla_w_ukv[l],
            'swa_sinks': swa_sinks[l],
            'peer_w_q': peer_w_q[l], 'peer_sub_keys': peer_sub_keys[l],
            'peer_u': peer_u[l], 'peer_v': peer_v[l],
        }
        y_prompt, st = trunk_layer(y_prompt, c_ctx[None, :], p, context_mixer)
        st_nat_k.append(st[0]); st_nat_v.append(st[1]); st_ckv.append(st[2])
        st_krope.append(st[3]); st_swa_k.append(st[4]); st_swa_v.append(st[5])
        ctx = (cache_nat_k[:, l], cache_nat_v[:, l], cache_mla_ckv[:, l], cache_mla_krope[:, l],
               cache_swa_k[:, l], cache_swa_v[:, l])
        y_sample, _ = trunk_layer(y_sample, c, p,
                                  lambda hh, pp: latent_mixer(hh, pp, ctx, rope))
    new_nat_k = jnp.stack(st_nat_k, axis=1)
    new_nat_v = jnp.stack(st_nat_v, axis=1)
    new_mla_ckv = jnp.stack(st_ckv, axis=1)
    new_mla_krope = jnp.stack(st_krope, axis=1)
    new_swa_k = jnp.stack(st_swa_k, axis=1)
    new_swa_v = jnp.stack(st_swa_v, axis=1)
    return (y_prompt, y_sample, new_nat_k, new_nat_v, new_mla_ckv, new_mla_krope, new_swa_k, new_swa_v)
```

```python
import functools
import math

import jax
import jax.numpy as jnp
import numpy as np
from jax import lax
from jax.experimental import pallas as pl
from jax.experimental.pallas import tpu as pltpu

F32 = jnp.float32
BF16 = jnp.bfloat16

D_MODEL = 1024
DEPTH = 2
GRID_W = 64
HEAD_DIM = 64
N_HEADS = 4
GROUP_W = 256
CHUNK = 128
NA_KH = 8
NA_KW = 16
MLA_Q_RANK = 256
MLA_KV_RANK = 128
MLA_NOPE = 64
MLA_ROPE = 32
MLA_SCALE = (MLA_NOPE + MLA_ROPE) ** -0.5
SWA_WIN = 128
PEER_HEADS = 8
PEER_KEYS = 128
PEER_TOPK = 16
N_EXPERTS = PEER_KEYS * PEER_KEYS
ROPE_BASE = 10000.0
LN_EPS = 1e-5
NEG_INF = -1e30
ALPHA = (2 * DEPTH) ** 0.25

V7X_VMEM_LIMIT_BYTES = 56 * 1024 * 1024
TM = 256
TB_ROUTE = 256
TB_EXP = 512
EC = 1024
N_SLAB = EC // PEER_KEYS
N_CAND_ROWS = 96

_C_A = 0
_C_QB, _C_KB, _C_VB = 512, 768, 1024
_C_CQ, _C_CKV, _C_KR, _C_KRS = 1280, 1536, 1664, 1792
_C_QD, _C_QDS, _C_KD, _C_KDS, _C_VD = 1920, 2176, 2432, 2688, 2944
_C_END = 3200


def _cparams(sem):
    return pltpu.CompilerParams(dimension_semantics=sem, vmem_limit_bytes=V7X_VMEM_LIMIT_BYTES)


def _dot(a, b):
    return jnp.dot(a, b, preferred_element_type=F32)


def _dot_nt(a, b):
    return lax.dot_general(a, b, (((1,), (1,)), ((), ())), preferred_element_type=F32)


def _layer_norm(x, g, b):
    mu = jnp.mean(x, axis=-1, keepdims=True)
    xc = x - mu
    var = jnp.mean(xc * xc, axis=-1, keepdims=True)
    return xc * lax.rsqrt(var + LN_EPS) * g + b


def _rms_norm(x, g):
    return x * lax.rsqrt(jnp.mean(x * x, axis=-1, keepdims=True) + LN_EPS) * g


def _ada_kernel(c_ref, w_ref, b_ref, o_ref):
    c = c_ref[...]
    a = c * jax.nn.sigmoid(c)
    a_hi = a.astype(BF16)
    a_lo = (a - a_hi.astype(F32)).astype(BF16)
    w = w_ref[0]
    w_hi = w.astype(BF16)
    w_lo = (w - w_hi.astype(F32)).astype(BF16)
    o_ref[0] = _dot(a_hi, w_hi) + _dot(a_hi, w_lo) + _dot(a_lo, w_hi) + b_ref[0]


def _ada_call(conds, w_ada, b_ada):
    tn = 1536
    n = w_ada.shape[-1]
    return pl.pallas_call(
        _ada_kernel,
        grid=(DEPTH, n // tn),
        in_specs=[pl.BlockSpec((8, D_MODEL), lambda l, j: (0, 0)),
                  pl.BlockSpec((1, D_MODEL, tn), lambda l, j: (l, 0, j)),
                  pl.BlockSpec((1, 1, tn), lambda l, j: (l, 0, j))],
        out_specs=pl.BlockSpec((1, 8, tn), lambda l, j: (l, 0, j)),
        out_shape=jax.ShapeDtypeStruct((DEPTH, 8, n), F32),
        compiler_params=_cparams(("parallel", "parallel")),
        name="ada_mod",
    )(conds, w_ada, b_ada.reshape(DEPTH, 1, n))


def _inproj_kernel(x_ref, mod_ref, w_ref, wuq_ref, wukv_ref, gq_ref, gkv_ref,
                   cosc_ref, sinc_ref, cosd_ref, sind_ref,
                   za_ref, qb_ref, kb_ref, vb_ref, kb16_ref, vb16_ref,
                   qcn_ref, qcr_ref, ckvn_ref, krc_ref, krc16_ref, kcn_ref, vc_ref,
                   qd_ref, kd_ref, vd_ref, kd16_ref, vd16_ref):
    m = mod_ref[0]
    h = x_ref[...] * (1.0 + m[1:2]) + m[0:1]
    z = _dot(h.astype(BF16), w_ref[...])
    cosc, sinc = cosc_ref[...], sinc_ref[...]
    cosd, sind = cosd_ref[...], sind_ref[...]

    za_ref[...] = z[:, _C_A:_C_QB]
    qb_ref[...] = z[:, _C_QB:_C_KB].astype(BF16)
    kb = z[:, _C_KB:_C_VB]
    vb = z[:, _C_VB:_C_CQ]
    kb_ref[...] = kb
    vb_ref[...] = vb
    kb16_ref[...] = kb.astype(BF16)
    vb16_ref[...] = vb.astype(BF16)

    cqn = _rms_norm(z[:, _C_CQ:_C_CKV], gq_ref[...])
    q = _dot(cqn.astype(BF16), wuq_ref[...])
    qcn_ref[...] = q[:, 0:256].astype(BF16)
    qcr_ref[...] = (q[:, 256:384] * cosc + q[:, 384:512] * sinc).astype(BF16)
    ckvn = _rms_norm(z[:, _C_CKV:_C_KR], gkv_ref[...])
    ckvn_ref[...] = ckvn
    kv = _dot(ckvn.astype(BF16), wukv_ref[...])
    kcn_ref[...] = kv[:, 0:256].astype(BF16)
    vc_ref[...] = kv[:, 256:512].astype(BF16)
    kr = z[:, _C_KR:_C_KRS] * cosc + z[:, _C_KRS:_C_QD] * sinc
    krc_ref[...] = kr
    krc16_ref[...] = kr.astype(BF16)

    qd_ref[...] = (z[:, _C_QD:_C_QDS] * cosd + z[:, _C_QDS:_C_KD] * sind).astype(BF16)
    kd = z[:, _C_KD:_C_KDS] * cosd + z[:, _C_KDS:_C_VD] * sind
    vd = z[:, _C_VD:_C_END]
    kd_ref[...] = kd
    vd_ref[...] = vd
    kd16_ref[...] = kd.astype(BF16)
    vd16_ref[...] = vd.astype(BF16)


def _cond_of_tile(i, tile, n_ctx, lat_len):
    n_ctx_tiles = n_ctx // tile
    return jnp.where(i < n_ctx_tiles, 0, 1 + (i - n_ctx_tiles) // (lat_len // tile))


def _rope_block_of_tile(i, tile, n_ctx, lat_len):
    n_ctx_tiles = n_ctx // tile
    return jnp.where(i < n_ctx_tiles, 0, 1 + (i - n_ctx_tiles) % (lat_len // tile))


def _inproj_call(x, mod, wbig, wuq, wukv, gq, gkv, tabs, n_ctx, lat_len):
    nt = x.shape[0]
    cond = functools.partial(_cond_of_tile, tile=TM, n_ctx=n_ctx, lat_len=lat_len)
    rblk = functools.partial(_rope_block_of_tile, tile=TM, n_ctx=n_ctx, lat_len=lat_len)
    row = lambda w: pl.BlockSpec((TM, w), lambda i: (i, 0))
    const = lambda a: pl.BlockSpec(a.shape, lambda i: (0,) * a.ndim)
    tab = lambda w: pl.BlockSpec((TM, w), lambda i: (rblk(i), 0))
    outs = [(512, F32), (256, BF16), (256, F32), (256, F32), (256, BF16), (256, BF16),
            (256, BF16), (128, BF16), (128, F32), (128, F32), (128, BF16), (256, BF16), (256, BF16),
            (256, BF16), (256, F32), (256, F32), (256, BF16), (256, BF16)]
    return pl.pallas_call(
        _inproj_kernel,
        grid=(nt // TM,),
        in_specs=[row(D_MODEL),
                  pl.BlockSpec((1, 6, D_MODEL), lambda i: (cond(i), 0, 0)),
                  const(wbig), const(wuq), const(wukv), const(gq), const(gkv),
                  tab(128), tab(128), tab(256), tab(256)],
        out_specs=[row(w) for w, _ in outs],
        out_shape=[jax.ShapeDtypeStruct((nt, w), dt) for w, dt in outs],
        compiler_params=_cparams(("parallel",)),
        name="inproj",
    )(x, mod, wbig, wuq, wukv, gq, gkv, *tabs)


def _kvexp_kernel(c_ref, w_ref, k_ref, v_ref):
    kv = _dot(c_ref[...].astype(BF16), w_ref[...])
    k_ref[...] = kv[:, 0:256].astype(BF16)
    v_ref[...] = kv[:, 256:512].astype(BF16)


def _kvexp_call(ckv, wukv):
    n = ckv.shape[0]
    return pl.pallas_call(
        _kvexp_kernel,
        grid=(1,),
        in_specs=[pl.BlockSpec(ckv.shape, lambda i: (0, 0)), pl.BlockSpec(wukv.shape, lambda i: (0, 0))],
        out_specs=[pl.BlockSpec((n, 256), lambda i: (0, 0))] * 2,
        out_shape=[jax.ShapeDtypeStruct((n, 256), BF16)] * 2,
        compiler_params=_cparams(("arbitrary",)),
        name="mla_cache_expand",
    )(ckv, wukv)


def _chunk_mlp_kernel(z_ref, g_ref, b_ref, ws_ref, bs_ref, o_ref):
    g = jax.nn.gelu(z_ref[...])
    u = g[:, 0:GROUP_W]
    v = _layer_norm(g[:, GROUP_W:2 * GROUP_W], g_ref[...], b_ref[...]).astype(BF16)
    lane_head = lax.broadcasted_iota(jnp.int32, (1, GROUP_W), 1) // HEAD_DIM
    mixed = bs_ref[...]
    for hd in range(N_HEADS):
        mixed = mixed + jnp.where(lane_head == hd, _dot(ws_ref[hd], v), 0.0)
    o_ref[...] = u * mixed


def _chunk_mlp_call(za, g, b, ws16, bs_full):
    nt = za.shape[0]
    return pl.pallas_call(
        _chunk_mlp_kernel,
        grid=(nt // CHUNK,),
        in_specs=[pl.BlockSpec((CHUNK, 2 * GROUP_W), lambda i: (i, 0)),
                  pl.BlockSpec((1, GROUP_W), lambda i: (0, 0)),
                  pl.BlockSpec((1, GROUP_W), lambda i: (0, 0)),
                  pl.BlockSpec((N_HEADS, CHUNK, CHUNK), lambda i: (0, 0, 0)),
                  pl.BlockSpec((CHUNK, GROUP_W), lambda i: (0, 0))],
        out_specs=pl.BlockSpec((CHUNK, GROUP_W), lambda i: (i, 0)),
        out_shape=jax.ShapeDtypeStruct((nt, GROUP_W), F32),
        compiler_params=_cparams(("parallel",)),
        name="chunk_mlp",
    )(za, g, b, ws16, bs_full)


def _attn_kernel(*refs, tq, wk, scale, has_q2, has_extra, bias_heads, has_sink, start_fn):
    refs = list(refs)
    sink_ref = refs.pop(0) if has_sink else None
    q1_ref = refs.pop(0)
    q2_ref = refs.pop(0) if has_q2 else None
    k1_ref = refs.pop(0)
    k2_ref = refs.pop(0) if has_q2 else None
    v_ref = refs.pop(0)
    if has_extra:
        xk1_ref = refs.pop(0)
        xk2_ref = refs.pop(0) if has_q2 else None
        xv_ref = refs.pop(0)
    bias_ref = refs.pop(0) if bias_heads else None
    o_ref = refs.pop(0)

    def stack_heads(q, width):
        lane_head = lax.broadcasted_iota(jnp.int32, (1, q.shape[1]), 1) // width
        return jnp.concatenate([jnp.where(lane_head == hd, q, jnp.zeros_like(q)) for hd in range(N_HEADS)], axis=0)

    q1s = stack_heads(q1_ref[...], HEAD_DIM)
    q2s = stack_heads(q2_ref[...], MLA_ROPE) if has_q2 else None

    start = start_fn(pl.program_id(1))
    if isinstance(start, int):
        win = pl.ds(start, wk)
    else:
        win = pl.ds(pl.multiple_of(start, 64), wk)
    s = _dot_nt(q1s, k1_ref[win, :])
    if has_q2:
        s = s + _dot_nt(q2s, k2_ref[win, :])
    s = s * scale
    if bias_heads == N_HEADS:
        s = s + bias_ref[...]
    elif bias_heads == 1:
        b = bias_ref[...]
        s = s + jnp.concatenate([b] * N_HEADS, axis=0)
    m = jnp.max(s, axis=-1, keepdims=True)
    if has_extra:
        sx = _dot_nt(q1s, xk1_ref[...])
        if has_q2:
            sx = sx + _dot_nt(q2s, xk2_ref[...])
        sx = sx * scale
        m = jnp.maximum(m, jnp.max(sx, axis=-1, keepdims=True))
    if has_sink:
        sink = jnp.concatenate([jnp.full((tq, 1), sink_ref[hd], F32) for hd in range(N_HEADS)], axis=0)
        m = jnp.maximum(m, sink)
    p = jnp.exp(s - m)
    denom = jnp.sum(p, axis=-1, keepdims=True)
    o = _dot(p.astype(BF16), v_ref[win, :])
    if has_extra:
        px = jnp.exp(sx - m)
        denom = denom + jnp.sum(px, axis=-1, keepdims=True)
        o = o + _dot(px.astype(BF16), xv_ref[...])
    if has_sink:
        denom = denom + jnp.exp(sink - m)
    o = o / denom
    lane_head = lax.broadcasted_iota(jnp.int32, (1, N_HEADS * HEAD_DIM), 1) // HEAD_DIM
    out = jnp.zeros((tq, N_HEADS * HEAD_DIM), F32)
    for hd in range(N_HEADS):
        out = out + jnp.where(lane_head == hd, o[hd * tq:(hd + 1) * tq], 0.0)
    o_ref[...] = out


def _attn_call(name, grid, tq, wk, scale, q_index, q1, k1, v, k_spec_fn, *, q2=None, k2=None,
               extra=None, bias=None, bias_index=None, sinks=None, start_fn=lambda j: 0):
    has_q2 = q2 is not None
    has_extra = extra is not None
    bias_heads = 0 if bias is None else bias.shape[1] // tq
    args, specs = [], []
    if sinks is not None:
        args.append(sinks)
        specs.append(pl.BlockSpec(memory_space=pltpu.SMEM))
    args.append(q1)
    specs.append(pl.BlockSpec((tq, 256), lambda b, j: (q_index(b, j), 0)))
    if has_q2:
        args.append(q2)
        specs.append(pl.BlockSpec((tq, 128), lambda b, j: (q_index(b, j), 0)))
    args.append(k1)
    specs.append(k_spec_fn(256))
    if has_q2:
        args.append(k2)
        specs.append(k_spec_fn(128))
    args.append(v)
    specs.append(k_spec_fn(256))
    if has_extra:
        for a in extra:
            args.append(a)
            specs.append(pl.BlockSpec((None,) + a.shape[1:], lambda b, j: (b, 0, 0)))
    if bias is not None:
        args.append(bias)
        specs.append(pl.BlockSpec((None,) + bias.shape[1:], lambda b, j: (bias_index(j), 0, 0)))
    kern = functools.partial(_attn_kernel, tq=tq, wk=wk, scale=scale, has_q2=has_q2, has_extra=has_extra,
                             bias_heads=bias_heads, has_sink=sinks is not None, start_fn=start_fn)
    return pl.pallas_call(
        kern,
        grid=grid,
        in_specs=specs,
        out_specs=pl.BlockSpec((tq, 256), lambda b, j: (b * grid[1] + j, 0)),
        out_shape=jax.ShapeDtypeStruct((grid[0] * grid[1] * tq, 256), F32),
        compiler_params=_cparams(("parallel", "parallel")),
        name=name,
    )(*args)


def _merge_kernel(oa_ref, ob_ref, oc_ref, od_ref, x_ref, mod_ref, g_ref, w_ref, lg_ref, lb_ref, o_ref):
    m = mod_ref[0]
    acc = None
    for gi, r in enumerate((oa_ref, ob_ref, oc_ref, od_ref)):
        og = _rms_norm(r[...], g_ref[:, gi * GROUP_W:(gi + 1) * GROUP_W]).astype(BF16)
        part = _dot(og, w_ref[gi * GROUP_W:(gi + 1) * GROUP_W, :])
        acc = part if acc is None else acc + part
    y = ALPHA * x_ref[...] + m[2:3] * acc
    o_ref[...] = _layer_norm(y, lg_ref[...], lb_ref[...])


def _merge_call(oa, ob, oc, od, x, mod, gout, wout16, lg, lb, n_ctx, lat_len):
    nt = x.shape[0]
    cond = functools.partial(_cond_of_tile, tile=TM, n_ctx=n_ctx, lat_len=lat_len)
    row = lambda w: pl.BlockSpec((TM, w), lambda i: (i, 0))
    const = lambda a: pl.BlockSpec(a.shape, lambda i: (0,) * a.ndim)
    return pl.pallas_call(
        _merge_kernel,
        grid=(nt // TM,),
        in_specs=[row(256), row(256), row(256), row(256), row(D_MODEL),
                  pl.BlockSpec((1, 6, D_MODEL), lambda i: (cond(i), 0, 0)),
                  const(gout), const(wout16), const(lg), const(lb)],
        out_specs=row(D_MODEL),
        out_shape=jax.ShapeDtypeStruct((nt, D_MODEL), F32),
        compiler_params=_cparams(("parallel",)),
        name="merge_out",
    )(oa, ob, oc, od, x, mod, gout, wout16, lg, lb)


_N_TOP = PEER_TOPK + 1


def _top_values(s, n_top):
    rows = lax.broadcasted_iota(jnp.int32, s.shape, 0).astype(F32)
    out = []
    for _ in range(n_top):
        m = jnp.max(s, axis=0, keepdims=True)
        first = jnp.min(jnp.where(s == m, rows, float(s.shape[0])), axis=0, keepdims=True)
        s = jnp.where(rows == first, -jnp.inf, s)
        out.append(m)
    return out


def _route_kernel(x_ref, mod_ref, wq_ref, keys_ref, h2_ref, p0_ref, t0_ref, s1_ref, e1_ref, st_scr):
    m = mod_ref[0]
    h2 = (x_ref[...] * (1.0 + m[4:5]) + m[3:4]).astype(BF16)
    h2_ref[...] = h2
    q = _dot(h2, wq_ref[...])
    tb = q.shape[0]
    for p in range(2):
        kp = keys_ref[p]
        k_hi = kp.astype(BF16)
        k_lo = (kp - k_hi.astype(F32)).astype(BF16)
        for hd in range(PEER_HEADS):
            c0 = (hd * 2 + p) * PEER_KEYS
            qs = q[:, c0:c0 + PEER_KEYS]
            q_hi = qs.astype(BF16)
            q_lo = (qs - q_hi.astype(F32)).astype(BF16)
            s = _dot_nt(q_hi, k_hi) + _dot_nt(q_hi, k_lo) + _dot_nt(q_lo, k_hi)
            st_scr[hd * 2 + p] = s.T

    neg = jnp.full((8, tb), -jnp.inf, F32)
    row8 = lax.broadcasted_iota(jnp.int32, (8, tb), 0)

    def per_head(hd, carry):
        s0 = st_scr[hd * 2]
        s1 = st_scr[hd * 2 + 1]
        sv0 = _top_values(s0, _N_TOP)
        sv1 = _top_values(s1, _N_TOP)
        sv1p = jnp.concatenate(sv1 + [neg[0:7]], axis=0)
        sv0p = jnp.concatenate(sv0[8:] + [neg[0:7]], axis=0)
        blocks = [sv0[0] + sv1p]
        for a in range(1, 8):
            nb = _N_TOP // (a + 1)
            blocks.append(jnp.where(row8 < nb, sv0[a] + sv1p[0:8], -jnp.inf))
        blocks.append(sv0p + sv1[0])
        cand = jnp.concatenate(blocks, axis=0)
        c = _top_values(cand, _N_TOP)
        thr = 0.5 * (c[PEER_TOPK - 1] + c[PEER_TOPK])
        z = jnp.zeros_like(thr)
        for k in range(PEER_TOPK):
            z = z + jnp.exp(c[k] - c[0])
        p0_ref[hd] = jnp.exp(s0 - sv0[0]) / z
        t0_ref[hd] = thr - s0
        s1_ref[hd] = s1
        e1_ref[hd] = jnp.exp(s1 - sv1[0])
        return carry

    lax.fori_loop(0, PEER_HEADS, per_head, 0)


def _route_call(x1, mod, wq16, keys, n_ctx, lat_len):
    nt = x1.shape[0]
    tb = TB_ROUTE
    cond = functools.partial(_cond_of_tile, tile=tb, n_ctx=n_ctx, lat_len=lat_len)
    fac = lambda: pl.BlockSpec((PEER_HEADS, PEER_KEYS, tb), lambda i: (0, 0, i))
    fshape = jax.ShapeDtypeStruct((PEER_HEADS, PEER_KEYS, nt), F32)
    return pl.pallas_call(
        _route_kernel,
        grid=(nt // tb,),
        in_specs=[pl.BlockSpec((tb, D_MODEL), lambda i: (i, 0)),
                  pl.BlockSpec((1, 6, D_MODEL), lambda i: (cond(i), 0, 0)),
                  pl.BlockSpec(wq16.shape, lambda i: (0, 0)),
                  pl.BlockSpec(keys.shape, lambda i: (0, 0, 0))],
        out_specs=[pl.BlockSpec((tb, D_MODEL), lambda i: (i, 0)), fac(), fac(), fac(), fac()],
        out_shape=[jax.ShapeDtypeStruct((nt, D_MODEL), BF16), fshape, fshape, fshape, fshape],
        scratch_shapes=[pltpu.VMEM((2 * PEER_HEADS, PEER_KEYS, tb), F32)],
        compiler_params=_cparams(("parallel",)),
        name="peer_route",
    )(x1, mod, wq16, keys)


def _expert_kernel(h2_ref, u_ref, vt_ref, p0_ref, t0_ref, s1_ref, e1_ref, x_ref, mod_ref, lg_ref, lb_ref,
                   o_ref, acc_scr, g_scr):
    c = pl.program_id(1)

    @pl.when(c == 0)
    def _():
        acc_scr[...] = jnp.zeros_like(acc_scr)

    st = _dot_nt(u_ref[...], h2_ref[...])
    act = jax.nn.gelu(st)
    for sl in range(N_SLAB):
        w = None
        for hd in range(PEER_HEADS):
            thr = t0_ref[hd, 0, sl:sl + 1, :]
            e0 = p0_ref[hd, 0, sl:sl + 1, :]
            term = e0 * jnp.where(s1_ref[hd] > thr, e1_ref[hd], 0.0)
            w = term if w is None else w + term
        g_scr[sl * PEER_KEYS:(sl + 1) * PEER_KEYS, :] = (w * act[sl * PEER_KEYS:(sl + 1) * PEER_KEYS]).astype(BF16)
    acc_scr[...] += _dot(vt_ref[...], g_scr[...])

    @pl.when(c == pl.num_programs(1) - 1)
    def _():
        m = mod_ref[0]
        y = ALPHA * x_ref[...] + m[5:6] * acc_scr[...].T
        o_ref[...] = _layer_norm(y, lg_ref[...], lb_ref[...])


def _expert_call(h2, u16, vt16, p0, t0, s1, e1, x1, mod, lg, lb, n_ctx, lat_len):
    nt = x1.shape[0]
    tb = TB_EXP
    n_chunks = N_EXPERTS // EC
    cond = functools.partial(_cond_of_tile, tile=tb, n_ctx=n_ctx, lat_len=lat_len)
    p0r = p0.reshape(PEER_HEADS, n_chunks, N_SLAB, nt)
    t0r = t0.reshape(PEER_HEADS, n_chunks, N_SLAB, nt)
    slab = lambda: pl.BlockSpec((PEER_HEADS, 1, N_SLAB, tb), lambda i, c: (0, c, 0, i))
    full = lambda: pl.BlockSpec((PEER_HEADS, PEER_KEYS, tb), lambda i, c: (0, 0, i))
    const = lambda a: pl.BlockSpec(a.shape, lambda i, c: (0,) * a.ndim)
    return pl.pallas_call(
        _expert_kernel,
        grid=(nt // tb, n_chunks),
        in_specs=[pl.BlockSpec((tb, D_MODEL), lambda i, c: (i, 0)),
                  pl.BlockSpec((EC, D_MODEL), lambda i, c: (c, 0)),
                  pl.BlockSpec((D_MODEL, EC), lambda i, c: (0, c)),
                  slab(), slab(), full(), full(),
                  pl.BlockSpec((tb, D_MODEL), lambda i, c: (i, 0)),
                  pl.BlockSpec((1, 6, D_MODEL), lambda i, c: (cond(i), 0, 0)),
                  const(lg), const(lb)],
        out_specs=pl.BlockSpec((tb, D_MODEL), lambda i, c: (i, 0)),
        out_shape=jax.ShapeDtypeStruct((nt, D_MODEL), F32),
        scratch_shapes=[pltpu.VMEM((D_MODEL, tb), F32), pltpu.VMEM((EC, tb), BF16)],
        compiler_params=_cparams(("parallel", "arbitrary")),
        name="peer_experts",
    )(h2, u16, vt16, p0r, t0r, s1, e1, x1, mod, lg, lb)


def _swap_halves(w, n_heads, rot):
    k = w.shape[0]
    w4 = w.reshape(k, n_heads, 2, rot // 2)
    return jnp.concatenate([w4[:, :, 1:2], w4[:, :, 0:1]], axis=2).reshape(k, n_heads * rot)


def _inproj_weight(w):
    a = w[:, 0:512]
    qb, kb, vb = w[:, 512:768], w[:, 768:1024], w[:, 1024:1280]
    cq, ckv, kr = w[:, 1280:1536], w[:, 1536:1664], w[:, 1664:1696]
    qd, kd, vd = w[:, 1696:1952], w[:, 1952:2080], w[:, 2080:2208]
    rep = lambda m: jnp.repeat(m.reshape(D_MODEL, 2, HEAD_DIM), 2, axis=1).reshape(D_MODEL, 256)
    kr4 = jnp.tile(kr, (1, N_HEADS))
    kdr = rep(kd)
    cols = [a, qb, kb, vb, cq, ckv, kr4, _swap_halves(kr4, N_HEADS, MLA_ROPE),
            qd, _swap_halves(qd, N_HEADS, HEAD_DIM), kdr, _swap_halves(kdr, N_HEADS, HEAD_DIM), rep(vd)]
    return jnp.concatenate(cols, axis=1).astype(BF16)


def _mla_weights(w_uq, w_ukv):
    q3 = w_uq.reshape(MLA_Q_RANK, N_HEADS, MLA_NOPE + MLA_ROPE)
    nope = q3[:, :, :MLA_NOPE].reshape(MLA_Q_RANK, N_HEADS * MLA_NOPE)
    rope = q3[:, :, MLA_NOPE:].reshape(MLA_Q_RANK, N_HEADS * MLA_ROPE)
    wuq = jnp.concatenate([nope, rope, _swap_halves(rope, N_HEADS, MLA_ROPE)], axis=1).astype(BF16)
    kv3 = w_ukv.reshape(MLA_KV_RANK, N_HEADS, MLA_NOPE + 64)
    wukv = jnp.concatenate([kv3[:, :, :MLA_NOPE].reshape(MLA_KV_RANK, 256),
                            kv3[:, :, MLA_NOPE:].reshape(MLA_KV_RANK, 256)], axis=1).astype(BF16)
    return wuq, wukv


def _rope_tables(lat_len, rot, tile):
    t = jnp.arange(lat_len)
    row = (t // GRID_W).astype(F32)
    col = (t % GRID_W).astype(F32)
    nf = rot // 4
    freqs = ROPE_BASE ** (-jnp.arange(nf, dtype=F32) / nf)
    ang = jnp.concatenate([row[:, None] * freqs, col[:, None] * freqs], -1)
    cos, sin = jnp.cos(ang), jnp.sin(ang)
    cos_t = jnp.tile(jnp.concatenate([cos, cos], -1), (1, N_HEADS))
    sin_t = jnp.tile(jnp.concatenate([-sin, sin], -1), (1, N_HEADS))
    w = N_HEADS * rot
    return (jnp.concatenate([jnp.ones((tile, w), F32), cos_t], 0),
            jnp.concatenate([jnp.zeros((tile, w), F32), sin_t], 0))


def _na_bias_tables(rpb, rows):
    kh = min(NA_KH, rows)
    qrow = np.array([0, 1, 2, 3, rows // 2, rows - 3, rows - 2, rows - 1])
    start = np.clip(qrow - kh // 2, 0, rows - kh)
    dr = start[:, None] + np.arange(kh)[None, :] - qrow[:, None] + NA_KH - 1
    qc = np.arange(GRID_W)
    kc = np.arange(GRID_W)
    dc = np.clip(kc[None, :] - qc[:, None] + NA_KW - 1, 0, 2 * NA_KW - 2)
    cstart = np.clip(qc - NA_KW // 2, 0, GRID_W - NA_KW)
    ok = (kc[None, :] >= cstart[:, None]) & (kc[None, :] < cstart[:, None] + NA_KW)
    b = rpb[:, dr[:, None, :, None], dc[None, :, None, :]]
    b = jnp.where(ok[None, None, :, None, :], b, NEG_INF)
    return b.transpose(1, 0, 2, 3, 4).reshape(8, N_HEADS * GRID_W, kh * GRID_W)


def _swa_bias_tables(lat_len):
    nb = lat_len // SWA_WIN
    qi = np.arange(SWA_WIN)
    kj = np.arange(3 * SWA_WIN)
    in_win = np.abs(kj[None, :] - SWA_WIN - qi[:, None]) <= SWA_WIN
    tabs = []
    for n in (0, 1, nb - 1):
        kpos = n * SWA_WIN - SWA_WIN + kj
        in_seq = (kpos >= 0) & (kpos < lat_len)
        tabs.append(np.where(in_win & in_seq[None, :], 0.0, NEG_INF))
    return jnp.asarray(np.stack(tabs), F32)


def _heads_to_lanes(t):
    b, h, s, d = t.shape
    return t.transpose(0, 2, 1, 3).reshape(b, s, h * d)


def _lanes_to_heads(t, b, s, h):
    return t.reshape(b, s, h, -1).transpose(0, 2, 1, 3)


def kernel(x_prompt, x_sample, cache_nat_k, cache_nat_v, cache_mla_ckv, cache_mla_krope, cache_swa_k, cache_swa_v, c, c_ctx, w_in, w_out, out_norm_g, w_ada, b_ada, ln1_g, ln1_b, ln2_g, ln2_b, a_norm_g, a_norm_b, a_w_s, a_b_s, nat_rpb, mla_q_norm_g, mla_w_uq, mla_kv_norm_g, mla_w_ukv, swa_sinks, peer_w_q, peer_sub_keys, peer_u, peer_v):
    n_b, seq, d = x_prompt.shape
    n_db, lat_len, _ = x_sample.shape
    past = cache_nat_k.shape[3]
    n_ctx = n_b * seq
    nt = n_ctx + n_db * lat_len
    rows = lat_len // GRID_W

    x = jnp.concatenate([x_prompt.reshape(n_ctx, d), x_sample.reshape(n_db * lat_len, d)], axis=0)
    conds = jnp.zeros((8, d), F32).at[0].set(c_ctx).at[1:1 + n_db].set(c)
    mods = _ada_call(conds, w_ada, b_ada).reshape(DEPTH, 8, 6, d)

    cos_c, sin_c = _rope_tables(lat_len, MLA_ROPE, TM)
    cos_d, sin_d = _rope_tables(lat_len, HEAD_DIM, TM)
    swa_bias = _swa_bias_tables(lat_len)
    nb_swa = lat_len // SWA_WIN

    q_ctx = lambda b, j: b
    lat_q = lambda tq: (lambda b, j: (n_ctx + b * lat_len) // tq + j)
    ctx_keys = lambda w: pl.BlockSpec((seq, w), lambda b, j: (b, 0))
    lat_keys = lambda w: pl.BlockSpec((lat_len, w), lambda b, j: (n_ctx // lat_len + b, 0))

    states = [[] for _ in range(6)]
    for l in range(DEPTH):
        mod = mods[l]
        wbig = _inproj_weight(w_in[l])
        wuq, wukv = _mla_weights(mla_w_uq[l], mla_w_ukv[l])
        (za, qb, kb, vb, kb16, vb16, qcn, qcr, ckvn, krc, krc16, kcn, vc,
         qd, kd, vd, kd16, vd16) = _inproj_call(
            x, mod, wbig, wuq, wukv, mla_q_norm_g[l][None], mla_kv_norm_g[l][None],
            (cos_c, sin_c, cos_d, sin_d), n_ctx, lat_len)

        bs_full = jnp.repeat(a_b_s[l].T, HEAD_DIM, axis=1)
        oa = _chunk_mlp_call(za, a_norm_g[l][None], a_norm_b[l][None], a_w_s[l].astype(BF16), bs_full)

        ob_c = _attn_call("ctx_attn_b", (n_b, 1), seq, seq, HEAD_DIM ** -0.5, q_ctx, qb, kb16, vb16, ctx_keys)
        oc_c = _attn_call("ctx_attn_c", (n_b, 1), seq, seq, MLA_SCALE, q_ctx, qcn, kcn, vc, ctx_keys,
                          q2=qcr, k2=krc16)
        od_c = _attn_call("ctx_attn_d", (n_b, 1), seq, seq, HEAD_DIM ** -0.5, q_ctx, qd, kd16, vd16, ctx_keys,
                          sinks=swa_sinks[l])

        na_bias = _na_bias_tables(nat_rpb[l], rows)
        kh = min(NA_KH, rows)
        ob_l = _attn_call(
            "lat_attn_b", (n_db, rows), GRID_W, kh * GRID_W, HEAD_DIM ** -0.5, lat_q(GRID_W), qb, kb16, vb16, lat_keys,
            extra=(_heads_to_lanes(cache_nat_k[:, l]).astype(BF16), _heads_to_lanes(cache_nat_v[:, l]).astype(BF16)),
            bias=na_bias,
            bias_index=lambda r: jnp.where(r < 4, r, jnp.where(r > rows - 4, r - (rows - 8), 4)),
            start_fn=lambda r: jnp.clip(r - kh // 2, 0, rows - kh) * GRID_W)
        kxn, vx = _kvexp_call(cache_mla_ckv[:, l].reshape(n_db * past, MLA_KV_RANK), wukv)
        krx = jnp.tile(cache_mla_krope[:, l], (1, 1, N_HEADS)).astype(BF16)
        oc_l = _attn_call(
            "lat_attn_c", (n_db, lat_len // 128), 128, lat_len, MLA_SCALE, lat_q(128), qcn, kcn, vc, lat_keys,
            q2=qcr, k2=krc16,
            extra=(kxn.reshape(n_db, past, 256), krx, vx.reshape(n_db, past, 256)))
        pad = lambda t: jnp.pad(t[n_ctx:].reshape(n_db, lat_len, 256), ((0, 0), (SWA_WIN, SWA_WIN), (0, 0)))
        rep_kv = lambda t: jnp.repeat(t, 2, axis=1)
        od_l = _attn_call(
            "lat_attn_d", (n_db, nb_swa), SWA_WIN, 3 * SWA_WIN, HEAD_DIM ** -0.5, lat_q(SWA_WIN), qd,
            pad(kd16), pad(vd16),
            lambda w: pl.BlockSpec((None, lat_len + 2 * SWA_WIN, w), lambda b, j: (b, 0, 0)),
            extra=(_heads_to_lanes(rep_kv(cache_swa_k[:, l])).astype(BF16),
                   _heads_to_lanes(rep_kv(cache_swa_v[:, l])).astype(BF16)),
            bias=swa_bias,
            bias_index=lambda n: jnp.where(n == 0, 0, jnp.where(n == nb_swa - 1, 2, 1)),
            sinks=swa_sinks[l],
            start_fn=lambda n: n * SWA_WIN)
        ob = jnp.concatenate([ob_c, ob_l], axis=0)
        oc = jnp.concatenate([oc_c, oc_l], axis=0)
        od = jnp.concatenate([od_c, od_l], axis=0)

        x1 = _merge_call(oa, ob, oc, od, x, mod, out_norm_g[l][None], w_out[l].astype(BF16),
                         ln1_g[l][None], ln1_b[l][None], n_ctx, lat_len)
        h2, p0, t0, s1, e1 = _route_call(x1, mod, peer_w_q[l].astype(BF16), peer_sub_keys[l], n_ctx, lat_len)
        x = _expert_call(h2, peer_u[l].astype(BF16), peer_v[l].T.astype(BF16), p0, t0, s1, e1, x1, mod,
                         ln2_g[l][None], ln2_b[l][None], n_ctx, lat_len)

        states[0].append(_lanes_to_heads(kb[:n_ctx], n_b, seq, N_HEADS))
        states[1].append(_lanes_to_heads(vb[:n_ctx], n_b, seq, N_HEADS))
        states[2].append(ckvn[:n_ctx].reshape(n_b, seq, MLA_KV_RANK))
        states[3].append(krc[:n_ctx, :MLA_ROPE].reshape(n_b, seq, MLA_ROPE))
        states[4].append(_lanes_to_heads(kd[:n_ctx], n_b, seq, N_HEADS)[:, ::2])
        states[5].append(_lanes_to_heads(vd[:n_ctx], n_b, seq, N_HEADS)[:, ::2])

    y_prompt = x[:n_ctx].reshape(n_b, seq, d)
    y_sample = x[n_ctx:].reshape(n_db, lat_len, d)
    return (y_prompt, y_sample) + tuple(jnp.stack(s, axis=1) for s in states)
```

```python
import functools
import math

import jax
import jax.numpy as jnp
import numpy as np
from jax import lax
from jax.experimental import pallas as pl
from jax.experimental.pallas import tpu as pltpu

F32 = jnp.float32
BF16 = jnp.bfloat16

D_MODEL = 1024
DEPTH = 2
GRID_W = 64
HEAD_DIM = 64
N_HEADS = 4
GROUP_W = 256
CHUNK = 128
NA_KH = 8
NA_KW = 16
MLA_Q_RANK = 256
MLA_KV_RANK = 128
MLA_NOPE = 64
MLA_ROPE = 32
MLA_SCALE = (MLA_NOPE + MLA_ROPE) ** -0.5
SWA_WIN = 128
PEER_HEADS = 8
PEER_KEYS = 128
PEER_TOPK = 16
N_EXPERTS = PEER_KEYS * PEER_KEYS
ROPE_BASE = 10000.0
LN_EPS = 1e-5
NEG_INF = -1e30
ALPHA = (2 * DEPTH) ** 0.25

V7X_VMEM_LIMIT_BYTES = 56 * 1024 * 1024
TM = 256
TB_ROUTE = 256
TB_EXP = 512
EC = 1024
N_SLAB = EC // PEER_KEYS
EXP_PIECE = 256
N_CAND_ROWS = 96

_C_A = 0
_C_QB, _C_KB, _C_VB = 512, 768, 1024
_C_CQ, _C_CKV, _C_KR, _C_KRS = 1280, 1536, 1664, 1792
_C_QD, _C_QDS, _C_KD, _C_KDS, _C_VD = 1920, 2176, 2432, 2688, 2944
_C_END = 3200


def _cparams(sem):
    return pltpu.CompilerParams(dimension_semantics=sem, vmem_limit_bytes=V7X_VMEM_LIMIT_BYTES)


def _dot(a, b):
    return jnp.dot(a, b, preferred_element_type=F32)


def _dot_nt(a, b):
    return lax.dot_general(a, b, (((1,), (1,)), ((), ())), preferred_element_type=F32)


def _layer_norm(x, g, b):
    mu = jnp.mean(x, axis=-1, keepdims=True)
    xc = x - mu
    var = jnp.mean(xc * xc, axis=-1, keepdims=True)
    return xc * lax.rsqrt(var + LN_EPS) * g + b


def _rms_norm(x, g):
    return x * lax.rsqrt(jnp.mean(x * x, axis=-1, keepdims=True) + LN_EPS) * g


def _ada_kernel(c_ref, w_ref, b_ref, o_ref):
    c = c_ref[...]
    a = c * jax.nn.sigmoid(c)
    a_hi = a.astype(BF16)
    a_lo = (a - a_hi.astype(F32)).astype(BF16)
    w = w_ref[0]
    w_hi = w.astype(BF16)
    w_lo = (w - w_hi.astype(F32)).astype(BF16)
    o_ref[0] = _dot(a_hi, w_hi) + _dot(a_hi, w_lo) + _dot(a_lo, w_hi) + b_ref[0]


def _ada_call(conds, w_ada, b_ada):
    tn = 1536
    n = w_ada.shape[-1]
    return pl.pallas_call(
        _ada_kernel,
        grid=(DEPTH, n // tn),
        in_specs=[pl.BlockSpec((8, D_MODEL), lambda l, j: (0, 0)),
                  pl.BlockSpec((1, D_MODEL, tn), lambda l, j: (l, 0, j)),
                  pl.BlockSpec((1, 1, tn), lambda l, j: (l, 0, j))],
        out_specs=pl.BlockSpec((1, 8, tn), lambda l, j: (l, 0, j)),
        out_shape=jax.ShapeDtypeStruct((DEPTH, 8, n), F32),
        compiler_params=_cparams(("parallel", "parallel")),
        name="ada_mod",
    )(conds, w_ada, b_ada.reshape(DEPTH, 1, n))


def _inproj_kernel(x_ref, mod_ref, w_ref, wuq_ref, wukv_ref, gq_ref, gkv_ref,
                   cosc_ref, sinc_ref, cosd_ref, sind_ref,
                   za_ref, qb_ref, kb_ref, vb_ref, kb16_ref, vb16_ref,
                   qcn_ref, qcr_ref, ckvn_ref, krc_ref, krc16_ref, kcn_ref, vc_ref,
                   qd_ref, kd_ref, vd_ref, kd16_ref, vd16_ref):
    m = mod_ref[0]
    h = x_ref[...] * (1.0 + m[1:2]) + m[0:1]
    z = _dot(h.astype(BF16), w_ref[...])
    cosc, sinc = cosc_ref[...], sinc_ref[...]
    cosd, sind = cosd_ref[...], sind_ref[...]

    za_ref[...] = z[:, _C_A:_C_QB]
    qb_ref[...] = z[:, _C_QB:_C_KB].astype(BF16)
    kb = z[:, _C_KB:_C_VB]
    vb = z[:, _C_VB:_C_CQ]
    kb_ref[...] = kb
    vb_ref[...] = vb
    kb16_ref[...] = kb.astype(BF16)
    vb16_ref[...] = vb.astype(BF16)

    cqn = _rms_norm(z[:, _C_CQ:_C_CKV], gq_ref[...])
    q = _dot(cqn.astype(BF16), wuq_ref[...])
    qcn_ref[...] = q[:, 0:256].astype(BF16)
    qcr_ref[...] = (q[:, 256:384] * cosc + q[:, 384:512] * sinc).astype(BF16)
    ckvn = _rms_norm(z[:, _C_CKV:_C_KR], gkv_ref[...])
    ckvn_ref[...] = ckvn
    kv = _dot(ckvn.astype(BF16), wukv_ref[...])
    kcn_ref[...] = kv[:, 0:256].astype(BF16)
    vc_ref[...] = kv[:, 256:512].astype(BF16)
    kr = z[:, _C_KR:_C_KRS] * cosc + z[:, _C_KRS:_C_QD] * sinc
    krc_ref[...] = kr
    krc16_ref[...] = kr.astype(BF16)

    qd_ref[...] = (z[:, _C_QD:_C_QDS] * cosd + z[:, _C_QDS:_C_KD] * sind).astype(BF16)
    kd = z[:, _C_KD:_C_KDS] * cosd + z[:, _C_KDS:_C_VD] * sind
    vd = z[:, _C_VD:_C_END]
    kd_ref[...] = kd
    vd_ref[...] = vd
    kd16_ref[...] = kd.astype(BF16)
    vd16_ref[...] = vd.astype(BF16)


def _cond_of_tile(i, tile, n_ctx, lat_len):
    n_ctx_tiles = n_ctx // tile
    return jnp.where(i < n_ctx_tiles, 0, 1 + (i - n_ctx_tiles) // (lat_len // tile))


def _rope_block_of_tile(i, tile, n_ctx, lat_len):
    n_ctx_tiles = n_ctx // tile
    return jnp.where(i < n_ctx_tiles, 0, 1 + (i - n_ctx_tiles) % (lat_len // tile))


def _inproj_call(x, mod, wbig, wuq, wukv, gq, gkv, tabs, n_ctx, lat_len):
    nt = x.shape[0]
    cond = functools.partial(_cond_of_tile, tile=TM, n_ctx=n_ctx, lat_len=lat_len)
    rblk = functools.partial(_rope_block_of_tile, tile=TM, n_ctx=n_ctx, lat_len=lat_len)
    row = lambda w: pl.BlockSpec((TM, w), lambda i: (i, 0))
    const = lambda a: pl.BlockSpec(a.shape, lambda i: (0,) * a.ndim)
    tab = lambda w: pl.BlockSpec((TM, w), lambda i: (rblk(i), 0))
    outs = [(512, F32), (256, BF16), (256, F32), (256, F32), (256, BF16), (256, BF16),
            (256, BF16), (128, BF16), (128, F32), (128, F32), (128, BF16), (256, BF16), (256, BF16),
            (256, BF16), (256, F32), (256, F32), (256, BF16), (256, BF16)]
    return pl.pallas_call(
        _inproj_kernel,
        grid=(nt // TM,),
        in_specs=[row(D_MODEL),
                  pl.BlockSpec((1, 6, D_MODEL), lambda i: (cond(i), 0, 0)),
                  const(wbig), const(wuq), const(wukv), const(gq), const(gkv),
                  tab(128), tab(128), tab(256), tab(256)],
        out_specs=[row(w) for w, _ in outs],
        out_shape=[jax.ShapeDtypeStruct((nt, w), dt) for w, dt in outs],
        compiler_params=_cparams(("parallel",)),
        name="inproj",
    )(x, mod, wbig, wuq, wukv, gq, gkv, *tabs)


def _kvexp_kernel(c_ref, w_ref, k_ref, v_ref):
    kv = _dot(c_ref[...].astype(BF16), w_ref[...])
    k_ref[...] = kv[:, 0:256].astype(BF16)
    v_ref[...] = kv[:, 256:512].astype(BF16)


def _kvexp_call(ckv, wukv):
    n = ckv.shape[0]
    return pl.pallas_call(
        _kvexp_kernel,
        grid=(1,),
        in_specs=[pl.BlockSpec(ckv.shape, lambda i: (0, 0)), pl.BlockSpec(wukv.shape, lambda i: (0, 0))],
        out_specs=[pl.BlockSpec((n, 256), lambda i: (0, 0))] * 2,
        out_shape=[jax.ShapeDtypeStruct((n, 256), BF16)] * 2,
        compiler_params=_cparams(("arbitrary",)),
        name="mla_cache_expand",
    )(ckv, wukv)


def _chunk_mlp_kernel(z_ref, g_ref, b_ref, ws_ref, bs_ref, o_ref):
    g = jax.nn.gelu(z_ref[...])
    u = g[:, 0:GROUP_W]
    v = _layer_norm(g[:, GROUP_W:2 * GROUP_W], g_ref[...], b_ref[...]).astype(BF16)
    lane_head = lax.broadcasted_iota(jnp.int32, (1, GROUP_W), 1) // HEAD_DIM
    mixed = bs_ref[...]
    for hd in range(N_HEADS):
        mixed = mixed + jnp.where(lane_head == hd, _dot(ws_ref[hd], v), 0.0)
    o_ref[...] = u * mixed


def _chunk_mlp_call(za, g, b, ws16, bs_full):
    nt = za.shape[0]
    return pl.pallas_call(
        _chunk_mlp_kernel,
        grid=(nt // CHUNK,),
        in_specs=[pl.BlockSpec((CHUNK, 2 * GROUP_W), lambda i: (i, 0)),
                  pl.BlockSpec((1, GROUP_W), lambda i: (0, 0)),
                  pl.BlockSpec((1, GROUP_W), lambda i: (0, 0)),
                  pl.BlockSpec((N_HEADS, CHUNK, CHUNK), lambda i: (0, 0, 0)),
                  pl.BlockSpec((CHUNK, GROUP_W), lambda i: (0, 0))],
        out_specs=pl.BlockSpec((CHUNK, GROUP_W), lambda i: (i, 0)),
        out_shape=jax.ShapeDtypeStruct((nt, GROUP_W), F32),
        compiler_params=_cparams(("parallel",)),
        name="chunk_mlp",
    )(za, g, b, ws16, bs_full)


def _attn_kernel(*refs, tq, wk, scale, has_q2, has_extra, bias_heads, has_sink, start_fn):
    refs = list(refs)
    sink_ref = refs.pop(0) if has_sink else None
    q1_ref = refs.pop(0)
    q2_ref = refs.pop(0) if has_q2 else None
    k1_ref = refs.pop(0)
    k2_ref = refs.pop(0) if has_q2 else None
    v_ref = refs.pop(0)
    if has_extra:
        xk1_ref = refs.pop(0)
        xk2_ref = refs.pop(0) if has_q2 else None
        xv_ref = refs.pop(0)
    bias_ref = refs.pop(0) if bias_heads else None
    o_ref = refs.pop(0)

    def stack_heads(q, width):
        lane_head = lax.broadcasted_iota(jnp.int32, (1, q.shape[1]), 1) // width
        return jnp.concatenate([jnp.where(lane_head == hd, q, jnp.zeros_like(q)) for hd in range(N_HEADS)], axis=0)

    q1s = stack_heads(q1_ref[...], HEAD_DIM)
    q2s = stack_heads(q2_ref[...], MLA_ROPE) if has_q2 else None

    start = start_fn(pl.program_id(1))
    if isinstance(start, int):
        win = pl.ds(start, wk)
    else:
        win = pl.ds(pl.multiple_of(start, 64), wk)
    s = _dot_nt(q1s, k1_ref[win, :])
    if has_q2:
        s = s + _dot_nt(q2s, k2_ref[win, :])
    s = s * scale
    if bias_heads == N_HEADS:
        s = s + bias_ref[...]
    elif bias_heads == 1:
        b = bias_ref[...]
        s = s + jnp.concatenate([b] * N_HEADS, axis=0)
    m = jnp.max(s, axis=-1, keepdims=True)
    if has_extra:
        sx = _dot_nt(q1s, xk1_ref[...])
        if has_q2:
            sx = sx + _dot_nt(q2s, xk2_ref[...])
        sx = sx * scale
        m = jnp.maximum(m, jnp.max(sx, axis=-1, keepdims=True))
    if has_sink:
        sink = jnp.concatenate([jnp.full((tq, 1), sink_ref[hd], F32) for hd in range(N_HEADS)], axis=0)
        m = jnp.maximum(m, sink)
    p = jnp.exp(s - m)
    denom = jnp.sum(p, axis=-1, keepdims=True)
    o = _dot(p.astype(BF16), v_ref[win, :])
    if has_extra:
        px = jnp.exp(sx - m)
        denom = denom + jnp.sum(px, axis=-1, keepdims=True)
        o = o + _dot(px.astype(BF16), xv_ref[...])
    if has_sink:
        denom = denom + jnp.exp(sink - m)
    o = o / denom
    lane_head = lax.broadcasted_iota(jnp.int32, (1, N_HEADS * HEAD_DIM), 1) // HEAD_DIM
    out = jnp.zeros((tq, N_HEADS * HEAD_DIM), F32)
    for hd in range(N_HEADS):
        out = out + jnp.where(lane_head == hd, o[hd * tq:(hd + 1) * tq], 0.0)
    o_ref[...] = out


def _attn_call(name, grid, tq, wk, scale, q_index, q1, k1, v, k_spec_fn, *, q2=None, k2=None,
               extra=None, bias=None, bias_index=None, sinks=None, start_fn=lambda j: 0):
    has_q2 = q2 is not None
    has_extra = extra is not None
    bias_heads = 0 if bias is None else bias.shape[1] // tq
    args, specs = [], []
    if sinks is not None:
        args.append(sinks)
        specs.append(pl.BlockSpec(memory_space=pltpu.SMEM))
    args.append(q1)
    specs.append(pl.BlockSpec((tq, 256), lambda b, j: (q_index(b, j), 0)))
    if has_q2:
        args.append(q2)
        specs.append(pl.BlockSpec((tq, 128), lambda b, j: (q_index(b, j), 0)))
    args.append(k1)
    specs.append(k_spec_fn(256))
    if has_q2:
        args.append(k2)
        specs.append(k_spec_fn(128))
    args.append(v)
    specs.append(k_spec_fn(256))
    if has_extra:
        for a in extra:
            args.append(a)
            specs.append(pl.BlockSpec((None,) + a.shape[1:], lambda b, j: (b, 0, 0)))
    if bias is not None:
        args.append(bias)
        specs.append(pl.BlockSpec((None,) + bias.shape[1:], lambda b, j: (bias_index(j), 0, 0)))
    kern = functools.partial(_attn_kernel, tq=tq, wk=wk, scale=scale, has_q2=has_q2, has_extra=has_extra,
                             bias_heads=bias_heads, has_sink=sinks is not None, start_fn=start_fn)
    return pl.pallas_call(
        kern,
        grid=grid,
        in_specs=specs,
        out_specs=pl.BlockSpec((tq, 256), lambda b, j: (b * grid[1] + j, 0)),
        out_shape=jax.ShapeDtypeStruct((grid[0] * grid[1] * tq, 256), F32),
        compiler_params=_cparams(("parallel", "parallel")),
        name=name,
    )(*args)


def _merge_kernel(oa_ref, ob_ref, oc_ref, od_ref, x_ref, mod_ref, g_ref, w_ref, lg_ref, lb_ref, o_ref):
    m = mod_ref[0]
    acc = None
    for gi, r in enumerate((oa_ref, ob_ref, oc_ref, od_ref)):
        og = _rms_norm(r[...], g_ref[:, gi * GROUP_W:(gi + 1) * GROUP_W]).astype(BF16)
        part = _dot(og, w_ref[gi * GROUP_W:(gi + 1) * GROUP_W, :])
        acc = part if acc is None else acc + part
    y = ALPHA * x_ref[...] + m[2:3] * acc
    o_ref[...] = _layer_norm(y, lg_ref[...], lb_ref[...])


def _merge_call(oa, ob, oc, od, x, mod, gout, wout16, lg, lb, n_ctx, lat_len):
    nt = x.shape[0]
    cond = functools.partial(_cond_of_tile, tile=TM, n_ctx=n_ctx, lat_len=lat_len)
    row = lambda w: pl.BlockSpec((TM, w), lambda i: (i, 0))
    const = lambda a: pl.BlockSpec(a.shape, lambda i: (0,) * a.ndim)
    return pl.pallas_call(
        _merge_kernel,
        grid=(nt // TM,),
        in_specs=[row(256), row(256), row(256), row(256), row(D_MODEL),
                  pl.BlockSpec((1, 6, D_MODEL), lambda i: (cond(i), 0, 0)),
                  const(gout), const(wout16), const(lg), const(lb)],
        out_specs=row(D_MODEL),
        out_shape=jax.ShapeDtypeStruct((nt, D_MODEL), F32),
        compiler_params=_cparams(("parallel",)),
        name="merge_out",
    )(oa, ob, oc, od, x, mod, gout, wout16, lg, lb)


_N_TOP = PEER_TOPK + 1


def _top_values(s, n_top):
    rows = lax.broadcasted_iota(jnp.int32, s.shape, 0).astype(F32)
    out = []
    for _ in range(n_top):
        m = jnp.max(s, axis=0, keepdims=True)
        first = jnp.min(jnp.where(s == m, rows, float(s.shape[0])), axis=0, keepdims=True)
        s = jnp.where(rows == first, -jnp.inf, s)
        out.append(m)
    return out


def _route_kernel(x_ref, mod_ref, wq_ref, keys_ref, h2_ref, p0_ref, te_ref, e1_ref, st_scr):
    m = mod_ref[0]
    h2 = (x_ref[...] * (1.0 + m[4:5]) + m[3:4]).astype(BF16)
    h2_ref[...] = h2
    q = _dot(h2, wq_ref[...])
    tb = q.shape[0]
    for p in range(2):
        kp = keys_ref[p]
        k_hi = kp.astype(BF16)
        k_lo = (kp - k_hi.astype(F32)).astype(BF16)
        for hd in range(PEER_HEADS):
            c0 = (hd * 2 + p) * PEER_KEYS
            qs = q[:, c0:c0 + PEER_KEYS]
            q_hi = qs.astype(BF16)
            q_lo = (qs - q_hi.astype(F32)).astype(BF16)
            s = _dot_nt(q_hi, k_hi) + _dot_nt(q_hi, k_lo) + _dot_nt(q_lo, k_hi)
            st_scr[hd * 2 + p] = s.T

    neg = jnp.full((8, tb), -jnp.inf, F32)
    row8 = lax.broadcasted_iota(jnp.int32, (8, tb), 0)

    def per_head(hd, carry):
        s0 = st_scr[hd * 2]
        s1 = st_scr[hd * 2 + 1]
        sv0 = _top_values(s0, _N_TOP)
        sv1 = _top_values(s1, _N_TOP)
        sv1p = jnp.concatenate(sv1 + [neg[0:7]], axis=0)
        sv0p = jnp.concatenate(sv0[8:] + [neg[0:7]], axis=0)
        blocks = [sv0[0] + sv1p]
        for a in range(1, 8):
            nb = _N_TOP // (a + 1)
            blocks.append(jnp.where(row8 < nb, sv0[a] + sv1p[0:8], -jnp.inf))
        blocks.append(sv0p + sv1[0])
        cand = jnp.concatenate(blocks, axis=0)
        c = _top_values(cand, _N_TOP)
        thr = 0.5 * (c[PEER_TOPK - 1] + c[PEER_TOPK])
        z = jnp.zeros_like(thr)
        for k in range(PEER_TOPK):
            z = z + jnp.exp(c[k] - c[0])
        p0_ref[hd] = jnp.exp(s0 - sv0[0]) / z
        te_ref[hd] = jnp.exp((thr - sv1[0]) - s0)
        e1_ref[hd] = jnp.exp(s1 - sv1[0])
        return carry

    lax.fori_loop(0, PEER_HEADS, per_head, 0)


def _route_call(x1, mod, wq16, keys, n_ctx, lat_len):
    nt = x1.shape[0]
    tb = TB_ROUTE
    cond = functools.partial(_cond_of_tile, tile=tb, n_ctx=n_ctx, lat_len=lat_len)
    fac = lambda: pl.BlockSpec((PEER_HEADS, PEER_KEYS, tb), lambda i: (0, 0, i))
    fshape = jax.ShapeDtypeStruct((PEER_HEADS, PEER_KEYS, nt), F32)
    return pl.pallas_call(
        _route_kernel,
        grid=(nt // tb,),
        in_specs=[pl.BlockSpec((tb, D_MODEL), lambda i: (i, 0)),
                  pl.BlockSpec((1, 6, D_MODEL), lambda i: (cond(i), 0, 0)),
                  pl.BlockSpec(wq16.shape, lambda i: (0, 0)),
                  pl.BlockSpec(keys.shape, lambda i: (0, 0, 0))],
        out_specs=[pl.BlockSpec((tb, D_MODEL), lambda i: (i, 0)), fac(), fac(), fac()],
        out_shape=[jax.ShapeDtypeStruct((nt, D_MODEL), BF16), fshape, fshape, fshape],
        scratch_shapes=[pltpu.VMEM((2 * PEER_HEADS, PEER_KEYS, tb), F32)],
        compiler_params=_cparams(("parallel",)),
        name="peer_route",
    )(x1, mod, wq16, keys)


def _expert_kernel(h2_ref, u_ref, vt_ref, p0_ref, te_ref, e1_ref, x_ref, mod_ref, lg_ref, lb_ref,
                   o_ref, acc_scr, act_scr):
    c = pl.program_id(1)
    n_lane = h2_ref.shape[0] // PEER_KEYS

    @pl.when(c == 0)
    def _():
        acc_scr[...] = jnp.zeros_like(acc_scr)

    act_scr[...] = jax.nn.gelu(_dot_nt(u_ref[...], h2_ref[...]))
    for pc in range(EC // EXP_PIECE):
        slabs = range(pc * EXP_PIECE // PEER_KEYS, (pc + 1) * EXP_PIECE // PEER_KEYS)
        g_lanes = []
        for ln in range(n_lane):
            lanes = slice(ln * PEER_KEYS, (ln + 1) * PEER_KEYS)
            w = [None] * len(slabs)
            for hd in range(PEER_HEADS):
                e1 = e1_ref[hd, :, lanes]
                for k, sl in enumerate(slabs):
                    te = te_ref[hd, 0, sl:sl + 1, lanes]
                    e0 = p0_ref[hd, 0, sl:sl + 1, lanes]
                    term = e0 * jnp.where(e1 > te, e1, 0.0)
                    w[k] = term if w[k] is None else w[k] + term
            g_lanes.append(jnp.concatenate(
                [(w[k] * act_scr[sl * PEER_KEYS:(sl + 1) * PEER_KEYS, lanes]).astype(BF16)
                 for k, sl in enumerate(slabs)], axis=0))
        g = jnp.concatenate(g_lanes, axis=1)
        acc_scr[...] += _dot(vt_ref[:, pc * EXP_PIECE:(pc + 1) * EXP_PIECE], g)

    @pl.when(c == pl.num_programs(1) - 1)
    def _():
        m = mod_ref[0]
        y = ALPHA * x_ref[...] + m[5:6] * acc_scr[...].T
        o_ref[...] = _layer_norm(y, lg_ref[...], lb_ref[...])


def _expert_call(h2, u16, vt16, p0, te, e1, x1, mod, lg, lb, n_ctx, lat_len):
    nt = x1.shape[0]
    tb = TB_EXP
    n_chunks = N_EXPERTS // EC
    cond = functools.partial(_cond_of_tile, tile=tb, n_ctx=n_ctx, lat_len=lat_len)
    p0r = p0.reshape(PEER_HEADS, n_chunks, N_SLAB, nt)
    ter = te.reshape(PEER_HEADS, n_chunks, N_SLAB, nt)
    slab = lambda: pl.BlockSpec((PEER_HEADS, 1, N_SLAB, tb), lambda i, c: (0, c, 0, i))
    full = lambda: pl.BlockSpec((PEER_HEADS, PEER_KEYS, tb), lambda i, c: (0, 0, i))
    const = lambda a: pl.BlockSpec(a.shape, lambda i, c: (0,) * a.ndim)
    return pl.pallas_call(
        _expert_kernel,
        grid=(nt // tb, n_chunks),
        in_specs=[pl.BlockSpec((tb, D_MODEL), lambda i, c: (i, 0)),
                  pl.BlockSpec((EC, D_MODEL), lambda i, c: (c, 0)),
                  pl.BlockSpec((D_MODEL, EC), lambda i, c: (0, c)),
                  slab(), slab(), full(),
                  pl.BlockSpec((tb, D_MODEL), lambda i, c: (i, 0)),
                  pl.BlockSpec((1, 6, D_MODEL), lambda i, c: (cond(i), 0, 0)),
                  const(lg), const(lb)],
        out_specs=pl.BlockSpec((tb, D_MODEL), lambda i, c: (i, 0)),
        out_shape=jax.ShapeDtypeStruct((nt, D_MODEL), F32),
        scratch_shapes=[pltpu.VMEM((D_MODEL, tb), F32), pltpu.VMEM((EC, tb), F32)],
        compiler_params=_cparams(("parallel", "arbitrary")),
        name="peer_experts",
    )(h2, u16, vt16, p0r, ter, e1, x1, mod, lg, lb)


def _swap_halves(w, n_heads, rot):
    k = w.shape[0]
    w4 = w.reshape(k, n_heads, 2, rot // 2)
    return jnp.concatenate([w4[:, :, 1:2], w4[:, :, 0:1]], axis=2).reshape(k, n_heads * rot)


def _inproj_weight(w):
    a = w[:, 0:512]
    qb, kb, vb = w[:, 512:768], w[:, 768:1024], w[:, 1024:1280]
    cq, ckv, kr = w[:, 1280:1536], w[:, 1536:1664], w[:, 1664:1696]
    qd, kd, vd = w[:, 1696:1952], w[:, 1952:2080], w[:, 2080:2208]
    rep = lambda m: jnp.repeat(m.reshape(D_MODEL, 2, HEAD_DIM), 2, axis=1).reshape(D_MODEL, 256)
    kr4 = jnp.tile(kr, (1, N_HEADS))
    kdr = rep(kd)
    cols = [a, qb, kb, vb, cq, ckv, kr4, _swap_halves(kr4, N_HEADS, MLA_ROPE),
            qd, _swap_halves(qd, N_HEADS, HEAD_DIM), kdr, _swap_halves(kdr, N_HEADS, HEAD_DIM), rep(vd)]
    return jnp.concatenate(cols, axis=1).astype(BF16)


def _mla_weights(w_uq, w_ukv):
    q3 = w_uq.reshape(MLA_Q_RANK, N_HEADS, MLA_NOPE + MLA_ROPE)
    nope = q3[:, :, :MLA_NOPE].reshape(MLA_Q_RANK, N_HEADS * MLA_NOPE)
    rope = q3[:, :, MLA_NOPE:].reshape(MLA_Q_RANK, N_HEADS * MLA_ROPE)
    wuq = jnp.concatenate([nope, rope, _swap_halves(rope, N_HEADS, MLA_ROPE)], axis=1).astype(BF16)
    kv3 = w_ukv.reshape(MLA_KV_RANK, N_HEADS, MLA_NOPE + 64)
    wukv = jnp.concatenate([kv3[:, :, :MLA_NOPE].reshape(MLA_KV_RANK, 256),
                            kv3[:, :, MLA_NOPE:].reshape(MLA_KV_RANK, 256)], axis=1).astype(BF16)
    return wuq, wukv


def _rope_tables(lat_len, rot, tile):
    t = jnp.arange(lat_len)
    row = (t // GRID_W).astype(F32)
    col = (t % GRID_W).astype(F32)
    nf = rot // 4
    freqs = ROPE_BASE ** (-jnp.arange(nf, dtype=F32) / nf)
    ang = jnp.concatenate([row[:, None] * freqs, col[:, None] * freqs], -1)
    cos, sin = jnp.cos(ang), jnp.sin(ang)
    cos_t = jnp.tile(jnp.concatenate([cos, cos], -1), (1, N_HEADS))
    sin_t = jnp.tile(jnp.concatenate([-sin, sin], -1), (1, N_HEADS))
    w = N_HEADS * rot
    return (jnp.concatenate([jnp.ones((tile, w), F32), cos_t], 0),
            jnp.concatenate([jnp.zeros((tile, w), F32), sin_t], 0))


def _na_bias_tables(rpb, rows):
    kh = min(NA_KH, rows)
    qrow = np.array([0, 1, 2, 3, rows // 2, rows - 3, rows - 2, rows - 1])
    start = np.clip(qrow - kh // 2, 0, rows - kh)
    dr0 = start - qrow + NA_KH - 1
    qc = np.arange(GRID_W)
    kc = np.arange(GRID_W)
    cstart = np.clip(qc - NA_KW // 2, 0, GRID_W - NA_KW)
    ok = (kc[None, :] >= cstart[:, None]) & (kc[None, :] < cstart[:, None] + NA_KW)
    n_h, n_a, n_c = rpb.shape
    edge = GRID_W - NA_KW
    w = jnp.concatenate([jnp.broadcast_to(rpb[:, :, :1], (n_h, n_a, edge)), rpb,
                         jnp.broadcast_to(rpb[:, :, -1:], (n_h, n_a, edge + 1))], axis=-1)
    skew = jnp.tile(w, (1, 1, GRID_W))[:, :, :GRID_W * (2 * GRID_W - 1)].reshape(n_h, n_a, GRID_W, 2 * GRID_W - 1)
    toep = skew[:, :, :, GRID_W - 1:]
    tabs = []
    for d0 in dr0:
        b = jnp.where(ok[None, None], toep[:, d0:d0 + kh], NEG_INF)
        tabs.append(b.transpose(0, 2, 1, 3).reshape(N_HEADS * GRID_W, kh * GRID_W))
    return jnp.stack(tabs)


def _swa_bias_tables(lat_len):
    nb = lat_len // SWA_WIN
    qi = np.arange(SWA_WIN)
    kj = np.arange(3 * SWA_WIN)
    in_win = np.abs(kj[None, :] - SWA_WIN - qi[:, None]) <= SWA_WIN
    tabs = []
    for n in (0, 1, nb - 1):
        kpos = n * SWA_WIN - SWA_WIN + kj
        in_seq = (kpos >= 0) & (kpos < lat_len)
        tabs.append(np.where(in_win & in_seq[None, :], 0.0, NEG_INF))
    return jnp.asarray(np.stack(tabs), F32)


def _heads_to_lanes(t):
    b, h, s, d = t.shape
    return t.transpose(0, 2, 1, 3).reshape(b, s, h * d)


def _lanes_to_heads(t, b, s, h):
    return t.reshape(b, s, h, -1).transpose(0, 2, 1, 3)


def kernel(x_prompt, x_sample, cache_nat_k, cache_nat_v, cache_mla_ckv, cache_mla_krope, cache_swa_k, cache_swa_v, c, c_ctx, w_in, w_out, out_norm_g, w_ada, b_ada, ln1_g, ln1_b, ln2_g, ln2_b, a_norm_g, a_norm_b, a_w_s, a_b_s, nat_rpb, mla_q_norm_g, mla_w_uq, mla_kv_norm_g, mla_w_ukv, swa_sinks, peer_w_q, peer_sub_keys, peer_u, peer_v):
    n_b, seq, d = x_prompt.shape
    n_db, lat_len, _ = x_sample.shape
    past = cache_nat_k.shape[3]
    n_ctx = n_b * seq
    nt = n_ctx + n_db * lat_len
    rows = lat_len // GRID_W

    x = jnp.concatenate([x_prompt.reshape(n_ctx, d), x_sample.reshape(n_db * lat_len, d)], axis=0)
    conds = jnp.zeros((8, d), F32).at[0].set(c_ctx).at[1:1 + n_db].set(c)
    mods = _ada_call(conds, w_ada, b_ada).reshape(DEPTH, 8, 6, d)

    cos_c, sin_c = _rope_tables(lat_len, MLA_ROPE, TM)
    cos_d, sin_d = _rope_tables(lat_len, HEAD_DIM, TM)
    swa_bias = _swa_bias_tables(lat_len)
    nb_swa = lat_len // SWA_WIN

    q_ctx = lambda b, j: b
    lat_q = lambda tq: (lambda b, j: (n_ctx + b * lat_len) // tq + j)
    ctx_keys = lambda w: pl.BlockSpec((seq, w), lambda b, j: (b, 0))
    lat_keys = lambda w: pl.BlockSpec((lat_len, w), lambda b, j: (n_ctx // lat_len + b, 0))

    states = [[] for _ in range(6)]
    for l in range(DEPTH):
        mod = mods[l]
        wbig = _inproj_weight(w_in[l])
        wuq, wukv = _mla_weights(mla_w_uq[l], mla_w_ukv[l])
        (za, qb, kb, vb, kb16, vb16, qcn, qcr, ckvn, krc, krc16, kcn, vc,
         qd, kd, vd, kd16, vd16) = _inproj_call(
            x, mod, wbig, wuq, wukv, mla_q_norm_g[l][None], mla_kv_norm_g[l][None],
            (cos_c, sin_c, cos_d, sin_d), n_ctx, lat_len)

        bs_full = jnp.repeat(a_b_s[l].T, HEAD_DIM, axis=1)
        oa = _chunk_mlp_call(za, a_norm_g[l][None], a_norm_b[l][None], a_w_s[l].astype(BF16), bs_full)

        ob_c = _attn_call("ctx_attn_b", (n_b, 1), seq, seq, HEAD_DIM ** -0.5, q_ctx, qb, kb16, vb16, ctx_keys)
        oc_c = _attn_call("ctx_attn_c", (n_b, 1), seq, seq, MLA_SCALE, q_ctx, qcn, kcn, vc, ctx_keys,
                          q2=qcr, k2=krc16)
        od_c = _attn_call("ctx_attn_d", (n_b, 1), seq, seq, HEAD_DIM ** -0.5, q_ctx, qd, kd16, vd16, ctx_keys,
                          sinks=swa_sinks[l])

        na_bias = _na_bias_tables(nat_rpb[l], rows)
        kh = min(NA_KH, rows)
        ob_l = _attn_call(
            "lat_attn_b", (n_db, rows), GRID_W, kh * GRID_W, HEAD_DIM ** -0.5, lat_q(GRID_W), qb, kb16, vb16, lat_keys,
            extra=(_heads_to_lanes(cache_nat_k[:, l]).astype(BF16), _heads_to_lanes(cache_nat_v[:, l]).astype(BF16)),
            bias=na_bias,
            bias_index=lambda r: jnp.where(r < 4, r, jnp.where(r > rows - 4, r - (rows - 8), 4)),
            start_fn=lambda r: jnp.clip(r - kh // 2, 0, rows - kh) * GRID_W)
        kxn, vx = _kvexp_call(cache_mla_ckv[:, l].reshape(n_db * past, MLA_KV_RANK), wukv)
        krx = jnp.tile(cache_mla_krope[:, l], (1, 1, N_HEADS)).astype(BF16)
        oc_l = _attn_call(
            "lat_attn_c", (n_db, lat_len // 128), 128, lat_len, MLA_SCALE, lat_q(128), qcn, kcn, vc, lat_keys,
            q2=qcr, k2=krc16,
            extra=(kxn.reshape(n_db, past, 256), krx, vx.reshape(n_db, past, 256)))
        pad = lambda t: jnp.pad(t[n_ctx:].reshape(n_db, lat_len, 256), ((0, 0), (SWA_WIN, SWA_WIN), (0, 0)))
        rep_kv = lambda t: jnp.repeat(t, 2, axis=1)
        od_l = _attn_call(
            "lat_attn_d", (n_db, nb_swa), SWA_WIN, 3 * SWA_WIN, HEAD_DIM ** -0.5, lat_q(SWA_WIN), qd,
            pad(kd16), pad(vd16),
            lambda w: pl.BlockSpec((None, lat_len + 2 * SWA_WIN, w), lambda b, j: (b, 0, 0)),
            extra=(_heads_to_lanes(rep_kv(cache_swa_k[:, l])).astype(BF16),
                   _heads_to_lanes(rep_kv(cache_swa_v[:, l])).astype(BF16)),
            bias=swa_bias,
            bias_index=lambda n: jnp.where(n == 0, 0, jnp.where(n == nb_swa - 1, 2, 1)),
            sinks=swa_sinks[l],
            start_fn=lambda n: n * SWA_WIN)
        ob = jnp.concatenate([ob_c, ob_l], axis=0)
        oc = jnp.concatenate([oc_c, oc_l], axis=0)
        od = jnp.concatenate([od_c, od_l], axis=0)

        x1 = _merge_call(oa, ob, oc, od, x, mod, out_norm_g[l][None], w_out[l].astype(BF16),
                         ln1_g[l][None], ln1_b[l][None], n_ctx, lat_len)
        h2, p0, te, e1 = _route_call(x1, mod, peer_w_q[l].astype(BF16), peer_sub_keys[l], n_ctx, lat_len)
        x = _expert_call(h2, peer_u[l].astype(BF16), peer_v[l].T.astype(BF16), p0, te, e1, x1, mod,
                         ln2_g[l][None], ln2_b[l][None], n_ctx, lat_len)

        states[0].append(_lanes_to_heads(kb[:n_ctx], n_b, seq, N_HEADS))
        states[1].append(_lanes_to_heads(vb[:n_ctx], n_b, seq, N_HEADS))
        states[2].append(ckvn[:n_ctx].reshape(n_b, seq, MLA_KV_RANK))
        states[3].append(krc[:n_ctx, :MLA_ROPE].reshape(n_b, seq, MLA_ROPE))
        states[4].append(_lanes_to_heads(kd[:n_ctx], n_b, seq, N_HEADS)[:, ::2])
        states[5].append(_lanes_to_heads(vd[:n_ctx], n_b, seq, N_HEADS)[:, ::2])

    y_prompt = x[:n_ctx].reshape(n_b, seq, d)
    y_sample = x[n_ctx:].reshape(n_db, lat_len, d)
    return (y_prompt, y_sample) + tuple(jnp.stack(s, axis=1) for s in states)
```

```python
import functools
import math

import jax
import jax.numpy as jnp
import numpy as np
from jax import lax
from jax.experimental import pallas as pl
from jax.experimental.pallas import tpu as pltpu

F32 = jnp.float32
BF16 = jnp.bfloat16

D_MODEL = 1024
DEPTH = 2
GRID_W = 64
HEAD_DIM = 64
N_HEADS = 4
GROUP_W = 256
CHUNK = 128
NA_KH = 8
NA_KW = 16
MLA_Q_RANK = 256
MLA_KV_RANK = 128
MLA_NOPE = 64
MLA_ROPE = 32
MLA_SCALE = (MLA_NOPE + MLA_ROPE) ** -0.5
SWA_WIN = 128
PEER_HEADS = 8
PEER_KEYS = 128
PEER_TOPK = 16
N_EXPERTS = PEER_KEYS * PEER_KEYS
ROPE_BASE = 10000.0
LN_EPS = 1e-5
NEG_INF = -1e30
ALPHA = (2 * DEPTH) ** 0.25

V7X_VMEM_LIMIT_BYTES = 56 * 1024 * 1024
TM = 256
TB_ROUTE = 256
TB_EXP = 512
ATTN_KEY_CHUNK = 1024
EC = 1024
N_SLAB = EC // PEER_KEYS
EXP_PIECE = 256

_C_A = 0
_C_QB, _C_KB, _C_VB = 512, 768, 1024
_C_CQ, _C_CKV, _C_KR, _C_KRS = 1280, 1536, 1664, 1792
_C_QD, _C_QDS, _C_KD, _C_KDS, _C_VD = 1920, 2176, 2432, 2688, 2944
_C_END = 3200


def _cparams(sem):
    return pltpu.CompilerParams(dimension_semantics=sem, vmem_limit_bytes=V7X_VMEM_LIMIT_BYTES)


def _dot(a, b):
    return jnp.dot(a, b, preferred_element_type=F32)


def _dot_nt(a, b):
    return lax.dot_general(a, b, (((1,), (1,)), ((), ())), preferred_element_type=F32)


def _layer_norm(x, g, b):
    mu = jnp.mean(x, axis=-1, keepdims=True)
    xc = x - mu
    var = jnp.mean(xc * xc, axis=-1, keepdims=True)
    return xc * lax.rsqrt(var + LN_EPS) * g + b


def _rms_norm(x, g):
    return x * lax.rsqrt(jnp.mean(x * x, axis=-1, keepdims=True) + LN_EPS) * g


def _ada_kernel(c_ref, w_ref, b_ref, o_ref):
    c = c_ref[...]
    a = c * jax.nn.sigmoid(c)
    a_hi = a.astype(BF16)
    a_lo = (a - a_hi.astype(F32)).astype(BF16)
    w = w_ref[0]
    w_hi = w.astype(BF16)
    w_lo = (w - w_hi.astype(F32)).astype(BF16)
    o_ref[0] = _dot(a_hi, w_hi) + _dot(a_hi, w_lo) + _dot(a_lo, w_hi) + b_ref[0]


def _ada_call(conds, w_ada, b_ada):
    tn = 1536
    n = w_ada.shape[-1]
    return pl.pallas_call(
        _ada_kernel,
        grid=(DEPTH, n // tn),
        in_specs=[pl.BlockSpec((8, D_MODEL), lambda l, j: (0, 0)),
                  pl.BlockSpec((1, D_MODEL, tn), lambda l, j: (l, 0, j)),
                  pl.BlockSpec((1, 1, tn), lambda l, j: (l, 0, j))],
        out_specs=pl.BlockSpec((1, 8, tn), lambda l, j: (l, 0, j)),
        out_shape=jax.ShapeDtypeStruct((DEPTH, 8, n), F32),
        compiler_params=_cparams(("parallel", "parallel")),
        name="ada_mod",
    )(conds, w_ada, b_ada.reshape(DEPTH, 1, n))


def _inproj_kernel(x_ref, mod_ref, w_ref, wuq_ref, wukv_ref, gq_ref, gkv_ref,
                   cosc_ref, sinc_ref, cosd_ref, sind_ref,
                   za_ref, qb_ref, kb_ref, vb_ref, kb16_ref, vb16_ref,
                   qcn_ref, qcr_ref, ckvn_ref, krc_ref, krc16_ref, kcn_ref, vc_ref,
                   qd_ref, kd_ref, vd_ref, kd16_ref, vd16_ref):
    m = mod_ref[0]
    h = x_ref[...] * (1.0 + m[1:2]) + m[0:1]
    z = _dot(h.astype(BF16), w_ref[...])
    cosc, sinc = cosc_ref[...], sinc_ref[...]
    cosd, sind = cosd_ref[...], sind_ref[...]

    za_ref[...] = z[:, _C_A:_C_QB]
    qb_ref[...] = z[:, _C_QB:_C_KB].astype(BF16)
    kb = z[:, _C_KB:_C_VB]
    vb = z[:, _C_VB:_C_CQ]
    kb_ref[...] = kb
    vb_ref[...] = vb
    kb16_ref[...] = kb.astype(BF16)
    vb16_ref[...] = vb.astype(BF16)

    cqn = _rms_norm(z[:, _C_CQ:_C_CKV], gq_ref[...])
    q = _dot(cqn.astype(BF16), wuq_ref[...])
    qcn_ref[...] = q[:, 0:256].astype(BF16)
    qcr_ref[...] = (q[:, 256:384] * cosc + q[:, 384:512] * sinc).astype(BF16)
    ckvn = _rms_norm(z[:, _C_CKV:_C_KR], gkv_ref[...])
    ckvn_ref[...] = ckvn
    kv = _dot(ckvn.astype(BF16), wukv_ref[...])
    kcn_ref[...] = kv[:, 0:256].astype(BF16)
    vc_ref[...] = kv[:, 256:512].astype(BF16)
    kr = z[:, _C_KR:_C_KRS] * cosc + z[:, _C_KRS:_C_QD] * sinc
    krc_ref[...] = kr
    krc16_ref[...] = kr.astype(BF16)

    qd_ref[...] = (z[:, _C_QD:_C_QDS] * cosd + z[:, _C_QDS:_C_KD] * sind).astype(BF16)
    kd = z[:, _C_KD:_C_KDS] * cosd + z[:, _C_KDS:_C_VD] * sind
    vd = z[:, _C_VD:_C_END]
    kd_ref[...] = kd
    vd_ref[...] = vd
    kd16_ref[...] = kd.astype(BF16)
    vd16_ref[...] = vd.astype(BF16)


def _cond_of_tile(i, tile, n_ctx, lat_len):
    n_ctx_tiles = n_ctx // tile
    return jnp.where(i < n_ctx_tiles, 0, 1 + (i - n_ctx_tiles) // (lat_len // tile))


def _rope_block_of_tile(i, tile, n_ctx, lat_len):
    n_ctx_tiles = n_ctx // tile
    return jnp.where(i < n_ctx_tiles, 0, 1 + (i - n_ctx_tiles) % (lat_len // tile))


def _inproj_call(x, mod, wbig, wuq, wukv, gq, gkv, tabs, n_ctx, lat_len):
    nt = x.shape[0]
    cond = functools.partial(_cond_of_tile, tile=TM, n_ctx=n_ctx, lat_len=lat_len)
    rblk = functools.partial(_rope_block_of_tile, tile=TM, n_ctx=n_ctx, lat_len=lat_len)
    row = lambda w: pl.BlockSpec((TM, w), lambda i: (i, 0))
    const = lambda a: pl.BlockSpec(a.shape, lambda i: (0,) * a.ndim)
    tab = lambda w: pl.BlockSpec((TM, w), lambda i: (rblk(i), 0))
    outs = [(512, F32), (256, BF16), (256, F32), (256, F32), (256, BF16), (256, BF16),
            (256, BF16), (128, BF16), (128, F32), (128, F32), (128, BF16), (256, BF16), (256, BF16),
            (256, BF16), (256, F32), (256, F32), (256, BF16), (256, BF16)]
    return pl.pallas_call(
        _inproj_kernel,
        grid=(nt // TM,),
        in_specs=[row(D_MODEL),
                  pl.BlockSpec((1, 6, D_MODEL), lambda i: (cond(i), 0, 0)),
                  const(wbig), const(wuq), const(wukv), const(gq), const(gkv),
                  tab(128), tab(128), tab(256), tab(256)],
        out_specs=[row(w) for w, _ in outs],
        out_shape=[jax.ShapeDtypeStruct((nt, w), dt) for w, dt in outs],
        compiler_params=_cparams(("parallel",)),
        name="inproj",
    )(x, mod, wbig, wuq, wukv, gq, gkv, *tabs)


def _kvexp_kernel(c_ref, w_ref, k_ref, v_ref):
    kv = _dot(c_ref[...].astype(BF16), w_ref[...])
    k_ref[...] = kv[:, 0:256].astype(BF16)
    v_ref[...] = kv[:, 256:512].astype(BF16)


def _kvexp_call(ckv, wukv):
    n = ckv.shape[0]
    return pl.pallas_call(
        _kvexp_kernel,
        grid=(1,),
        in_specs=[pl.BlockSpec(ckv.shape, lambda i: (0, 0)), pl.BlockSpec(wukv.shape, lambda i: (0, 0))],
        out_specs=[pl.BlockSpec((n, 256), lambda i: (0, 0))] * 2,
        out_shape=[jax.ShapeDtypeStruct((n, 256), BF16)] * 2,
        compiler_params=_cparams(("arbitrary",)),
        name="mla_cache_expand",
    )(ckv, wukv)


def _chunk_mlp_kernel(z_ref, g_ref, b_ref, ws_ref, bs_ref, o_ref):
    g = jax.nn.gelu(z_ref[...])
    u = g[:, 0:GROUP_W]
    v = _layer_norm(g[:, GROUP_W:2 * GROUP_W], g_ref[...], b_ref[...]).astype(BF16)
    lane_head = lax.broadcasted_iota(jnp.int32, (1, GROUP_W), 1) // HEAD_DIM
    mixed = bs_ref[...]
    for hd in range(N_HEADS):
        mixed = mixed + jnp.where(lane_head == hd, _dot(ws_ref[hd], v), 0.0)
    o_ref[...] = u * mixed


def _chunk_mlp_call(za, g, b, ws16, bs_full):
    nt = za.shape[0]
    return pl.pallas_call(
        _chunk_mlp_kernel,
        grid=(nt // CHUNK,),
        in_specs=[pl.BlockSpec((CHUNK, 2 * GROUP_W), lambda i: (i, 0)),
                  pl.BlockSpec((1, GROUP_W), lambda i: (0, 0)),
                  pl.BlockSpec((1, GROUP_W), lambda i: (0, 0)),
                  pl.BlockSpec((N_HEADS, CHUNK, CHUNK), lambda i: (0, 0, 0)),
                  pl.BlockSpec((CHUNK, GROUP_W), lambda i: (0, 0))],
        out_specs=pl.BlockSpec((CHUNK, GROUP_W), lambda i: (i, 0)),
        out_shape=jax.ShapeDtypeStruct((nt, GROUP_W), F32),
        compiler_params=_cparams(("parallel",)),
        name="chunk_mlp",
    )(za, g, b, ws16, bs_full)


def _attn_kernel(*refs, tq, wk, kc, scale, has_q2, has_extra, bias_heads, has_sink, start_fn):
    refs = list(refs)
    sink_ref = refs.pop(0) if has_sink else None
    q1_ref = refs.pop(0)
    q2_ref = refs.pop(0) if has_q2 else None
    k1_ref = refs.pop(0)
    k2_ref = refs.pop(0) if has_q2 else None
    v_ref = refs.pop(0)
    if has_extra:
        xk1_ref = refs.pop(0)
        xk2_ref = refs.pop(0) if has_q2 else None
        xv_ref = refs.pop(0)
    bias_ref = refs.pop(0) if bias_heads else None
    o_ref = refs.pop(0)

    def stack_heads(q, width):
        lane_head = lax.broadcasted_iota(jnp.int32, (1, q.shape[1]), 1) // width
        return jnp.concatenate([jnp.where(lane_head == hd, q, jnp.zeros_like(q)) for hd in range(N_HEADS)], axis=0)

    q1s = stack_heads(q1_ref[...], HEAD_DIM)
    q2s = stack_heads(q2_ref[...], MLA_ROPE) if has_q2 else None

    start = start_fn(pl.program_id(1))
    if not isinstance(start, int):
        start = pl.multiple_of(start, 64)

    chunks = [("win", c0, min(kc, wk - c0)) for c0 in range(0, wk, kc)]
    if has_extra:
        chunks.append(("extra", 0, 0))
    sink = None
    if has_sink:
        sink = jnp.concatenate([jnp.full((tq, 1), sink_ref[hd], F32) for hd in range(N_HEADS)], axis=0)
    m = denom = o = None
    for kind, c0, n in chunks:
        if kind == "win":
            rows = pl.ds(start if c0 == 0 else start + c0, n)
            k1c, vc = k1_ref[rows, :], v_ref[rows, :]
            k2c = k2_ref[rows, :] if has_q2 else None
        else:
            k1c, vc = xk1_ref[...], xv_ref[...]
            k2c = xk2_ref[...] if has_q2 else None
        s = _dot_nt(q1s, k1c)
        if has_q2:
            s = s + _dot_nt(q2s, k2c)
        s = s * scale
        if kind == "win" and bias_heads == N_HEADS:
            s = s + bias_ref[:, c0:c0 + n]
        elif kind == "win" and bias_heads == 1:
            s = s + jnp.concatenate([bias_ref[:, c0:c0 + n]] * N_HEADS, axis=0)
        mc = jnp.max(s, axis=-1, keepdims=True)
        if m is None:
            m_new = mc if sink is None else jnp.maximum(mc, sink)
        else:
            m_new = jnp.maximum(m, mc)
        p = jnp.exp(s - m_new)
        pv = _dot(p.astype(BF16), vc)
        if m is None:
            denom, o = jnp.sum(p, axis=-1, keepdims=True), pv
        else:
            alpha = jnp.exp(m - m_new)
            denom = alpha * denom + jnp.sum(p, axis=-1, keepdims=True)
            o = alpha * o + pv
        m = m_new
    if has_sink:
        denom = denom + jnp.exp(sink - m)
    o = o / denom
    lane_head = lax.broadcasted_iota(jnp.int32, (1, N_HEADS * HEAD_DIM), 1) // HEAD_DIM
    out = jnp.zeros((tq, N_HEADS * HEAD_DIM), F32)
    for hd in range(N_HEADS):
        out = out + jnp.where(lane_head == hd, o[hd * tq:(hd + 1) * tq], 0.0)
    o_ref[...] = out


def _attn_call(name, grid, tq, wk, scale, q_index, q1, k1, v, k_spec_fn, *, q2=None, k2=None,
               extra=None, bias=None, bias_index=None, sinks=None, start_fn=lambda j: 0, kc=ATTN_KEY_CHUNK):
    has_q2 = q2 is not None
    has_extra = extra is not None
    bias_heads = 0 if bias is None else bias.shape[1] // tq
    args, specs = [], []
    if sinks is not None:
        args.append(sinks)
        specs.append(pl.BlockSpec(memory_space=pltpu.SMEM))
    args.append(q1)
    specs.append(pl.BlockSpec((tq, 256), lambda b, j: (q_index(b, j), 0)))
    if has_q2:
        args.append(q2)
        specs.append(pl.BlockSpec((tq, 128), lambda b, j: (q_index(b, j), 0)))
    args.append(k1)
    specs.append(k_spec_fn(256))
    if has_q2:
        args.append(k2)
        specs.append(k_spec_fn(128))
    args.append(v)
    specs.append(k_spec_fn(256))
    if has_extra:
        for a in extra:
            args.append(a)
            specs.append(pl.BlockSpec((None,) + a.shape[1:], lambda b, j: (b, 0, 0)))
    if bias is not None:
        args.append(bias)
        specs.append(pl.BlockSpec((None,) + bias.shape[1:], lambda b, j: (bias_index(j), 0, 0)))
    kern = functools.partial(_attn_kernel, tq=tq, wk=wk, kc=kc, scale=scale, has_q2=has_q2, has_extra=has_extra,
                             bias_heads=bias_heads, has_sink=sinks is not None, start_fn=start_fn)
    return pl.pallas_call(
        kern,
        grid=grid,
        in_specs=specs,
        out_specs=pl.BlockSpec((tq, 256), lambda b, j: (b * grid[1] + j, 0)),
        out_shape=jax.ShapeDtypeStruct((grid[0] * grid[1] * tq, 256), F32),
        compiler_params=_cparams(("parallel", "parallel")),
        name=name,
    )(*args)


def _merge_kernel(oa_ref, ob_ref, oc_ref, od_ref, x_ref, mod_ref, g_ref, w_ref, lg_ref, lb_ref, o_ref):
    m = mod_ref[0]
    acc = None
    for gi, r in enumerate((oa_ref, ob_ref, oc_ref, od_ref)):
        og = _rms_norm(r[...], g_ref[:, gi * GROUP_W:(gi + 1) * GROUP_W]).astype(BF16)
        part = _dot(og, w_ref[gi * GROUP_W:(gi + 1) * GROUP_W, :])
        acc = part if acc is None else acc + part
    y = ALPHA * x_ref[...] + m[2:3] * acc
    o_ref[...] = _layer_norm(y, lg_ref[...], lb_ref[...])


def _merge_call(oa, ob, oc, od, x, mod, gout, wout16, lg, lb, n_ctx, lat_len):
    nt = x.shape[0]
    cond = functools.partial(_cond_of_tile, tile=TM, n_ctx=n_ctx, lat_len=lat_len)
    row = lambda w: pl.BlockSpec((TM, w), lambda i: (i, 0))
    const = lambda a: pl.BlockSpec(a.shape, lambda i: (0,) * a.ndim)
    return pl.pallas_call(
        _merge_kernel,
        grid=(nt // TM,),
        in_specs=[row(256), row(256), row(256), row(256), row(D_MODEL),
                  pl.BlockSpec((1, 6, D_MODEL), lambda i: (cond(i), 0, 0)),
                  const(gout), const(wout16), const(lg), const(lb)],
        out_specs=row(D_MODEL),
        out_shape=jax.ShapeDtypeStruct((nt, D_MODEL), F32),
        compiler_params=_cparams(("parallel",)),
        name="merge_out",
    )(oa, ob, oc, od, x, mod, gout, wout16, lg, lb)


_N_TOP = PEER_TOPK + 1


def _sort_network(n):
    pairs = []
    p = 1
    while p < n:
        k = p
        while k >= 1:
            for j in range(k % p, n - k, 2 * k):
                for i in range(min(k, n - j - k)):
                    if (i + j) // (2 * p) == (i + j + k) // (2 * p):
                        pairs.append((i + j, i + j + k))
            k //= 2
        p *= 2
    return pairs


def _merge_top(levels, n_top):
    levels = list(levels)
    sub = lax.broadcasted_iota(jnp.int32, levels[0].shape, 0).astype(F32)
    out = []
    for k in range(n_top):
        head = levels[0]
        m = jnp.max(head, axis=0, keepdims=True)
        out.append(m)
        first = jnp.min(jnp.where(head == m, sub, 8.0), axis=0, keepdims=True)
        pop = sub == first
        for v in range(n_top - 1 - k):
            nxt = levels[v + 1] if v + 1 < len(levels) else -jnp.inf
            levels[v] = jnp.where(pop, nxt, levels[v])
    return out


def _top_values(s, n_top):
    g = [s[8 * v:8 * v + 8] for v in range(s.shape[0] // 8)]
    for i, j in _sort_network(len(g)):
        g[i], g[j] = jnp.maximum(g[i], g[j]), jnp.minimum(g[i], g[j])
    return _merge_top(g, n_top)


_CAND_LEVELS = (17, 8, 5, 4, 13, 4, 1, 0)


def _candidate_levels(sv0, sv1):
    shape = (8,) + sv0[0].shape[1:]
    row = lax.broadcasted_iota(jnp.int32, shape, 0)
    pick = lambda vals, default: functools.reduce(
        lambda acc, rv: jnp.where(row == rv[0], rv[1], acc), vals, jnp.full(shape, default, F32))
    fixed0 = pick([(r, sv0[r]) for r in range(4)], 0.0)
    fixed1 = pick([(4 + r, sv1[r]) for r in range(3)], 0.0)
    n_valid = pick([(r, float(n)) for r, n in enumerate(_CAND_LEVELS)], 0.0)
    levels = []
    for v in range(max(_CAND_LEVELS)):
        moving0 = sv0[min(4 + v, len(sv0) - 1)]
        lv = jnp.where(row < 4, fixed0 + sv1[v], moving0 + fixed1)
        levels.append(jnp.where(n_valid > float(v), lv, -jnp.inf))
    return levels


def _route_kernel(x_ref, mod_ref, wq_ref, keys_ref, h2_ref, p0_ref, te_ref, e1_ref, st_scr):
    m = mod_ref[0]
    h2 = (x_ref[...] * (1.0 + m[4:5]) + m[3:4]).astype(BF16)
    h2_ref[...] = h2
    q = _dot(h2, wq_ref[...])
    tb = q.shape[0]
    for p in range(2):
        kp = keys_ref[p]
        k_hi = kp.astype(BF16)
        k_lo = (kp - k_hi.astype(F32)).astype(BF16)
        for hd in range(PEER_HEADS):
            c0 = (hd * 2 + p) * PEER_KEYS
            qs = q[:, c0:c0 + PEER_KEYS]
            q_hi = qs.astype(BF16)
            q_lo = (qs - q_hi.astype(F32)).astype(BF16)
            st_scr[hd * 2 + p] = _dot_nt(k_hi, q_hi) + _dot_nt(k_lo, q_hi) + _dot_nt(k_hi, q_lo)

    def per_head(hd, carry):
        s0 = st_scr[hd * 2]
        s1 = st_scr[hd * 2 + 1]
        sv0 = _top_values(s0, _N_TOP)
        sv1 = _top_values(s1, _N_TOP)
        c = _merge_top(_candidate_levels(sv0, sv1), _N_TOP)
        thr = 0.5 * (c[PEER_TOPK - 1] + c[PEER_TOPK])
        z = jnp.zeros_like(thr)
        for k in range(PEER_TOPK):
            z = z + jnp.exp(c[k] - c[0])
        p0_ref[hd] = jnp.exp(s0 - sv0[0]) / z
        te_ref[hd] = jnp.exp((thr - sv1[0]) - s0)
        e1_ref[hd] = jnp.exp(s1 - sv1[0])
        return carry

    lax.fori_loop(0, PEER_HEADS, per_head, 0)


def _route_call(x1, mod, wq16, keys, n_ctx, lat_len):
    nt = x1.shape[0]
    tb = TB_ROUTE
    cond = functools.partial(_cond_of_tile, tile=tb, n_ctx=n_ctx, lat_len=lat_len)
    fac = lambda: pl.BlockSpec((PEER_HEADS, PEER_KEYS, tb), lambda i: (0, 0, i))
    fshape = jax.ShapeDtypeStruct((PEER_HEADS, PEER_KEYS, nt), F32)
    return pl.pallas_call(
        _route_kernel,
        grid=(nt // tb,),
        in_specs=[pl.BlockSpec((tb, D_MODEL), lambda i: (i, 0)),
                  pl.BlockSpec((1, 6, D_MODEL), lambda i: (cond(i), 0, 0)),
                  pl.BlockSpec(wq16.shape, lambda i: (0, 0)),
                  pl.BlockSpec(keys.shape, lambda i: (0, 0, 0))],
        out_specs=[pl.BlockSpec((tb, D_MODEL), lambda i: (i, 0)), fac(), fac(), fac()],
        out_shape=[jax.ShapeDtypeStruct((nt, D_MODEL), BF16), fshape, fshape, fshape],
        scratch_shapes=[pltpu.VMEM((2 * PEER_HEADS, PEER_KEYS, tb), F32)],
        compiler_params=_cparams(("parallel",)),
        name="peer_route",
    )(x1, mod, wq16, keys)


def _expert_kernel(h2_ref, u_ref, vt_ref, p0_ref, te_ref, e1_ref, x_ref, mod_ref, lg_ref, lb_ref,
                   o_ref, acc_scr, act_scr):
    c = pl.program_id(1)
    n_lane = h2_ref.shape[0] // PEER_KEYS

    @pl.when(c == 0)
    def _():
        acc_scr[...] = jnp.zeros_like(acc_scr)

    act_scr[...] = jax.nn.gelu(_dot_nt(u_ref[...], h2_ref[...]))
    for pc in range(EC // EXP_PIECE):
        slabs = range(pc * EXP_PIECE // PEER_KEYS, (pc + 1) * EXP_PIECE // PEER_KEYS)
        g_lanes = []
        for ln in range(n_lane):
            lanes = slice(ln * PEER_KEYS, (ln + 1) * PEER_KEYS)
            w = [None] * len(slabs)
            for hd in range(PEER_HEADS):
                e1 = e1_ref[hd, :, lanes]
                for k, sl in enumerate(slabs):
                    te = te_ref[hd, 0, sl:sl + 1, lanes]
                    e0 = p0_ref[hd, 0, sl:sl + 1, lanes]
                    term = e0 * jnp.where(e1 > te, e1, 0.0)
                    w[k] = term if w[k] is None else w[k] + term
            g_lanes.append(jnp.concatenate(
                [(w[k] * act_scr[sl * PEER_KEYS:(sl + 1) * PEER_KEYS, lanes]).astype(BF16)
                 for k, sl in enumerate(slabs)], axis=0))
        g = jnp.concatenate(g_lanes, axis=1)
        acc_scr[...] += _dot(vt_ref[:, pc * EXP_PIECE:(pc + 1) * EXP_PIECE], g)

    @pl.when(c == pl.num_programs(1) - 1)
    def _():
        m = mod_ref[0]
        y = ALPHA * x_ref[...] + m[5:6] * acc_scr[...].T
        o_ref[...] = _layer_norm(y, lg_ref[...], lb_ref[...])


def _expert_call(h2, u16, vt16, p0, te, e1, x1, mod, lg, lb, n_ctx, lat_len):
    nt = x1.shape[0]
    tb = TB_EXP
    n_chunks = N_EXPERTS // EC
    cond = functools.partial(_cond_of_tile, tile=tb, n_ctx=n_ctx, lat_len=lat_len)
    p0r = p0.reshape(PEER_HEADS, n_chunks, N_SLAB, nt)
    ter = te.reshape(PEER_HEADS, n_chunks, N_SLAB, nt)
    slab = lambda: pl.BlockSpec((PEER_HEADS, 1, N_SLAB, tb), lambda i, c: (0, c, 0, i))
    full = lambda: pl.BlockSpec((PEER_HEADS, PEER_KEYS, tb), lambda i, c: (0, 0, i))
    const = lambda a: pl.BlockSpec(a.shape, lambda i, c: (0,) * a.ndim)
    return pl.pallas_call(
        _expert_kernel,
        grid=(nt // tb, n_chunks),
        in_specs=[pl.BlockSpec((tb, D_MODEL), lambda i, c: (i, 0)),
                  pl.BlockSpec((EC, D_MODEL), lambda i, c: (c, 0)),
                  pl.BlockSpec((D_MODEL, EC), lambda i, c: (0, c)),
                  slab(), slab(), full(),
                  pl.BlockSpec((tb, D_MODEL), lambda i, c: (i, 0)),
                  pl.BlockSpec((1, 6, D_MODEL), lambda i, c: (cond(i), 0, 0)),
                  const(lg), const(lb)],
        out_specs=pl.BlockSpec((tb, D_MODEL), lambda i, c: (i, 0)),
        out_shape=jax.ShapeDtypeStruct((nt, D_MODEL), F32),
        scratch_shapes=[pltpu.VMEM((D_MODEL, tb), F32), pltpu.VMEM((EC, tb), F32)],
        compiler_params=_cparams(("parallel", "arbitrary")),
        name="peer_experts",
    )(h2, u16, vt16, p0r, ter, e1, x1, mod, lg, lb)


def _swap_halves(w, n_heads, rot):
    k = w.shape[0]
    w4 = w.reshape(k, n_heads, 2, rot // 2)
    return jnp.concatenate([w4[:, :, 1:2], w4[:, :, 0:1]], axis=2).reshape(k, n_heads * rot)


def _inproj_weight(w):
    a = w[:, 0:512]
    qb, kb, vb = w[:, 512:768], w[:, 768:1024], w[:, 1024:1280]
    cq, ckv, kr = w[:, 1280:1536], w[:, 1536:1664], w[:, 1664:1696]
    qd, kd, vd = w[:, 1696:1952], w[:, 1952:2080], w[:, 2080:2208]
    rep = lambda m: jnp.repeat(m.reshape(D_MODEL, 2, HEAD_DIM), 2, axis=1).reshape(D_MODEL, 256)
    kr4 = jnp.tile(kr, (1, N_HEADS))
    kdr = rep(kd)
    cols = [a, qb, kb, vb, cq, ckv, kr4, _swap_halves(kr4, N_HEADS, MLA_ROPE),
            qd, _swap_halves(qd, N_HEADS, HEAD_DIM), kdr, _swap_halves(kdr, N_HEADS, HEAD_DIM), rep(vd)]
    return jnp.concatenate(cols, axis=1).astype(BF16)


def _mla_weights(w_uq, w_ukv):
    q3 = w_uq.reshape(MLA_Q_RANK, N_HEADS, MLA_NOPE + MLA_ROPE)
    nope = q3[:, :, :MLA_NOPE].reshape(MLA_Q_RANK, N_HEADS * MLA_NOPE)
    rope = q3[:, :, MLA_NOPE:].reshape(MLA_Q_RANK, N_HEADS * MLA_ROPE)
    wuq = jnp.concatenate([nope, rope, _swap_halves(rope, N_HEADS, MLA_ROPE)], axis=1).astype(BF16)
    kv3 = w_ukv.reshape(MLA_KV_RANK, N_HEADS, MLA_NOPE + 64)
    wukv = jnp.concatenate([kv3[:, :, :MLA_NOPE].reshape(MLA_KV_RANK, 256),
                            kv3[:, :, MLA_NOPE:].reshape(MLA_KV_RANK, 256)], axis=1).astype(BF16)
    return wuq, wukv


def _rope_tables(lat_len, rot, tile):
    t = jnp.arange(lat_len)
    row = (t // GRID_W).astype(F32)
    col = (t % GRID_W).astype(F32)
    nf = rot // 4
    freqs = ROPE_BASE ** (-jnp.arange(nf, dtype=F32) / nf)
    ang = jnp.concatenate([row[:, None] * freqs, col[:, None] * freqs], -1)
    cos, sin = jnp.cos(ang), jnp.sin(ang)
    cos_t = jnp.tile(jnp.concatenate([cos, cos], -1), (1, N_HEADS))
    sin_t = jnp.tile(jnp.concatenate([-sin, sin], -1), (1, N_HEADS))
    w = N_HEADS * rot
    return (jnp.concatenate([jnp.ones((tile, w), F32), cos_t], 0),
            jnp.concatenate([jnp.zeros((tile, w), F32), sin_t], 0))


def _na_bias_tables(rpb, rows):
    kh = min(NA_KH, rows)
    qrow = np.array([0, 1, 2, 3, rows // 2, rows - 3, rows - 2, rows - 1])
    start = np.clip(qrow - kh // 2, 0, rows - kh)
    dr0 = start - qrow + NA_KH - 1
    qc = np.arange(GRID_W)
    kc = np.arange(GRID_W)
    cstart = np.clip(qc - NA_KW // 2, 0, GRID_W - NA_KW)
    ok = (kc[None, :] >= cstart[:, None]) & (kc[None, :] < cstart[:, None] + NA_KW)
    n_h, n_a, n_c = rpb.shape
    edge = GRID_W - NA_KW
    w = jnp.concatenate([jnp.broadcast_to(rpb[:, :, :1], (n_h, n_a, edge)), rpb,
                         jnp.broadcast_to(rpb[:, :, -1:], (n_h, n_a, edge + 1))], axis=-1)
    skew = jnp.tile(w, (1, 1, GRID_W))[:, :, :GRID_W * (2 * GRID_W - 1)].reshape(n_h, n_a, GRID_W, 2 * GRID_W - 1)
    toep = skew[:, :, :, GRID_W - 1:]
    tabs = []
    for d0 in dr0:
        b = jnp.where(ok[None, None], toep[:, d0:d0 + kh], NEG_INF)
        tabs.append(b.transpose(0, 2, 1, 3).reshape(N_HEADS * GRID_W, kh * GRID_W))
    return jnp.stack(tabs)


def _swa_bias_tables(lat_len):
    nb = lat_len // SWA_WIN
    qi = np.arange(SWA_WIN)
    kj = np.arange(3 * SWA_WIN)
    tabs = []
    for n in (0, 1, nb - 1):
        kpos = _swa_window_start(n, nb) + kj
        in_win = np.abs(kpos[None, :] - (n * SWA_WIN + qi[:, None])) <= SWA_WIN
        tabs.append(np.where(in_win, 0.0, NEG_INF))
    return jnp.asarray(np.stack(tabs), F32)


def _swa_window_start(n, nb):
    lo = n - 1
    lo = jnp.clip(lo, 0, nb - 3) if isinstance(n, jax.Array) else min(max(lo, 0), nb - 3)
    return lo * SWA_WIN


def _heads_to_lanes(t):
    b, h, s, d = t.shape
    return t.transpose(0, 2, 1, 3).reshape(b, s, h * d)


def _lanes_to_heads(t, b, s, h):
    return t.reshape(b, s, h, -1).transpose(0, 2, 1, 3)


def kernel(x_prompt, x_sample, cache_nat_k, cache_nat_v, cache_mla_ckv, cache_mla_krope, cache_swa_k, cache_swa_v, c, c_ctx, w_in, w_out, out_norm_g, w_ada, b_ada, ln1_g, ln1_b, ln2_g, ln2_b, a_norm_g, a_norm_b, a_w_s, a_b_s, nat_rpb, mla_q_norm_g, mla_w_uq, mla_kv_norm_g, mla_w_ukv, swa_sinks, peer_w_q, peer_sub_keys, peer_u, peer_v):
    n_b, seq, d = x_prompt.shape
    n_db, lat_len, _ = x_sample.shape
    past = cache_nat_k.shape[3]
    n_ctx = n_b * seq
    nt = n_ctx + n_db * lat_len
    rows = lat_len // GRID_W

    x = jnp.concatenate([x_prompt.reshape(n_ctx, d), x_sample.reshape(n_db * lat_len, d)], axis=0)
    conds = jnp.zeros((8, d), F32).at[0].set(c_ctx).at[1:1 + n_db].set(c)
    mods = _ada_call(conds, w_ada, b_ada).reshape(DEPTH, 8, 6, d)

    cos_c, sin_c = _rope_tables(lat_len, MLA_ROPE, TM)
    cos_d, sin_d = _rope_tables(lat_len, HEAD_DIM, TM)
    swa_bias = _swa_bias_tables(lat_len)
    nb_swa = lat_len // SWA_WIN

    q_ctx = lambda b, j: b
    lat_q = lambda tq: (lambda b, j: (n_ctx + b * lat_len) // tq + j)
    ctx_keys = lambda w: pl.BlockSpec((seq, w), lambda b, j: (b, 0))
    lat_keys = lambda w: pl.BlockSpec((lat_len, w), lambda b, j: (n_ctx // lat_len + b, 0))

    states = [[] for _ in range(6)]
    for l in range(DEPTH):
        mod = mods[l]
        wbig = _inproj_weight(w_in[l])
        wuq, wukv = _mla_weights(mla_w_uq[l], mla_w_ukv[l])
        (za, qb, kb, vb, kb16, vb16, qcn, qcr, ckvn, krc, krc16, kcn, vc,
         qd, kd, vd, kd16, vd16) = _inproj_call(
            x, mod, wbig, wuq, wukv, mla_q_norm_g[l][None], mla_kv_norm_g[l][None],
            (cos_c, sin_c, cos_d, sin_d), n_ctx, lat_len)

        bs_full = jnp.repeat(a_b_s[l].T, HEAD_DIM, axis=1)
        oa = _chunk_mlp_call(za, a_norm_g[l][None], a_norm_b[l][None], a_w_s[l].astype(BF16), bs_full)

        ob_c = _attn_call("ctx_attn_b", (n_b, 1), seq, seq, HEAD_DIM ** -0.5, q_ctx, qb, kb16, vb16, ctx_keys)
        oc_c = _attn_call("ctx_attn_c", (n_b, 1), seq, seq, MLA_SCALE, q_ctx, qcn, kcn, vc, ctx_keys,
                          q2=qcr, k2=krc16)
        od_c = _attn_call("ctx_attn_d", (n_b, 1), seq, seq, HEAD_DIM ** -0.5, q_ctx, qd, kd16, vd16, ctx_keys,
                          sinks=swa_sinks[l])

        na_bias = _na_bias_tables(nat_rpb[l], rows)
        kh = min(NA_KH, rows)
        ob_l = _attn_call(
            "lat_attn_b", (n_db, rows), GRID_W, kh * GRID_W, HEAD_DIM ** -0.5, lat_q(GRID_W), qb, kb16, vb16, lat_keys,
            extra=(_heads_to_lanes(cache_nat_k[:, l]).astype(BF16), _heads_to_lanes(cache_nat_v[:, l]).astype(BF16)),
            bias=na_bias,
            bias_index=lambda r: jnp.where(r < 4, r, jnp.where(r > rows - 4, r - (rows - 8), 4)),
            start_fn=lambda r: jnp.clip(r - kh // 2, 0, rows - kh) * GRID_W)
        kxn, vx = _kvexp_call(cache_mla_ckv[:, l].reshape(n_db * past, MLA_KV_RANK), wukv)
        krx = jnp.tile(cache_mla_krope[:, l], (1, 1, N_HEADS)).astype(BF16)
        oc_l = _attn_call(
            "lat_attn_c", (n_db, lat_len // 256), 256, lat_len, MLA_SCALE, lat_q(256), qcn, kcn, vc, lat_keys,
            q2=qcr, k2=krc16,
            extra=(kxn.reshape(n_db, past, 256), krx, vx.reshape(n_db, past, 256)))
        rep_kv = lambda t: jnp.repeat(t, 2, axis=1)
        od_l = _attn_call(
            "lat_attn_d", (n_db, nb_swa), SWA_WIN, 3 * SWA_WIN, HEAD_DIM ** -0.5, lat_q(SWA_WIN), qd,
            kd16, vd16, lat_keys,
            extra=(_heads_to_lanes(rep_kv(cache_swa_k[:, l])).astype(BF16),
                   _heads_to_lanes(rep_kv(cache_swa_v[:, l])).astype(BF16)),
            bias=swa_bias,
            bias_index=lambda n: jnp.where(n == 0, 0, jnp.where(n == nb_swa - 1, 2, 1)),
            sinks=swa_sinks[l],
            start_fn=lambda n: _swa_window_start(n, nb_swa))
        ob = jnp.concatenate([ob_c, ob_l], axis=0)
        oc = jnp.concatenate([oc_c, oc_l], axis=0)
        od = jnp.concatenate([od_c, od_l], axis=0)

        x1 = _merge_call(oa, ob, oc, od, x, mod, out_norm_g[l][None], w_out[l].astype(BF16),
                         ln1_g[l][None], ln1_b[l][None], n_ctx, lat_len)
        h2, p0, te, e1 = _route_call(x1, mod, peer_w_q[l].astype(BF16), peer_sub_keys[l], n_ctx, lat_len)
        x = _expert_call(h2, peer_u[l].astype(BF16), peer_v[l].T.astype(BF16), p0, te, e1, x1, mod,
                         ln2_g[l][None], ln2_b[l][None], n_ctx, lat_len)

        states[0].append(_lanes_to_heads(kb[:n_ctx], n_b, seq, N_HEADS))
        states[1].append(_lanes_to_heads(vb[:n_ctx], n_b, seq, N_HEADS))
        states[2].append(ckvn[:n_ctx].reshape(n_b, seq, MLA_KV_RANK))
        states[3].append(krc[:n_ctx, :MLA_ROPE].reshape(n_b, seq, MLA_ROPE))
        states[4].append(_lanes_to_heads(kd[:n_ctx], n_b, seq, N_HEADS)[:, ::2])
        states[5].append(_lanes_to_heads(vd[:n_ctx], n_b, seq, N_HEADS)[:, ::2])

    y_prompt = x[:n_ctx].reshape(n_b, seq, d)
    y_sample = x[n_ctx:].reshape(n_db, lat_len, d)
    return (y_prompt, y_sample) + tuple(jnp.stack(s, axis=1) for s in states)
```

```python
import functools
import math

import jax
import jax.numpy as jnp
import numpy as np
from jax import lax
from jax.experimental import pallas as pl
from jax.experimental.pallas import tpu as pltpu

F32 = jnp.float32
BF16 = jnp.bfloat16

D_MODEL = 1024
DEPTH = 2
GRID_W = 64
HEAD_DIM = 64
N_HEADS = 4
GROUP_W = 256
CHUNK = 128
NA_KH = 8
NA_KW = 16
MLA_Q_RANK = 256
MLA_KV_RANK = 128
MLA_NOPE = 64
MLA_ROPE = 32
MLA_SCALE = (MLA_NOPE + MLA_ROPE) ** -0.5
SWA_WIN = 128
PEER_HEADS = 8
PEER_KEYS = 128
PEER_TOPK = 16
N_EXPERTS = PEER_KEYS * PEER_KEYS
ROPE_BASE = 10000.0
LN_EPS = 1e-5
NEG_INF = -1e30
ALPHA = (2 * DEPTH) ** 0.25

V7X_VMEM_LIMIT_BYTES = 56 * 1024 * 1024
TM = 256
TB_ROUTE = 256
TB_EXP = 512
ATTN_KEY_CHUNK = 1024
EC = 1024
N_SLAB = EC // PEER_KEYS
EXP_PIECE = 256
_C_A = 0
_C_QB, _C_KB, _C_VB = 512, 768, 1024
_C_CQ, _C_CKV, _C_KR, _C_KRS = 1280, 1536, 1664, 1920
_C_QD, _C_QDS, _C_KD, _C_KDS, _C_VD = 2176, 2432, 2688, 2944, 3200
_C_END = 3456
PAIR_W = MLA_NOPE + MLA_ROPE


def _cparams(sem):
    return pltpu.CompilerParams(dimension_semantics=sem, vmem_limit_bytes=V7X_VMEM_LIMIT_BYTES)


def _dot(a, b):
    return jnp.dot(a, b, preferred_element_type=F32)


def _dot_nt(a, b):
    return lax.dot_general(a, b, (((1,), (1,)), ((), ())), preferred_element_type=F32)


def _layer_norm(x, g, b):
    mu = jnp.mean(x, axis=-1, keepdims=True)
    xc = x - mu
    var = jnp.mean(xc * xc, axis=-1, keepdims=True)
    return xc * lax.rsqrt(var + LN_EPS) * g + b


def _rms_norm(x, g):
    return x * lax.rsqrt(jnp.mean(x * x, axis=-1, keepdims=True) + LN_EPS) * g


def _ada_kernel(c_ref, w_ref, b_ref, o_ref):
    c = c_ref[...]
    a = c * jax.nn.sigmoid(c)
    a_hi = a.astype(BF16)
    a_lo = (a - a_hi.astype(F32)).astype(BF16)
    w = w_ref[0]
    w_hi = w.astype(BF16)
    w_lo = (w - w_hi.astype(F32)).astype(BF16)
    o_ref[0] = _dot(a_hi, w_hi) + _dot(a_hi, w_lo) + _dot(a_lo, w_hi) + b_ref[0]


def _ada_call(conds, w_ada, b_ada):
    tn = 1536
    n = w_ada.shape[-1]
    return pl.pallas_call(
        _ada_kernel,
        grid=(DEPTH, n // tn),
        in_specs=[pl.BlockSpec((8, D_MODEL), lambda l, j: (0, 0)),
                  pl.BlockSpec((1, D_MODEL, tn), lambda l, j: (l, 0, j)),
                  pl.BlockSpec((1, 1, tn), lambda l, j: (l, 0, j))],
        out_specs=pl.BlockSpec((1, 8, tn), lambda l, j: (l, 0, j)),
        out_shape=jax.ShapeDtypeStruct((DEPTH, 8, n), F32),
        compiler_params=_cparams(("parallel", "parallel")),
        name="ada_mod",
    )(conds, w_ada, b_ada.reshape(DEPTH, 1, n))


def _inproj_kernel(x_ref, mod_ref, w_ref, wuq_ref, wukv_ref, gq_ref, gkv_ref,
                   cosc_ref, sinc_ref, cosd_ref, sind_ref, ag_ref, ab_ref, aws_ref, abs_ref,
                   oa_ref, qb_ref, kb_ref, vb_ref, kb16_ref, vb16_ref,
                   qp0_ref, qp1_ref, ckvn_ref, krp_ref, kp0_ref, kp1_ref, vc_ref,
                   qd_ref, kd_ref, vd_ref, kd16_ref, vd16_ref):
    m = mod_ref[0]
    h = x_ref[...] * (1.0 + m[1:2]) + m[0:1]
    z = _dot(h.astype(BF16), w_ref[...])
    cosc, sinc = cosc_ref[...], sinc_ref[...]
    cosd, sind = cosd_ref[...], sind_ref[...]

    for c0 in range(0, z.shape[0], CHUNK):
        oa_ref[c0:c0 + CHUNK, :] = _chunk_mlp(z[c0:c0 + CHUNK, _C_A:_C_QB], ag_ref[...], ab_ref[...], aws_ref,
                                              abs_ref[...])
    qb_ref[...] = z[:, _C_QB:_C_KB].astype(BF16)
    kb = z[:, _C_KB:_C_VB]
    vb = z[:, _C_VB:_C_CQ]
    kb_ref[...] = kb
    vb_ref[...] = vb
    kb16_ref[...] = kb.astype(BF16)
    vb16_ref[...] = vb.astype(BF16)

    cqn = _rms_norm(z[:, _C_CQ:_C_CKV], gq_ref[...])
    q = _dot(cqn.astype(BF16), wuq_ref[...])
    qp0_ref[...] = (q[:, 0:256] * cosc + q[:, 512:768] * sinc).astype(BF16)
    qp1_ref[...] = (q[:, 256:512] * cosc + q[:, 768:1024] * sinc).astype(BF16)
    ckvn = _rms_norm(z[:, _C_CKV:_C_KR], gkv_ref[...])
    ckvn_ref[...] = ckvn
    kv = _dot(ckvn.astype(BF16), wukv_ref[...])
    kr = z[:, _C_KR:_C_KRS] * cosc + z[:, _C_KRS:_C_QD] * sinc
    krp_ref[...] = kr
    kp0_ref[...] = (kv[:, 0:256] + kr).astype(BF16)
    kp1_ref[...] = (kv[:, 256:512] + kr).astype(BF16)
    vc_ref[...] = kv[:, 512:768].astype(BF16)

    qd_ref[...] = (z[:, _C_QD:_C_QDS] * cosd + z[:, _C_QDS:_C_KD] * sind).astype(BF16)
    kd = z[:, _C_KD:_C_KDS] * cosd + z[:, _C_KDS:_C_VD] * sind
    vd = z[:, _C_VD:_C_END]
    kd_ref[...] = kd
    vd_ref[...] = vd
    kd16_ref[...] = kd.astype(BF16)
    vd16_ref[...] = vd.astype(BF16)


def _cond_of_tile(i, tile, n_ctx, lat_len):
    n_ctx_tiles = n_ctx // tile
    return jnp.where(i < n_ctx_tiles, 0, 1 + (i - n_ctx_tiles) // (lat_len // tile))


def _rope_block_of_tile(i, tile, n_ctx, lat_len):
    n_ctx_tiles = n_ctx // tile
    return jnp.where(i < n_ctx_tiles, 0, 1 + (i - n_ctx_tiles) % (lat_len // tile))


def _inproj_call(x, mod, wbig, wuq, wukv, gq, gkv, tabs, mixer_a, n_ctx, lat_len):
    nt = x.shape[0]
    cond = functools.partial(_cond_of_tile, tile=TM, n_ctx=n_ctx, lat_len=lat_len)
    rblk = functools.partial(_rope_block_of_tile, tile=TM, n_ctx=n_ctx, lat_len=lat_len)
    row = lambda w: pl.BlockSpec((TM, w), lambda i: (i, 0))
    const = lambda a: pl.BlockSpec(a.shape, lambda i: (0,) * a.ndim)
    tab = lambda w: pl.BlockSpec((TM, w), lambda i: (rblk(i), 0))
    outs = [(256, F32), (256, BF16), (256, F32), (256, F32), (256, BF16), (256, BF16),
            (256, BF16), (256, BF16), (128, F32), (256, F32), (256, BF16), (256, BF16), (256, BF16),
            (256, BF16), (256, F32), (256, F32), (256, BF16), (256, BF16)]
    return pl.pallas_call(
        _inproj_kernel,
        grid=(nt // TM,),
        in_specs=[row(D_MODEL),
                  pl.BlockSpec((1, 6, D_MODEL), lambda i: (cond(i), 0, 0)),
                  const(wbig), const(wuq), const(wukv), const(gq), const(gkv),
                  tab(256), tab(256), tab(256), tab(256)] + [const(a) for a in mixer_a],
        out_specs=[row(w) for w, _ in outs],
        out_shape=[jax.ShapeDtypeStruct((nt, w), dt) for w, dt in outs],
        compiler_params=_cparams(("parallel",)),
        name="inproj",
    )(x, mod, wbig, wuq, wukv, gq, gkv, *tabs, *mixer_a)


def _kvexp_kernel(c_ref, kr_ref, w_ref, k0_ref, k1_ref, v_ref):
    kv = _dot(c_ref[...].astype(BF16), w_ref[...])
    kr = kr_ref[...]
    k0_ref[...] = (kv[:, 0:256] + kr).astype(BF16)
    k1_ref[...] = (kv[:, 256:512] + kr).astype(BF16)
    v_ref[...] = kv[:, 512:768].astype(BF16)


def _kvexp_call(ckv, kr_pair, wukv):
    n = ckv.shape[0]
    whole = lambda a: pl.BlockSpec(a.shape, lambda i: (0, 0))
    return pl.pallas_call(
        _kvexp_kernel,
        grid=(1,),
        in_specs=[whole(ckv), whole(kr_pair), whole(wukv)],
        out_specs=[pl.BlockSpec((n, 256), lambda i: (0, 0))] * 3,
        out_shape=[jax.ShapeDtypeStruct((n, 256), BF16)] * 3,
        compiler_params=_cparams(("arbitrary",)),
        name="mla_cache_expand",
    )(ckv, kr_pair, wukv)


def _chunk_mlp(z, gain, bias, ws_ref, bs):
    g = jax.nn.gelu(z)
    u = g[:, 0:GROUP_W]
    v = _layer_norm(g[:, GROUP_W:2 * GROUP_W], gain, bias).astype(BF16)
    lane_head = lax.broadcasted_iota(jnp.int32, (1, GROUP_W), 1) // HEAD_DIM
    mixed = bs
    for hd in range(N_HEADS):
        mixed = mixed + jnp.where(lane_head == hd, _dot(ws_ref[hd], v), 0.0)
    return u * mixed


def _attn_kernel(*refs, tq, wk, kc, scale, has_q2, has_extra, bias_heads, has_sink, start_fn):
    refs = list(refs)
    sink_ref = refs.pop(0) if has_sink else None
    q1_ref = refs.pop(0)
    q2_ref = refs.pop(0) if has_q2 else None
    k1_ref = refs.pop(0)
    k2_ref = refs.pop(0) if has_q2 else None
    v_ref = refs.pop(0)
    if has_extra:
        xk1_ref = refs.pop(0)
        xk2_ref = refs.pop(0) if has_q2 else None
        xv_ref = refs.pop(0)
    bias_ref = refs.pop(0) if bias_heads else None
    o_ref = refs.pop(0)

    def stack_heads(q, width, n):
        lane_head = lax.broadcasted_iota(jnp.int32, (1, q.shape[1]), 1) // width
        return jnp.concatenate([jnp.where(lane_head == hd, q, jnp.zeros_like(q)) for hd in range(n)], axis=0)

    if has_q2:
        q1s = stack_heads(q1_ref[...], PAIR_W, 2)
        q2s = stack_heads(q2_ref[...], PAIR_W, 2)
    else:
        q1s = stack_heads(q1_ref[...], HEAD_DIM, N_HEADS)

    start = start_fn(pl.program_id(1))
    if not isinstance(start, int):
        start = pl.multiple_of(start, 64)

    chunks = [("win", c0, min(kc, wk - c0)) for c0 in range(0, wk, kc)]
    if has_extra:
        chunks.append(("extra", 0, 0))
    sink = None
    if has_sink:
        sink = sink_ref[:, 0:1]
    m = denom = o = None
    for kind, c0, n in chunks:
        if kind == "win":
            rows = pl.ds(start if c0 == 0 else start + c0, n)
            k1c, vc = k1_ref[rows, :], v_ref[rows, :]
            k2c = k2_ref[rows, :] if has_q2 else None
        else:
            k1c, vc = xk1_ref[...], xv_ref[...]
            k2c = xk2_ref[...] if has_q2 else None
        s = _dot_nt(q1s, k1c)
        if has_q2:
            s = jnp.concatenate([s, _dot_nt(q2s, k2c)], axis=0)
        s = s * scale
        if kind == "win" and bias_heads == N_HEADS:
            s = s + bias_ref[:, c0:c0 + n]
        elif kind == "win" and bias_heads == 1:
            s = s + jnp.concatenate([bias_ref[:, c0:c0 + n]] * N_HEADS, axis=0)
        mc = jnp.max(s, axis=-1, keepdims=True)
        if m is None:
            m_new = mc if sink is None else jnp.maximum(mc, sink)
        else:
            m_new = jnp.maximum(m, mc)
        p = jnp.exp(s - m_new)
        pv = _dot(p.astype(BF16), vc)
        if m is None:
            denom, o = jnp.sum(p, axis=-1, keepdims=True), pv
        else:
            alpha = jnp.exp(m - m_new)
            denom = alpha * denom + jnp.sum(p, axis=-1, keepdims=True)
            o = alpha * o + pv
        m = m_new
    if has_sink:
        denom = denom + jnp.exp(sink - m)
    o = o / denom
    lane_head = lax.broadcasted_iota(jnp.int32, (1, N_HEADS * HEAD_DIM), 1) // HEAD_DIM
    out = jnp.zeros((tq, N_HEADS * HEAD_DIM), F32)
    for hd in range(N_HEADS):
        out = out + jnp.where(lane_head == hd, o[hd * tq:(hd + 1) * tq], 0.0)
    o_ref[...] = out


def _attn_call(name, grid, tq, wk, scale, q_index, q1, k1, v, k_spec_fn, *, q2=None, k2=None,
               extra=None, bias=None, bias_index=None, sinks=None, start_fn=lambda j: 0, kc=ATTN_KEY_CHUNK):
    has_q2 = q2 is not None
    has_extra = extra is not None
    bias_heads = 0 if bias is None else bias.shape[1] // tq
    args, specs = [], []
    if sinks is not None:
        sink_rows = jnp.broadcast_to(jnp.repeat(sinks.astype(F32), tq)[:, None], (N_HEADS * tq, 128))
        args.append(sink_rows)
        specs.append(pl.BlockSpec(sink_rows.shape, lambda b, j: (0, 0)))
    args.append(q1)
    specs.append(pl.BlockSpec((tq, 256), lambda b, j: (q_index(b, j), 0)))
    if has_q2:
        args.append(q2)
        specs.append(pl.BlockSpec((tq, 256), lambda b, j: (q_index(b, j), 0)))
    args.append(k1)
    specs.append(k_spec_fn(256))
    if has_q2:
        args.append(k2)
        specs.append(k_spec_fn(256))
    args.append(v)
    specs.append(k_spec_fn(256))
    if has_extra:
        for a in extra:
            args.append(a)
            specs.append(pl.BlockSpec((None,) + a.shape[1:], lambda b, j: (b, 0, 0)))
    if bias is not None:
        args.append(bias)
        specs.append(pl.BlockSpec((None,) + bias.shape[1:], lambda b, j: (bias_index(j), 0, 0)))
    kern = functools.partial(_attn_kernel, tq=tq, wk=wk, kc=kc, scale=scale, has_q2=has_q2, has_extra=has_extra,
                             bias_heads=bias_heads, has_sink=sinks is not None, start_fn=start_fn)
    return pl.pallas_call(
        kern,
        grid=grid,
        in_specs=specs,
        out_specs=pl.BlockSpec((tq, 256), lambda b, j: (b * grid[1] + j, 0)),
        out_shape=jax.ShapeDtypeStruct((grid[0] * grid[1] * tq, 256), F32),
        compiler_params=_cparams(("parallel", "parallel")),
        name=name,
    )(*args)


def _merge_kernel(oa_ref, obc_ref, obl_ref, occ_ref, ocl_ref, odc_ref, odl_ref, x_ref, mod_ref, g_ref, w_ref,
                  lg_ref, lb_ref, o_ref, *, n_ctx_tiles):
    m = mod_ref[0]
    is_ctx = pl.program_id(0) < n_ctx_tiles
    pick = lambda c_ref, l_ref: jnp.where(is_ctx, c_ref[...], l_ref[...])
    groups = (oa_ref[...], pick(obc_ref, obl_ref), pick(occ_ref, ocl_ref), pick(odc_ref, odl_ref))
    acc = None
    for gi, o in enumerate(groups):
        og = _rms_norm(o, g_ref[:, gi * GROUP_W:(gi + 1) * GROUP_W]).astype(BF16)
        part = _dot(og, w_ref[gi * GROUP_W:(gi + 1) * GROUP_W, :])
        acc = part if acc is None else acc + part
    y = ALPHA * x_ref[...] + m[2:3] * acc
    o_ref[...] = _layer_norm(y, lg_ref[...], lb_ref[...])


def _merge_call(oa, ob, oc, od, x, mod, gout, wout16, lg, lb, n_ctx, lat_len):
    nt = x.shape[0]
    nct = n_ctx // TM
    cond = functools.partial(_cond_of_tile, tile=TM, n_ctx=n_ctx, lat_len=lat_len)
    row = lambda w: pl.BlockSpec((TM, w), lambda i: (i, 0))
    ctx_row = lambda: pl.BlockSpec((TM, 256), lambda i: (jnp.minimum(i, nct - 1), 0))
    lat_row = lambda: pl.BlockSpec((TM, 256), lambda i: (jnp.maximum(i - nct, 0), 0))
    const = lambda a: pl.BlockSpec(a.shape, lambda i: (0,) * a.ndim)
    return pl.pallas_call(
        functools.partial(_merge_kernel, n_ctx_tiles=nct),
        grid=(nt // TM,),
        in_specs=[row(256), ctx_row(), lat_row(), ctx_row(), lat_row(), ctx_row(), lat_row(), row(D_MODEL),
                  pl.BlockSpec((1, 6, D_MODEL), lambda i: (cond(i), 0, 0)),
                  const(gout), const(wout16), const(lg), const(lb)],
        out_specs=row(D_MODEL),
        out_shape=jax.ShapeDtypeStruct((nt, D_MODEL), F32),
        compiler_params=_cparams(("parallel",)),
        name="merge_out",
    )(oa, *ob, *oc, *od, x, mod, gout, wout16, lg, lb)


_N_TOP = PEER_TOPK + 1
_GELU_C0 = math.sqrt(2.0 / math.pi)
_GELU_C1 = 0.044715 * _GELU_C0


def _sort_network(n):
    pairs = []
    p = 1
    while p < n:
        k = p
        while k >= 1:
            for j in range(k % p, n - k, 2 * k):
                for i in range(min(k, n - j - k)):
                    if (i + j) // (2 * p) == (i + j + k) // (2 * p):
                        pairs.append((i + j, i + j + k))
            k //= 2
        p *= 2
    return pairs


def _merge_top(levels, n_top):
    levels = list(levels)
    sub = lax.broadcasted_iota(jnp.int32, levels[0].shape, 0).astype(F32)
    out = []
    for k in range(n_top):
        head = levels[0]
        m = jnp.max(head, axis=0, keepdims=True)
        out.append(m)
        first = jnp.min(jnp.where(head == m, sub, 8.0), axis=0, keepdims=True)
        pop = sub == first
        for v in range(n_top - 1 - k):
            nxt = levels[v + 1] if v + 1 < len(levels) else -jnp.inf
            levels[v] = jnp.where(pop, nxt, levels[v])
    return out


def _top_values(s, n_top):
    g = [s[8 * v:8 * v + 8] for v in range(s.shape[0] // 8)]
    for i, j in _sort_network(len(g)):
        g[i], g[j] = jnp.maximum(g[i], g[j]), jnp.minimum(g[i], g[j])
    return _merge_top(g, n_top)


_CAND_LEVELS = (17, 8, 5, 4, 13, 4, 1, 0)


def _candidate_levels(sv0, sv1):
    shape = (8,) + sv0[0].shape[1:]
    row = lax.broadcasted_iota(jnp.int32, shape, 0)
    pick = lambda vals, default: functools.reduce(
        lambda acc, rv: jnp.where(row == rv[0], rv[1], acc), vals, jnp.full(shape, default, F32))
    fixed0 = pick([(r, sv0[r]) for r in range(4)], 0.0)
    fixed1 = pick([(4 + r, sv1[r]) for r in range(3)], 0.0)
    n_valid = pick([(r, float(n)) for r, n in enumerate(_CAND_LEVELS)], 0.0)
    levels = []
    for v in range(max(_CAND_LEVELS)):
        moving0 = sv0[min(4 + v, len(sv0) - 1)]
        lv = jnp.where(row < 4, fixed0 + sv1[v], moving0 + fixed1)
        levels.append(jnp.where(n_valid > float(v), lv, -jnp.inf))
    return levels


def _route_kernel(x_ref, mod_ref, wq_ref, keys_ref, h2_ref, p0_ref, te_ref, e1_ref, st_scr):
    m = mod_ref[0]
    h2 = (x_ref[...] * (1.0 + m[4:5]) + m[3:4]).astype(BF16)
    h2_ref[...] = h2
    q = _dot(h2, wq_ref[...])
    tb = q.shape[0]
    for p in range(2):
        kp = keys_ref[p]
        k_hi = kp.astype(BF16)
        k_lo = (kp - k_hi.astype(F32)).astype(BF16)
        for hd in range(PEER_HEADS):
            c0 = (hd * 2 + p) * PEER_KEYS
            qs = q[:, c0:c0 + PEER_KEYS]
            q_hi = qs.astype(BF16)
            q_lo = (qs - q_hi.astype(F32)).astype(BF16)
            st_scr[hd * 2 + p] = _dot_nt(k_hi, q_hi) + _dot_nt(k_lo, q_hi) + _dot_nt(k_hi, q_lo)

    def per_head(hd, carry):
        s0 = st_scr[hd * 2]
        s1 = st_scr[hd * 2 + 1]
        sv0 = _top_values(s0, _N_TOP)
        sv1 = _top_values(s1, _N_TOP)
        c = _merge_top(_candidate_levels(sv0, sv1), _N_TOP)
        thr = 0.5 * (c[PEER_TOPK - 1] + c[PEER_TOPK])
        z = jnp.zeros_like(thr)
        for k in range(PEER_TOPK):
            z = z + jnp.exp(c[k] - c[0])
        p0_ref[hd] = jnp.exp(s0 - sv0[0]) * (0.5 / z)
        te_ref[hd] = jnp.exp((thr - sv1[0]) - s0)
        e1_ref[hd] = jnp.exp(s1 - sv1[0])
        return carry

    lax.fori_loop(0, PEER_HEADS, per_head, 0)


def _route_call(x1, mod, wq16, keys, n_ctx, lat_len):
    nt = x1.shape[0]
    tb = TB_ROUTE
    cond = functools.partial(_cond_of_tile, tile=tb, n_ctx=n_ctx, lat_len=lat_len)
    fac = lambda: pl.BlockSpec((PEER_HEADS, PEER_KEYS, tb), lambda i: (0, 0, i))
    fshape = jax.ShapeDtypeStruct((PEER_HEADS, PEER_KEYS, nt), F32)
    return pl.pallas_call(
        _route_kernel,
        grid=(nt // tb,),
        in_specs=[pl.BlockSpec((tb, D_MODEL), lambda i: (i, 0)),
                  pl.BlockSpec((1, 6, D_MODEL), lambda i: (cond(i), 0, 0)),
                  pl.BlockSpec(wq16.shape, lambda i: (0, 0)),
                  pl.BlockSpec(keys.shape, lambda i: (0, 0, 0))],
        out_specs=[pl.BlockSpec((tb, D_MODEL), lambda i: (i, 0)), fac(), fac(), fac()],
        out_shape=[jax.ShapeDtypeStruct((nt, D_MODEL), BF16), fshape, fshape, fshape],
        scratch_shapes=[pltpu.VMEM((2 * PEER_HEADS, PEER_KEYS, tb), F32)],
        compiler_params=_cparams(("parallel",)),
        name="peer_route",
    )(x1, mod, wq16, keys)


def _expert_kernel(h2_ref, u_ref, vt_ref, p0_ref, te_ref, e1_ref, x_ref, mod_ref, lg_ref, lb_ref,
                   o_ref, acc_scr, act_scr):
    c = pl.program_id(1)
    n_lane = h2_ref.shape[0] // PEER_KEYS

    @pl.when(c == 0)
    def _():
        acc_scr[...] = jnp.zeros_like(acc_scr)

    st = _dot_nt(u_ref[...], h2_ref[...])
    th = jnp.tanh(st * (_GELU_C0 + _GELU_C1 * (st * st)))
    act_scr[...] = st + st * th
    for pc in range(EC // EXP_PIECE):
        slabs = range(pc * EXP_PIECE // PEER_KEYS, (pc + 1) * EXP_PIECE // PEER_KEYS)
        g_lanes = []
        for ln in range(n_lane):
            lanes = slice(ln * PEER_KEYS, (ln + 1) * PEER_KEYS)
            w = [None] * len(slabs)
            for hd in range(PEER_HEADS):
                e1 = e1_ref[hd, :, lanes]
                for k, sl in enumerate(slabs):
                    te = te_ref[hd, 0, sl:sl + 1, lanes]
                    e0 = p0_ref[hd, 0, sl:sl + 1, lanes]
                    term = e0 * jnp.where(e1 > te, e1, 0.0)
                    w[k] = term if w[k] is None else w[k] + term
            g_lanes.append(jnp.concatenate(
                [(w[k] * act_scr[sl * PEER_KEYS:(sl + 1) * PEER_KEYS, lanes]).astype(BF16)
                 for k, sl in enumerate(slabs)], axis=0))
        g = jnp.concatenate(g_lanes, axis=1)
        acc_scr[...] += _dot(vt_ref[:, pc * EXP_PIECE:(pc + 1) * EXP_PIECE], g)

    @pl.when(c == pl.num_programs(1) - 1)
    def _():
        m = mod_ref[0]
        y = ALPHA * x_ref[...] + m[5:6] * acc_scr[...].T
        o_ref[...] = _layer_norm(y, lg_ref[...], lb_ref[...])


def _expert_call(h2, u16, vt16, p0, te, e1, x1, mod, lg, lb, n_ctx, lat_len):
    nt = x1.shape[0]
    tb = TB_EXP
    n_chunks = N_EXPERTS // EC
    cond = functools.partial(_cond_of_tile, tile=tb, n_ctx=n_ctx, lat_len=lat_len)
    p0r = p0.reshape(PEER_HEADS, n_chunks, N_SLAB, nt)
    ter = te.reshape(PEER_HEADS, n_chunks, N_SLAB, nt)
    slab = lambda: pl.BlockSpec((PEER_HEADS, 1, N_SLAB, tb), lambda i, c: (0, c, 0, i))
    full = lambda: pl.BlockSpec((PEER_HEADS, PEER_KEYS, tb), lambda i, c: (0, 0, i))
    const = lambda a: pl.BlockSpec(a.shape, lambda i, c: (0,) * a.ndim)
    return pl.pallas_call(
        _expert_kernel,
        grid=(nt // tb, n_chunks),
        in_specs=[pl.BlockSpec((tb, D_MODEL), lambda i, c: (i, 0)),
                  pl.BlockSpec((EC, D_MODEL), lambda i, c: (c, 0)),
                  pl.BlockSpec((D_MODEL, EC), lambda i, c: (0, c)),
                  slab(), slab(), full(),
                  pl.BlockSpec((tb, D_MODEL), lambda i, c: (i, 0)),
                  pl.BlockSpec((1, 6, D_MODEL), lambda i, c: (cond(i), 0, 0)),
                  const(lg), const(lb)],
        out_specs=pl.BlockSpec((tb, D_MODEL), lambda i, c: (i, 0)),
        out_shape=jax.ShapeDtypeStruct((nt, D_MODEL), F32),
        scratch_shapes=[pltpu.VMEM((D_MODEL, tb), F32), pltpu.VMEM((EC, tb), F32)],
        compiler_params=_cparams(("parallel", "arbitrary")),
        name="peer_experts",
    )(h2, u16, vt16, p0r, ter, e1, x1, mod, lg, lb)


def _swap_halves(w, n_heads, rot):
    k = w.shape[0]
    w4 = w.reshape(k, n_heads, 2, rot // 2)
    return jnp.concatenate([w4[:, :, 1:2], w4[:, :, 0:1]], axis=2).reshape(k, n_heads * rot)


def _inproj_weight(w):
    a = w[:, 0:512]
    qb, kb, vb = w[:, 512:768], w[:, 768:1024], w[:, 1024:1280]
    cq, ckv, kr = w[:, 1280:1536], w[:, 1536:1664], w[:, 1664:1696]
    qd, kd, vd = w[:, 1696:1952], w[:, 1952:2080], w[:, 2080:2208]
    rep = lambda m: jnp.repeat(m.reshape(D_MODEL, 2, HEAD_DIM), 2, axis=1).reshape(D_MODEL, 256)
    kdr = rep(kd)
    cols = [a, qb, kb, vb, cq, ckv, _pair_cols(None, [kr, kr]), _pair_cols(None, [_swap_halves(kr, 1, MLA_ROPE)] * 2),
            qd, _swap_halves(qd, N_HEADS, HEAD_DIM), kdr, _swap_halves(kdr, N_HEADS, HEAD_DIM), rep(vd)]
    return jnp.concatenate(cols, axis=1).astype(BF16)


def _pair_cols(nope, rope):
    k = rope[0].shape[0]
    zeros = lambda w: jnp.zeros((k, w), rope[0].dtype)
    nope = nope if nope is not None else [zeros(MLA_NOPE)] * 2
    rope = rope if rope is not None else [zeros(MLA_ROPE)] * 2
    return jnp.concatenate([nope[0], rope[0], nope[1], rope[1], zeros(256 - 2 * PAIR_W)], axis=1)


def _mla_weights(w_uq, w_ukv):
    q3 = w_uq.reshape(MLA_Q_RANK, N_HEADS, PAIR_W)
    nope = [q3[:, hd, :MLA_NOPE] for hd in range(N_HEADS)]
    rope = [q3[:, hd, MLA_NOPE:] for hd in range(N_HEADS)]
    rope_sw = [_swap_halves(r, 1, MLA_ROPE) for r in rope]
    none64 = [jnp.zeros_like(nope[0])] * 2
    wuq = jnp.concatenate([_pair_cols(nope[0:2], rope[0:2]), _pair_cols(nope[2:4], rope[2:4]),
                           _pair_cols(none64, rope_sw[0:2]), _pair_cols(none64, rope_sw[2:4])], axis=1).astype(BF16)
    kv3 = w_ukv.reshape(MLA_KV_RANK, N_HEADS, MLA_NOPE + 64)
    knope = [kv3[:, hd, :MLA_NOPE] for hd in range(N_HEADS)]
    none32 = [jnp.zeros((MLA_KV_RANK, MLA_ROPE), w_ukv.dtype)] * 2
    wukv = jnp.concatenate([_pair_cols(knope[0:2], none32), _pair_cols(knope[2:4], none32),
                            kv3[:, :, MLA_NOPE:].reshape(MLA_KV_RANK, 256)], axis=1).astype(BF16)
    return wuq, wukv


def _pair_rope_tables(lat_len, tile):
    cos_t, sin_t = _rope_tables(lat_len, MLA_ROPE, tile)
    cos32, sin32 = cos_t[:, :MLA_ROPE], sin_t[:, :MLA_ROPE]
    n = cos_t.shape[0]
    one, zero = jnp.ones((n, MLA_NOPE), F32), jnp.zeros((n, MLA_NOPE), F32)
    return (jnp.concatenate([one, cos32, one, cos32, one], axis=1),
            jnp.concatenate([zero, sin32, zero, sin32, zero], axis=1))


def _rope_tables(lat_len, rot, tile):
    t = jnp.arange(lat_len)
    row = (t // GRID_W).astype(F32)
    col = (t % GRID_W).astype(F32)
    nf = rot // 4
    freqs = ROPE_BASE ** (-jnp.arange(nf, dtype=F32) / nf)
    ang = jnp.concatenate([row[:, None] * freqs, col[:, None] * freqs], -1)
    cos, sin = jnp.cos(ang), jnp.sin(ang)
    cos_t = jnp.tile(jnp.concatenate([cos, cos], -1), (1, N_HEADS))
    sin_t = jnp.tile(jnp.concatenate([-sin, sin], -1), (1, N_HEADS))
    w = N_HEADS * rot
    return (jnp.concatenate([jnp.ones((tile, w), F32), cos_t], 0),
            jnp.concatenate([jnp.zeros((tile, w), F32), sin_t], 0))


def _na_bias_tables(rpb, rows):
    kh = min(NA_KH, rows)
    qrow = np.array([0, 1, 2, 3, rows // 2, rows - 3, rows - 2, rows - 1])
    start = np.clip(qrow - kh // 2, 0, rows - kh)
    dr0 = start - qrow + NA_KH - 1
    qc = np.arange(GRID_W)
    kc = np.arange(GRID_W)
    cstart = np.clip(qc - NA_KW // 2, 0, GRID_W - NA_KW)
    ok = (kc[None, :] >= cstart[:, None]) & (kc[None, :] < cstart[:, None] + NA_KW)
    n_h, n_a, n_c = rpb.shape
    edge = GRID_W - NA_KW
    w = jnp.concatenate([jnp.broadcast_to(rpb[:, :, :1], (n_h, n_a, edge)), rpb,
                         jnp.broadcast_to(rpb[:, :, -1:], (n_h, n_a, edge + 1))], axis=-1)
    skew = jnp.tile(w, (1, 1, GRID_W))[:, :, :GRID_W * (2 * GRID_W - 1)].reshape(n_h, n_a, GRID_W, 2 * GRID_W - 1)
    toep = skew[:, :, :, GRID_W - 1:]
    tabs = []
    for d0 in dr0:
        b = jnp.where(ok[None, None], toep[:, d0:d0 + kh], NEG_INF)
        tabs.append(b.transpose(0, 2, 1, 3).reshape(N_HEADS * GRID_W, kh * GRID_W))
    return jnp.stack(tabs)


def _swa_bias_tables(lat_len):
    nb = lat_len // SWA_WIN
    qi = np.arange(SWA_WIN)
    kj = np.arange(3 * SWA_WIN)
    tabs = []
    for n in (0, 1, nb - 1):
        kpos = _swa_window_start(n, nb) + kj
        in_win = np.abs(kpos[None, :] - (n * SWA_WIN + qi[:, None])) <= SWA_WIN
        tabs.append(np.where(in_win, 0.0, NEG_INF))
    return jnp.asarray(np.stack(tabs), F32)


def _swa_window_start(n, nb):
    lo = n - 1
    lo = jnp.clip(lo, 0, nb - 3) if isinstance(n, jax.Array) else min(max(lo, 0), nb - 3)
    return lo * SWA_WIN


def _heads_to_lanes(t):
    b, h, s, d = t.shape
    return t.transpose(0, 2, 1, 3).reshape(b, s, h * d)


def _lanes_to_heads(t, b, s, h):
    return t.reshape(b, s, h, -1).transpose(0, 2, 1, 3)


def kernel(x_prompt, x_sample, cache_nat_k, cache_nat_v, cache_mla_ckv, cache_mla_krope, cache_swa_k, cache_swa_v, c, c_ctx, w_in, w_out, out_norm_g, w_ada, b_ada, ln1_g, ln1_b, ln2_g, ln2_b, a_norm_g, a_norm_b, a_w_s, a_b_s, nat_rpb, mla_q_norm_g, mla_w_uq, mla_kv_norm_g, mla_w_ukv, swa_sinks, peer_w_q, peer_sub_keys, peer_u, peer_v):
    n_b, seq, d = x_prompt.shape
    n_db, lat_len, _ = x_sample.shape
    past = cache_nat_k.shape[3]
    n_ctx = n_b * seq
    nt = n_ctx + n_db * lat_len
    rows = lat_len // GRID_W

    x = jnp.concatenate([x_prompt.reshape(n_ctx, d), x_sample.reshape(n_db * lat_len, d)], axis=0)
    conds = jnp.zeros((8, d), F32).at[0].set(c_ctx).at[1:1 + n_db].set(c)
    mods = _ada_call(conds, w_ada, b_ada).reshape(DEPTH, 8, 6, d)

    cos_c, sin_c = _pair_rope_tables(lat_len, TM)
    cos_d, sin_d = _rope_tables(lat_len, HEAD_DIM, TM)
    swa_bias = _swa_bias_tables(lat_len)
    nb_swa = lat_len // SWA_WIN

    q_ctx = lambda b, j: b
    lat_q = lambda tq: (lambda b, j: (n_ctx + b * lat_len) // tq + j)
    ctx_keys = lambda w: pl.BlockSpec((seq, w), lambda b, j: (b, 0))
    lat_keys = lambda w: pl.BlockSpec((lat_len, w), lambda b, j: (n_ctx // lat_len + b, 0))

    states = [[] for _ in range(6)]
    for l in range(DEPTH):
        mod = mods[l]
        wbig = _inproj_weight(w_in[l])
        wuq, wukv = _mla_weights(mla_w_uq[l], mla_w_ukv[l])
        mixer_a = (a_norm_g[l][None], a_norm_b[l][None], a_w_s[l].astype(BF16),
                   jnp.repeat(a_b_s[l].T, HEAD_DIM, axis=1))
        (oa, qb, kb, vb, kb16, vb16, qp0, qp1, ckvn, krp, kp0, kp1, vc,
         qd, kd, vd, kd16, vd16) = _inproj_call(
            x, mod, wbig, wuq, wukv, mla_q_norm_g[l][None], mla_kv_norm_g[l][None],
            (cos_c, sin_c, cos_d, sin_d), mixer_a, n_ctx, lat_len)

        ob_c = _attn_call("ctx_attn_b", (n_b, 1), seq, seq, HEAD_DIM ** -0.5, q_ctx, qb, kb16, vb16, ctx_keys)
        oc_c = _attn_call("ctx_attn_c", (n_b, 1), seq, seq, MLA_SCALE, q_ctx, qp0, kp0, vc, ctx_keys,
                          q2=qp1, k2=kp1)
        od_c = _attn_call("ctx_attn_d", (n_b, 1), seq, seq, HEAD_DIM ** -0.5, q_ctx, qd, kd16, vd16, ctx_keys,
                          sinks=swa_sinks[l])

        na_bias = _na_bias_tables(nat_rpb[l], rows)
        kh = min(NA_KH, rows)
        ob_l = _attn_call(
            "lat_attn_b", (n_db, rows), GRID_W, kh * GRID_W, HEAD_DIM ** -0.5, lat_q(GRID_W), qb, kb16, vb16, lat_keys,
            extra=(_heads_to_lanes(cache_nat_k[:, l]).astype(BF16), _heads_to_lanes(cache_nat_v[:, l]).astype(BF16)),
            bias=na_bias,
            bias_index=lambda r: jnp.where(r < 4, r, jnp.where(r > rows - 4, r - (rows - 8), 4)),
            start_fn=lambda r: jnp.clip(r - kh // 2, 0, rows - kh) * GRID_W)
        krx = cache_mla_krope[:, l].reshape(n_db * past, MLA_ROPE)
        kx0, kx1, vx = _kvexp_call(cache_mla_ckv[:, l].reshape(n_db * past, MLA_KV_RANK),
                                   _pair_cols(None, [krx, krx]), wukv)
        oc_l = _attn_call(
            "lat_attn_c", (n_db, lat_len // 256), 256, lat_len, MLA_SCALE, lat_q(256), qp0, kp0, vc, lat_keys,
            q2=qp1, k2=kp1,
            extra=tuple(t.reshape(n_db, past, 256) for t in (kx0, kx1, vx)))
        rep_kv = lambda t: jnp.repeat(t, 2, axis=1)
        od_l = _attn_call(
            "lat_attn_d", (n_db, nb_swa), SWA_WIN, 3 * SWA_WIN, HEAD_DIM ** -0.5, lat_q(SWA_WIN), qd,
            kd16, vd16, lat_keys,
            extra=(_heads_to_lanes(rep_kv(cache_swa_k[:, l])).astype(BF16),
                   _heads_to_lanes(rep_kv(cache_swa_v[:, l])).astype(BF16)),
            bias=swa_bias,
            bias_index=lambda n: jnp.where(n == 0, 0, jnp.where(n == nb_swa - 1, 2, 1)),
            sinks=swa_sinks[l],
            start_fn=lambda n: _swa_window_start(n, nb_swa))
        ob, oc, od = (ob_c, ob_l), (oc_c, oc_l), (od_c, od_l)

        x1 = _merge_call(oa, ob, oc, od, x, mod, out_norm_g[l][None], w_out[l].astype(BF16),
                         ln1_g[l][None], ln1_b[l][None], n_ctx, lat_len)
        h2, p0, te, e1 = _route_call(x1, mod, peer_w_q[l].astype(BF16), peer_sub_keys[l], n_ctx, lat_len)
        x = _expert_call(h2, peer_u[l].astype(BF16), peer_v[l].T.astype(BF16), p0, te, e1, x1, mod,
                         ln2_g[l][None], ln2_b[l][None], n_ctx, lat_len)

        states[0].append(_lanes_to_heads(kb[:n_ctx], n_b, seq, N_HEADS))
        states[1].append(_lanes_to_heads(vb[:n_ctx], n_b, seq, N_HEADS))
        states[2].append(ckvn[:n_ctx].reshape(n_b, seq, MLA_KV_RANK))
        states[3].append(krp[:n_ctx, MLA_NOPE:PAIR_W].reshape(n_b, seq, MLA_ROPE))
        states[4].append(_lanes_to_heads(kd[:n_ctx], n_b, seq, N_HEADS)[:, ::2])
        states[5].append(_lanes_to_heads(vd[:n_ctx], n_b, seq, N_HEADS)[:, ::2])

    y_prompt = x[:n_ctx].reshape(n_b, seq, d)
    y_sample = x[n_ctx:].reshape(n_db, lat_len, d)
    return (y_prompt, y_sample) + tuple(jnp.stack(s, axis=1) for s in states)
```

```python
import functools
import math

import jax
import jax.numpy as jnp
import numpy as np
from jax import lax
from jax.experimental import pallas as pl
from jax.experimental.pallas import tpu as pltpu

F32 = jnp.float32
BF16 = jnp.bfloat16

D_MODEL = 1024
DEPTH = 2
GRID_W = 64
HEAD_DIM = 64
N_HEADS = 4
GROUP_W = 256
CHUNK = 128
NA_KH = 8
NA_KW = 16
MLA_Q_RANK = 256
MLA_KV_RANK = 128
MLA_NOPE = 64
MLA_ROPE = 32
MLA_SCALE = (MLA_NOPE + MLA_ROPE) ** -0.5
SWA_WIN = 128
PEER_HEADS = 8
PEER_KEYS = 128
PEER_TOPK = 16
N_EXPERTS = PEER_KEYS * PEER_KEYS
ROPE_BASE = 10000.0
LN_EPS = 1e-5
NEG_INF = -1e30
ALPHA = (2 * DEPTH) ** 0.25

V7X_VMEM_LIMIT_BYTES = 56 * 1024 * 1024
TM = 256
TB_ROUTE = 256
TB_EXP = 512
ATTN_KEY_CHUNK = 1024
NA_ROWS_PER_STEP = 4
SWA_BLOCKS_PER_STEP = 1
EC = 1024
N_SLAB = EC // PEER_KEYS
EXP_PIECE = 256
_C_A = 0
_C_QB, _C_KB, _C_VB = 512, 768, 1024
_C_CQ, _C_CKV, _C_KR, _C_KRS = 1280, 1536, 1664, 1920
_C_QD, _C_QDS, _C_KD, _C_KDS, _C_VD = 2176, 2432, 2688, 2944, 3200
_C_END = 3456
PAIR_W = MLA_NOPE + MLA_ROPE


def _cparams(sem):
    return pltpu.CompilerParams(dimension_semantics=sem, vmem_limit_bytes=V7X_VMEM_LIMIT_BYTES)


def _dot(a, b):
    return jnp.dot(a, b, preferred_element_type=F32)


def _dot_nt(a, b):
    return lax.dot_general(a, b, (((1,), (1,)), ((), ())), preferred_element_type=F32)


def _layer_norm(x, g, b):
    mu = jnp.mean(x, axis=-1, keepdims=True)
    xc = x - mu
    var = jnp.mean(xc * xc, axis=-1, keepdims=True)
    return xc * lax.rsqrt(var + LN_EPS) * g + b


def _rms_norm(x, g):
    return x * lax.rsqrt(jnp.mean(x * x, axis=-1, keepdims=True) + LN_EPS) * g


def _ada_kernel(c_ref, w_ref, b_ref, o_ref):
    c = c_ref[...]
    a = c * jax.nn.sigmoid(c)
    a_hi = a.astype(BF16)
    a_lo = (a - a_hi.astype(F32)).astype(BF16)
    w = w_ref[0]
    w_hi = w.astype(BF16)
    w_lo = (w - w_hi.astype(F32)).astype(BF16)
    o_ref[0] = _dot(a_hi, w_hi) + _dot(a_hi, w_lo) + _dot(a_lo, w_hi) + b_ref[0]


def _ada_call(conds, w_ada, b_ada):
    tn = 1536
    n = w_ada.shape[-1]
    return pl.pallas_call(
        _ada_kernel,
        grid=(DEPTH, n // tn),
        in_specs=[pl.BlockSpec((8, D_MODEL), lambda l, j: (0, 0)),
                  pl.BlockSpec((1, D_MODEL, tn), lambda l, j: (l, 0, j)),
                  pl.BlockSpec((1, 1, tn), lambda l, j: (l, 0, j))],
        out_specs=pl.BlockSpec((1, 8, tn), lambda l, j: (l, 0, j)),
        out_shape=jax.ShapeDtypeStruct((DEPTH, 8, n), F32),
        compiler_params=_cparams(("parallel", "parallel")),
        name="ada_mod",
    )(conds, w_ada, b_ada.reshape(DEPTH, 1, n))


def _inproj_kernel(x_ref, mod_ref, w_ref, wuq_ref, wukv_ref, gq_ref, gkv_ref,
                   cosc_ref, sinc_ref, cosd_ref, sind_ref, ag_ref, ab_ref, aws_ref, abs_ref,
                   oa_ref, qb_ref, kb_ref, vb_ref, kb16_ref, vb16_ref,
                   qp0_ref, qp1_ref, ckvn_ref, krp_ref, kp0_ref, kp1_ref, vc_ref,
                   qd_ref, kd_ref, vd_ref, kd16_ref, vd16_ref):
    m = mod_ref[0]
    h = x_ref[...] * (1.0 + m[1:2]) + m[0:1]
    z = _dot(h.astype(BF16), w_ref[...])
    cosc, sinc = cosc_ref[...], sinc_ref[...]
    cosd, sind = cosd_ref[...], sind_ref[...]

    for c0 in range(0, z.shape[0], CHUNK):
        oa_ref[c0:c0 + CHUNK, :] = _chunk_mlp(z[c0:c0 + CHUNK, _C_A:_C_QB], ag_ref[...], ab_ref[...], aws_ref,
                                              abs_ref[...])
    qb_ref[...] = z[:, _C_QB:_C_KB].astype(BF16)
    kb = z[:, _C_KB:_C_VB]
    vb = z[:, _C_VB:_C_CQ]
    kb_ref[...] = kb
    vb_ref[...] = vb
    kb16_ref[...] = kb.astype(BF16)
    vb16_ref[...] = vb.astype(BF16)

    cqn = _rms_norm(z[:, _C_CQ:_C_CKV], gq_ref[...])
    q = _dot(cqn.astype(BF16), wuq_ref[...])
    qp0_ref[...] = (q[:, 0:256] * cosc + q[:, 512:768] * sinc).astype(BF16)
    qp1_ref[...] = (q[:, 256:512] * cosc + q[:, 768:1024] * sinc).astype(BF16)
    ckvn = _rms_norm(z[:, _C_CKV:_C_KR], gkv_ref[...])
    ckvn_ref[...] = ckvn
    kv = _dot(ckvn.astype(BF16), wukv_ref[...])
    kr = z[:, _C_KR:_C_KRS] * cosc + z[:, _C_KRS:_C_QD] * sinc
    krp_ref[...] = kr
    kp0_ref[...] = (kv[:, 0:256] + kr).astype(BF16)
    kp1_ref[...] = (kv[:, 256:512] + kr).astype(BF16)
    vc_ref[...] = kv[:, 512:768].astype(BF16)

    qd_ref[...] = (z[:, _C_QD:_C_QDS] * cosd + z[:, _C_QDS:_C_KD] * sind).astype(BF16)
    kd = z[:, _C_KD:_C_KDS] * cosd + z[:, _C_KDS:_C_VD] * sind
    vd = z[:, _C_VD:_C_END]
    kd_ref[...] = kd
    vd_ref[...] = vd
    kd16_ref[...] = kd.astype(BF16)
    vd16_ref[...] = vd.astype(BF16)


def _cond_of_tile(i, tile, n_ctx, lat_len):
    n_ctx_tiles = n_ctx // tile
    return jnp.where(i < n_ctx_tiles, 0, 1 + (i - n_ctx_tiles) // (lat_len // tile))


def _rope_block_of_tile(i, tile, n_ctx, lat_len):
    n_ctx_tiles = n_ctx // tile
    return jnp.where(i < n_ctx_tiles, 0, 1 + (i - n_ctx_tiles) % (lat_len // tile))


def _inproj_call(x, mod, wbig, wuq, wukv, gq, gkv, tabs, mixer_a, n_ctx, lat_len):
    nt = x.shape[0]
    cond = functools.partial(_cond_of_tile, tile=TM, n_ctx=n_ctx, lat_len=lat_len)
    rblk = functools.partial(_rope_block_of_tile, tile=TM, n_ctx=n_ctx, lat_len=lat_len)
    row = lambda w: pl.BlockSpec((TM, w), lambda i: (i, 0))
    const = lambda a: pl.BlockSpec(a.shape, lambda i: (0,) * a.ndim)
    tab = lambda w: pl.BlockSpec((TM, w), lambda i: (rblk(i), 0))
    outs = [(256, F32), (256, BF16), (256, F32), (256, F32), (256, BF16), (256, BF16),
            (256, BF16), (256, BF16), (128, F32), (256, F32), (256, BF16), (256, BF16), (256, BF16),
            (256, BF16), (256, F32), (256, F32), (256, BF16), (256, BF16)]
    return pl.pallas_call(
        _inproj_kernel,
        grid=(nt // TM,),
        in_specs=[row(D_MODEL),
                  pl.BlockSpec((1, 6, D_MODEL), lambda i: (cond(i), 0, 0)),
                  const(wbig), const(wuq), const(wukv), const(gq), const(gkv),
                  tab(256), tab(256), tab(256), tab(256)] + [const(a) for a in mixer_a],
        out_specs=[row(w) for w, _ in outs],
        out_shape=[jax.ShapeDtypeStruct((nt, w), dt) for w, dt in outs],
        compiler_params=_cparams(("parallel",)),
        name="inproj",
    )(x, mod, wbig, wuq, wukv, gq, gkv, *tabs, *mixer_a)


def _kvexp_kernel(c_ref, kr_ref, w_ref, k0_ref, k1_ref, v_ref):
    kv = _dot(c_ref[...].astype(BF16), w_ref[...])
    kr = kr_ref[...]
    k0_ref[...] = (kv[:, 0:256] + kr).astype(BF16)
    k1_ref[...] = (kv[:, 256:512] + kr).astype(BF16)
    v_ref[...] = kv[:, 512:768].astype(BF16)


def _kvexp_call(ckv, kr_pair, wukv):
    n = ckv.shape[0]
    whole = lambda a: pl.BlockSpec(a.shape, lambda i: (0, 0))
    return pl.pallas_call(
        _kvexp_kernel,
        grid=(1,),
        in_specs=[whole(ckv), whole(kr_pair), whole(wukv)],
        out_specs=[pl.BlockSpec((n, 256), lambda i: (0, 0))] * 3,
        out_shape=[jax.ShapeDtypeStruct((n, 256), BF16)] * 3,
        compiler_params=_cparams(("arbitrary",)),
        name="mla_cache_expand",
    )(ckv, kr_pair, wukv)


def _chunk_mlp(z, gain, bias, ws_ref, bs):
    g = jax.nn.gelu(z)
    u = g[:, 0:GROUP_W]
    v = _layer_norm(g[:, GROUP_W:2 * GROUP_W], gain, bias).astype(BF16)
    lane_head = lax.broadcasted_iota(jnp.int32, (1, GROUP_W), 1) // HEAD_DIM
    mixed = bs
    for hd in range(N_HEADS):
        mixed = mixed + jnp.where(lane_head == hd, _dot(ws_ref[hd], v), 0.0)
    return u * mixed


def _attn_kernel(*refs, tq, n_sub, **static):
    for sb in range(n_sub):
        _attn_tile(refs, pl.program_id(1) * n_sub + sb, slice(sb * tq, (sb + 1) * tq), tq=tq, **static)


def _attn_tile(refs, tile, q_rows, *, tq, wk, kc, scale, has_q2, has_extra, bias_heads, has_sink, start_fn, bias_index):
    refs = list(refs)
    sink_ref = refs.pop(0) if has_sink else None
    q1_ref = refs.pop(0)
    q2_ref = refs.pop(0) if has_q2 else None
    k1_ref = refs.pop(0)
    k2_ref = refs.pop(0) if has_q2 else None
    v_ref = refs.pop(0)
    if has_extra:
        xk1_ref = refs.pop(0)
        xk2_ref = refs.pop(0) if has_q2 else None
        xv_ref = refs.pop(0)
    bias_ref = refs.pop(0) if bias_heads else None
    o_ref = refs.pop(0)

    def stack_heads(q, width, n):
        lane_head = lax.broadcasted_iota(jnp.int32, (1, q.shape[1]), 1) // width
        return jnp.concatenate([jnp.where(lane_head == hd, q, jnp.zeros_like(q)) for hd in range(n)], axis=0)

    if has_q2:
        q1s = stack_heads(q1_ref[q_rows, :], PAIR_W, 2)
        q2s = stack_heads(q2_ref[q_rows, :], PAIR_W, 2)
    else:
        q1s = stack_heads(q1_ref[q_rows, :], HEAD_DIM, N_HEADS)

    start = start_fn(tile)
    if not isinstance(start, int):
        start = pl.multiple_of(start, 64)
    bias_tile = bias_ref.at[bias_index(tile)] if bias_heads else None

    chunks = [("win", c0, min(kc, wk - c0)) for c0 in range(0, wk, kc)]
    if has_extra:
        chunks.append(("extra", 0, 0))
    sink = None
    if has_sink:
        sink = sink_ref[:, 0:1]
    m = denom = o = None
    for kind, c0, n in chunks:
        if kind == "win":
            rows = pl.ds(start if c0 == 0 else start + c0, n)
            k1c, vc = k1_ref[rows, :], v_ref[rows, :]
            k2c = k2_ref[rows, :] if has_q2 else None
        else:
            k1c, vc = xk1_ref[...], xv_ref[...]
            k2c = xk2_ref[...] if has_q2 else None
        s = _dot_nt(q1s, k1c)
        if has_q2:
            s = jnp.concatenate([s, _dot_nt(q2s, k2c)], axis=0)
        s = s * scale
        if kind == "win" and bias_heads == N_HEADS:
            s = s + bias_tile[:, c0:c0 + n]
        elif kind == "win" and bias_heads == 1:
            s = s + jnp.concatenate([bias_tile[:, c0:c0 + n]] * N_HEADS, axis=0)
        mc = jnp.max(s, axis=-1, keepdims=True)
        if m is None:
            m_new = mc if sink is None else jnp.maximum(mc, sink)
        else:
            m_new = jnp.maximum(m, mc)
        p = jnp.exp(s - m_new)
        pv = _dot(p.astype(BF16), vc)
        if m is None:
            denom, o = jnp.sum(p, axis=-1, keepdims=True), pv
        else:
            alpha = jnp.exp(m - m_new)
            denom = alpha * denom + jnp.sum(p, axis=-1, keepdims=True)
            o = alpha * o + pv
        m = m_new
    if has_sink:
        denom = denom + jnp.exp(sink - m)
    o = o / denom
    lane_head = lax.broadcasted_iota(jnp.int32, (1, N_HEADS * HEAD_DIM), 1) // HEAD_DIM
    out = jnp.zeros((tq, N_HEADS * HEAD_DIM), F32)
    for hd in range(N_HEADS):
        out = out + jnp.where(lane_head == hd, o[hd * tq:(hd + 1) * tq], 0.0)
    o_ref[q_rows, :] = out


def _attn_call(name, grid, tq, wk, scale, q_index, q1, k1, v, k_spec_fn, *, q2=None, k2=None, extra=None,
               bias=None, bias_index=None, sinks=None, start_fn=lambda t: 0, kc=ATTN_KEY_CHUNK, n_sub=1):
    tb = n_sub * tq
    has_q2 = q2 is not None
    has_extra = extra is not None
    bias_heads = 0 if bias is None else bias.shape[1] // tq
    args, specs = [], []
    if sinks is not None:
        sink_rows = jnp.broadcast_to(jnp.repeat(sinks.astype(F32), tq)[:, None], (N_HEADS * tq, 128))
        args.append(sink_rows)
        specs.append(pl.BlockSpec(sink_rows.shape, lambda b, j: (0, 0)))
    args.append(q1)
    specs.append(pl.BlockSpec((tb, 256), lambda b, j: (q_index(b, j), 0)))
    if has_q2:
        args.append(q2)
        specs.append(pl.BlockSpec((tb, 256), lambda b, j: (q_index(b, j), 0)))
    args.append(k1)
    specs.append(k_spec_fn(256))
    if has_q2:
        args.append(k2)
        specs.append(k_spec_fn(256))
    args.append(v)
    specs.append(k_spec_fn(256))
    if has_extra:
        for a in extra:
            args.append(a)
            specs.append(pl.BlockSpec((None,) + a.shape[1:], lambda b, j: (b, 0, 0)))
    if bias is not None:
        args.append(bias)
        specs.append(pl.BlockSpec(bias.shape, lambda b, j: (0, 0, 0)))
    kern = functools.partial(_attn_kernel, tq=tq, n_sub=n_sub, wk=wk, kc=kc, scale=scale, has_q2=has_q2,
                             has_extra=has_extra, bias_heads=bias_heads, has_sink=sinks is not None,
                             start_fn=start_fn, bias_index=bias_index)
    return pl.pallas_call(
        kern,
        grid=grid,
        in_specs=specs,
        out_specs=pl.BlockSpec((tb, 256), lambda b, j: (b * grid[1] + j, 0)),
        out_shape=jax.ShapeDtypeStruct((grid[0] * grid[1] * tb, 256), F32),
        compiler_params=_cparams(("parallel", "parallel")),
        name=name,
    )(*args)


def _merge_kernel(oa_ref, obc_ref, obl_ref, occ_ref, ocl_ref, odc_ref, odl_ref, x_ref, mod_ref, g_ref, w_ref,
                  lg_ref, lb_ref, o_ref, *, n_ctx_tiles):
    m = mod_ref[0]
    is_ctx = pl.program_id(0) < n_ctx_tiles
    pick = lambda c_ref, l_ref: jnp.where(is_ctx, c_ref[...], l_ref[...])
    groups = (oa_ref[...], pick(obc_ref, obl_ref), pick(occ_ref, ocl_ref), pick(odc_ref, odl_ref))
    acc = None
    for gi, o in enumerate(groups):
        og = _rms_norm(o, g_ref[:, gi * GROUP_W:(gi + 1) * GROUP_W]).astype(BF16)
        part = _dot(og, w_ref[gi * GROUP_W:(gi + 1) * GROUP_W, :])
        acc = part if acc is None else acc + part
    y = ALPHA * x_ref[...] + m[2:3] * acc
    o_ref[...] = _layer_norm(y, lg_ref[...], lb_ref[...])


def _merge_call(oa, ob, oc, od, x, mod, gout, wout16, lg, lb, n_ctx, lat_len):
    nt = x.shape[0]
    nct = n_ctx // TM
    cond = functools.partial(_cond_of_tile, tile=TM, n_ctx=n_ctx, lat_len=lat_len)
    row = lambda w: pl.BlockSpec((TM, w), lambda i: (i, 0))
    ctx_row = lambda: pl.BlockSpec((TM, 256), lambda i: (jnp.minimum(i, nct - 1), 0))
    lat_row = lambda: pl.BlockSpec((TM, 256), lambda i: (jnp.maximum(i - nct, 0), 0))
    const = lambda a: pl.BlockSpec(a.shape, lambda i: (0,) * a.ndim)
    return pl.pallas_call(
        functools.partial(_merge_kernel, n_ctx_tiles=nct),
        grid=(nt // TM,),
        in_specs=[row(256), ctx_row(), lat_row(), ctx_row(), lat_row(), ctx_row(), lat_row(), row(D_MODEL),
                  pl.BlockSpec((1, 6, D_MODEL), lambda i: (cond(i), 0, 0)),
                  const(gout), const(wout16), const(lg), const(lb)],
        out_specs=row(D_MODEL),
        out_shape=jax.ShapeDtypeStruct((nt, D_MODEL), F32),
        compiler_params=_cparams(("parallel",)),
        name="merge_out",
    )(oa, *ob, *oc, *od, x, mod, gout, wout16, lg, lb)


_N_TOP = PEER_TOPK + 1
_GELU_C0 = math.sqrt(2.0 / math.pi)
_GELU_C1 = 0.044715 * _GELU_C0


def _sort_network(n):
    pairs = []
    p = 1
    while p < n:
        k = p
        while k >= 1:
            for j in range(k % p, n - k, 2 * k):
                for i in range(min(k, n - j - k)):
                    if (i + j) // (2 * p) == (i + j + k) // (2 * p):
                        pairs.append((i + j, i + j + k))
            k //= 2
        p *= 2
    return pairs


def _merge_top(levels, n_top):
    levels = list(levels)
    sub = lax.broadcasted_iota(jnp.int32, levels[0].shape, 0).astype(F32)
    out = []
    for k in range(n_top):
        head = levels[0]
        m = jnp.max(head, axis=0, keepdims=True)
        out.append(m)
        first = jnp.min(jnp.where(head == m, sub, 8.0), axis=0, keepdims=True)
        pop = sub == first
        for v in range(n_top - 1 - k):
            nxt = levels[v + 1] if v + 1 < len(levels) else -jnp.inf
            levels[v] = jnp.where(pop, nxt, levels[v])
    return out


def _top_values(s, n_top):
    g = [s[8 * v:8 * v + 8] for v in range(s.shape[0] // 8)]
    for i, j in _sort_network(len(g)):
        g[i], g[j] = jnp.maximum(g[i], g[j]), jnp.minimum(g[i], g[j])
    return _merge_top(g, n_top)


_CAND_LEVELS = (17, 8, 5, 4, 13, 4, 1, 0)


def _candidate_levels(sv0, sv1):
    shape = (8,) + sv0[0].shape[1:]
    row = lax.broadcasted_iota(jnp.int32, shape, 0)
    pick = lambda vals, default: functools.reduce(
        lambda acc, rv: jnp.where(row == rv[0], rv[1], acc), vals, jnp.full(shape, default, F32))
    fixed0 = pick([(r, sv0[r]) for r in range(4)], 0.0)
    fixed1 = pick([(4 + r, sv1[r]) for r in range(3)], 0.0)
    n_valid = pick([(r, float(n)) for r, n in enumerate(_CAND_LEVELS)], 0.0)
    levels = []
    for v in range(max(_CAND_LEVELS)):
        moving0 = sv0[min(4 + v, len(sv0) - 1)]
        lv = jnp.where(row < 4, fixed0 + sv1[v], moving0 + fixed1)
        levels.append(jnp.where(n_valid > float(v), lv, -jnp.inf))
    return levels


def _route_kernel(x_ref, mod_ref, wq_ref, keys_ref, h2_ref, p0_ref, te_ref, e1_ref, st_scr):
    m = mod_ref[0]
    h2 = (x_ref[...] * (1.0 + m[4:5]) + m[3:4]).astype(BF16)
    h2_ref[...] = h2
    q = _dot(h2, wq_ref[...])
    tb = q.shape[0]
    for p in range(2):
        kp = keys_ref[p]
        k_hi = kp.astype(BF16)
        k_lo = (kp - k_hi.astype(F32)).astype(BF16)
        for hd in range(PEER_HEADS):
            c0 = (hd * 2 + p) * PEER_KEYS
            qs = q[:, c0:c0 + PEER_KEYS]
            q_hi = qs.astype(BF16)
            q_lo = (qs - q_hi.astype(F32)).astype(BF16)
            st_scr[hd * 2 + p] = _dot_nt(k_hi, q_hi) + _dot_nt(k_lo, q_hi) + _dot_nt(k_hi, q_lo)

    def per_head(hd, carry):
        s0 = st_scr[hd * 2]
        s1 = st_scr[hd * 2 + 1]
        sv0 = _top_values(s0, _N_TOP)
        sv1 = _top_values(s1, _N_TOP)
        c = _merge_top(_candidate_levels(sv0, sv1), _N_TOP)
        thr = 0.5 * (c[PEER_TOPK - 1] + c[PEER_TOPK])
        z = jnp.zeros_like(thr)
        for k in range(PEER_TOPK):
            z = z + jnp.exp(c[k] - c[0])
        p0_ref[hd] = jnp.exp(s0 - sv0[0]) * (0.5 / z)
        te_ref[hd] = jnp.exp((thr - sv1[0]) - s0)
        e1_ref[hd] = jnp.exp(s1 - sv1[0])
        return carry

    lax.fori_loop(0, PEER_HEADS, per_head, 0)


def _route_call(x1, mod, wq16, keys, layer, n_ctx, lat_len):
    nt = x1.shape[0]
    tb = TB_ROUTE
    cond = functools.partial(_cond_of_tile, tile=tb, n_ctx=n_ctx, lat_len=lat_len)
    fac = lambda: pl.BlockSpec((PEER_HEADS, PEER_KEYS, tb), lambda i: (0, 0, i))
    fshape = jax.ShapeDtypeStruct((PEER_HEADS, PEER_KEYS, nt), F32)
    return pl.pallas_call(
        _route_kernel,
        grid=(nt // tb,),
        in_specs=[pl.BlockSpec((tb, D_MODEL), lambda i: (i, 0)),
                  pl.BlockSpec((1, 6, D_MODEL), lambda i: (cond(i), 0, 0)),
                  pl.BlockSpec((None,) + wq16.shape[1:], lambda i: (layer, 0, 0)),
                  pl.BlockSpec((None,) + keys.shape[1:], lambda i: (layer, 0, 0, 0))],
        out_specs=[pl.BlockSpec((tb, D_MODEL), lambda i: (i, 0)), fac(), fac(), fac()],
        out_shape=[jax.ShapeDtypeStruct((nt, D_MODEL), BF16), fshape, fshape, fshape],
        scratch_shapes=[pltpu.VMEM((2 * PEER_HEADS, PEER_KEYS, tb), F32)],
        compiler_params=_cparams(("parallel",)),
        name="peer_route",
    )(x1, mod, wq16, keys)


def _expert_kernel(h2_ref, u_ref, vt_ref, p0_ref, te_ref, e1_ref, x_ref, mod_ref, lg_ref, lb_ref,
                   o_ref, acc_scr, act_scr):
    c = pl.program_id(1)
    n_lane = h2_ref.shape[0] // PEER_KEYS

    @pl.when(c == 0)
    def _():
        acc_scr[...] = jnp.zeros_like(acc_scr)

    st = _dot_nt(u_ref[...], h2_ref[...])
    th = jnp.tanh(st * (_GELU_C0 + _GELU_C1 * (st * st)))
    act_scr[...] = st + st * th
    for pc in range(EC // EXP_PIECE):
        slabs = range(pc * EXP_PIECE // PEER_KEYS, (pc + 1) * EXP_PIECE // PEER_KEYS)
        g_lanes = []
        for ln in range(n_lane):
            lanes = slice(ln * PEER_KEYS, (ln + 1) * PEER_KEYS)
            w = [None] * len(slabs)
            for hd in range(PEER_HEADS):
                e1 = e1_ref[hd, :, lanes]
                for k, sl in enumerate(slabs):
                    te = te_ref[hd, 0, sl:sl + 1, lanes]
                    e0 = p0_ref[hd, 0, sl:sl + 1, lanes]
                    term = e0 * jnp.where(e1 > te, e1, 0.0)
                    w[k] = term if w[k] is None else w[k] + term
            g_lanes.append(jnp.concatenate(
                [(w[k] * act_scr[sl * PEER_KEYS:(sl + 1) * PEER_KEYS, lanes]).astype(BF16)
                 for k, sl in enumerate(slabs)], axis=0))
        g = jnp.concatenate(g_lanes, axis=1)
        acc_scr[...] += _dot(vt_ref[:, pc * EXP_PIECE:(pc + 1) * EXP_PIECE], g)

    @pl.when(c == pl.num_programs(1) - 1)
    def _():
        m = mod_ref[0]
        y = ALPHA * x_ref[...] + m[5:6] * acc_scr[...].T
        o_ref[...] = _layer_norm(y, lg_ref[...], lb_ref[...])


def _expert_call(h2, u16, vt16, layer, p0, te, e1, x1, mod, lg, lb, n_ctx, lat_len):
    nt = x1.shape[0]
    tb = TB_EXP
    n_chunks = N_EXPERTS // EC
    cond = functools.partial(_cond_of_tile, tile=tb, n_ctx=n_ctx, lat_len=lat_len)
    p0r = p0.reshape(PEER_HEADS, n_chunks, N_SLAB, nt)
    ter = te.reshape(PEER_HEADS, n_chunks, N_SLAB, nt)
    slab = lambda: pl.BlockSpec((PEER_HEADS, 1, N_SLAB, tb), lambda i, c: (0, c, 0, i))
    full = lambda: pl.BlockSpec((PEER_HEADS, PEER_KEYS, tb), lambda i, c: (0, 0, i))
    const = lambda a: pl.BlockSpec(a.shape, lambda i, c: (0,) * a.ndim)
    return pl.pallas_call(
        _expert_kernel,
        grid=(nt // tb, n_chunks),
        in_specs=[pl.BlockSpec((tb, D_MODEL), lambda i, c: (i, 0)),
                  pl.BlockSpec((None, EC, D_MODEL), lambda i, c: (layer, c, 0)),
                  pl.BlockSpec((None, D_MODEL, EC), lambda i, c: (layer, 0, c)),
                  slab(), slab(), full(),
                  pl.BlockSpec((tb, D_MODEL), lambda i, c: (i, 0)),
                  pl.BlockSpec((1, 6, D_MODEL), lambda i, c: (cond(i), 0, 0)),
                  const(lg), const(lb)],
        out_specs=pl.BlockSpec((tb, D_MODEL), lambda i, c: (i, 0)),
        out_shape=jax.ShapeDtypeStruct((nt, D_MODEL), F32),
        scratch_shapes=[pltpu.VMEM((D_MODEL, tb), F32), pltpu.VMEM((EC, tb), F32)],
        compiler_params=_cparams(("parallel", "arbitrary")),
        name="peer_experts",
    )(h2, u16, vt16, p0r, ter, e1, x1, mod, lg, lb)


def _swap_halves(w, n_heads, rot):
    k = w.shape[0]
    w4 = w.reshape(k, n_heads, 2, rot // 2)
    return jnp.concatenate([w4[:, :, 1:2], w4[:, :, 0:1]], axis=2).reshape(k, n_heads * rot)


def _inproj_weight(w):
    a = w[:, 0:512]
    qb, kb, vb = w[:, 512:768], w[:, 768:1024], w[:, 1024:1280]
    cq, ckv, kr = w[:, 1280:1536], w[:, 1536:1664], w[:, 1664:1696]
    qd, kd, vd = w[:, 1696:1952], w[:, 1952:2080], w[:, 2080:2208]
    rep = lambda m: jnp.repeat(m.reshape(D_MODEL, 2, HEAD_DIM), 2, axis=1).reshape(D_MODEL, 256)
    kdr = rep(kd)
    cols = [a, qb, kb, vb, cq, ckv, _pair_cols(None, [kr, kr]), _pair_cols(None, [_swap_halves(kr, 1, MLA_ROPE)] * 2),
            qd, _swap_halves(qd, N_HEADS, HEAD_DIM), kdr, _swap_halves(kdr, N_HEADS, HEAD_DIM), rep(vd)]
    return jnp.concatenate(cols, axis=1).astype(BF16)


def _pair_cols(nope, rope):
    k = rope[0].shape[0]
    zeros = lambda w: jnp.zeros((k, w), rope[0].dtype)
    nope = nope if nope is not None else [zeros(MLA_NOPE)] * 2
    rope = rope if rope is not None else [zeros(MLA_ROPE)] * 2
    return jnp.concatenate([nope[0], rope[0], nope[1], rope[1], zeros(256 - 2 * PAIR_W)], axis=1)


def _mla_weights(w_uq, w_ukv):
    q3 = w_uq.reshape(MLA_Q_RANK, N_HEADS, PAIR_W)
    nope = [q3[:, hd, :MLA_NOPE] for hd in range(N_HEADS)]
    rope = [q3[:, hd, MLA_NOPE:] for hd in range(N_HEADS)]
    rope_sw = [_swap_halves(r, 1, MLA_ROPE) for r in rope]
    none64 = [jnp.zeros_like(nope[0])] * 2
    wuq = jnp.concatenate([_pair_cols(nope[0:2], rope[0:2]), _pair_cols(nope[2:4], rope[2:4]),
                           _pair_cols(none64, rope_sw[0:2]), _pair_cols(none64, rope_sw[2:4])], axis=1).astype(BF16)
    kv3 = w_ukv.reshape(MLA_KV_RANK, N_HEADS, MLA_NOPE + 64)
    knope = [kv3[:, hd, :MLA_NOPE] for hd in range(N_HEADS)]
    none32 = [jnp.zeros((MLA_KV_RANK, MLA_ROPE), w_ukv.dtype)] * 2
    wukv = jnp.concatenate([_pair_cols(knope[0:2], none32), _pair_cols(knope[2:4], none32),
                            kv3[:, :, MLA_NOPE:].reshape(MLA_KV_RANK, 256)], axis=1).astype(BF16)
    return wuq, wukv


def _pair_rope_tables(lat_len, tile):
    cos_t, sin_t = _rope_tables(lat_len, MLA_ROPE, tile)
    cos32, sin32 = cos_t[:, :MLA_ROPE], sin_t[:, :MLA_ROPE]
    n = cos_t.shape[0]
    one, zero = jnp.ones((n, MLA_NOPE), F32), jnp.zeros((n, MLA_NOPE), F32)
    return (jnp.concatenate([one, cos32, one, cos32, one], axis=1),
            jnp.concatenate([zero, sin32, zero, sin32, zero], axis=1))


def _rope_tables(lat_len, rot, tile):
    t = jnp.arange(lat_len)
    row = (t // GRID_W).astype(F32)
    col = (t % GRID_W).astype(F32)
    nf = rot // 4
    freqs = ROPE_BASE ** (-jnp.arange(nf, dtype=F32) / nf)
    ang = jnp.concatenate([row[:, None] * freqs, col[:, None] * freqs], -1)
    cos, sin = jnp.cos(ang), jnp.sin(ang)
    cos_t = jnp.tile(jnp.concatenate([cos, cos], -1), (1, N_HEADS))
    sin_t = jnp.tile(jnp.concatenate([-sin, sin], -1), (1, N_HEADS))
    w = N_HEADS * rot
    return (jnp.concatenate([jnp.ones((tile, w), F32), cos_t], 0),
            jnp.concatenate([jnp.zeros((tile, w), F32), sin_t], 0))


def _na_bias_tables(rpb, rows):
    kh = min(NA_KH, rows)
    qrow = np.array([0, 1, 2, 3, rows // 2, rows - 3, rows - 2, rows - 1])
    start = np.clip(qrow - kh // 2, 0, rows - kh)
    dr0 = start - qrow + NA_KH - 1
    qc = np.arange(GRID_W)
    kc = np.arange(GRID_W)
    cstart = np.clip(qc - NA_KW // 2, 0, GRID_W - NA_KW)
    ok = (kc[None, :] >= cstart[:, None]) & (kc[None, :] < cstart[:, None] + NA_KW)
    n_h, n_a, n_c = rpb.shape
    edge = GRID_W - NA_KW
    w = jnp.concatenate([jnp.broadcast_to(rpb[:, :, :1], (n_h, n_a, edge)), rpb,
                         jnp.broadcast_to(rpb[:, :, -1:], (n_h, n_a, edge + 1))], axis=-1)
    skew = jnp.tile(w, (1, 1, GRID_W))[:, :, :GRID_W * (2 * GRID_W - 1)].reshape(n_h, n_a, GRID_W, 2 * GRID_W - 1)
    toep = skew[:, :, :, GRID_W - 1:]
    tabs = []
    for d0 in dr0:
        b = jnp.where(ok[None, None], toep[:, d0:d0 + kh], NEG_INF)
        tabs.append(b.transpose(0, 2, 1, 3).reshape(N_HEADS * GRID_W, kh * GRID_W))
    return jnp.stack(tabs)


def _swa_bias_tables(lat_len):
    nb = lat_len // SWA_WIN
    qi = np.arange(SWA_WIN)
    kj = np.arange(3 * SWA_WIN)
    tabs = []
    for n in (0, 1, nb - 1):
        kpos = _swa_window_start(n, nb) + kj
        in_win = np.abs(kpos[None, :] - (n * SWA_WIN + qi[:, None])) <= SWA_WIN
        tabs.append(np.where(in_win, 0.0, NEG_INF))
    return jnp.asarray(np.stack(tabs), F32)


def _swa_window_start(n, nb):
    lo = n - 1
    lo = jnp.clip(lo, 0, nb - 3) if isinstance(n, jax.Array) else min(max(lo, 0), nb - 3)
    return lo * SWA_WIN


def _heads_to_lanes(t):
    b, h, s, d = t.shape
    return t.transpose(0, 2, 1, 3).reshape(b, s, h * d)


def _lanes_to_heads(t, b, s, h):
    return t.reshape(b, s, h, -1).transpose(0, 2, 1, 3)


def kernel(x_prompt, x_sample, cache_nat_k, cache_nat_v, cache_mla_ckv, cache_mla_krope, cache_swa_k, cache_swa_v, c, c_ctx, w_in, w_out, out_norm_g, w_ada, b_ada, ln1_g, ln1_b, ln2_g, ln2_b, a_norm_g, a_norm_b, a_w_s, a_b_s, nat_rpb, mla_q_norm_g, mla_w_uq, mla_kv_norm_g, mla_w_ukv, swa_sinks, peer_w_q, peer_sub_keys, peer_u, peer_v):
    n_b, seq, d = x_prompt.shape
    n_db, lat_len, _ = x_sample.shape
    past = cache_nat_k.shape[3]
    n_ctx = n_b * seq
    nt = n_ctx + n_db * lat_len
    rows = lat_len // GRID_W

    x = jnp.concatenate([x_prompt.reshape(n_ctx, d), x_sample.reshape(n_db * lat_len, d)], axis=0)
    conds = jnp.zeros((8, d), F32).at[0].set(c_ctx).at[1:1 + n_db].set(c)
    mods = _ada_call(conds, w_ada, b_ada).reshape(DEPTH, 8, 6, d)

    cos_c, sin_c = _pair_rope_tables(lat_len, TM)
    cos_d, sin_d = _rope_tables(lat_len, HEAD_DIM, TM)
    swa_bias = _swa_bias_tables(lat_len)
    nb_swa = lat_len // SWA_WIN

    q_ctx = lambda b, j: b
    lat_q = lambda tq: (lambda b, j: (n_ctx + b * lat_len) // tq + j)
    ctx_keys = lambda w: pl.BlockSpec((seq, w), lambda b, j: (b, 0))
    lat_keys = lambda w: pl.BlockSpec((lat_len, w), lambda b, j: (n_ctx // lat_len + b, 0))

    wq16 = peer_w_q.astype(BF16)
    u16 = peer_u.astype(BF16)
    vt16 = jnp.swapaxes(peer_v, 1, 2).astype(BF16)

    states = [[] for _ in range(6)]
    for l in range(DEPTH):
        mod = mods[l]
        wbig = _inproj_weight(w_in[l])
        wuq, wukv = _mla_weights(mla_w_uq[l], mla_w_ukv[l])
        mixer_a = (a_norm_g[l][None], a_norm_b[l][None], a_w_s[l].astype(BF16),
                   jnp.repeat(a_b_s[l].T, HEAD_DIM, axis=1))
        (oa, qb, kb, vb, kb16, vb16, qp0, qp1, ckvn, krp, kp0, kp1, vc,
         qd, kd, vd, kd16, vd16) = _inproj_call(
            x, mod, wbig, wuq, wukv, mla_q_norm_g[l][None], mla_kv_norm_g[l][None],
            (cos_c, sin_c, cos_d, sin_d), mixer_a, n_ctx, lat_len)

        ob_c = _attn_call("ctx_attn_b", (n_b, 1), seq, seq, HEAD_DIM ** -0.5, q_ctx, qb, kb16, vb16, ctx_keys)
        oc_c = _attn_call("ctx_attn_c", (n_b, 1), seq, seq, MLA_SCALE, q_ctx, qp0, kp0, vc, ctx_keys,
                          q2=qp1, k2=kp1)
        od_c = _attn_call("ctx_attn_d", (n_b, 1), seq, seq, HEAD_DIM ** -0.5, q_ctx, qd, kd16, vd16, ctx_keys,
                          sinks=swa_sinks[l])

        na_bias = _na_bias_tables(nat_rpb[l], rows)
        kh = min(NA_KH, rows)
        ob_l = _attn_call(
            "lat_attn_b", (n_db, rows // NA_ROWS_PER_STEP), GRID_W, kh * GRID_W, HEAD_DIM ** -0.5,
            lat_q(NA_ROWS_PER_STEP * GRID_W), qb, kb16, vb16, lat_keys, n_sub=NA_ROWS_PER_STEP,
            extra=(_heads_to_lanes(cache_nat_k[:, l]).astype(BF16), _heads_to_lanes(cache_nat_v[:, l]).astype(BF16)),
            bias=na_bias,
            bias_index=lambda r: jnp.where(r < 4, r, jnp.where(r > rows - 4, r - (rows - 8), 4)),
            start_fn=lambda r: jnp.clip(r - kh // 2, 0, rows - kh) * GRID_W)
        krx = cache_mla_krope[:, l].reshape(n_db * past, MLA_ROPE)
        kx0, kx1, vx = _kvexp_call(cache_mla_ckv[:, l].reshape(n_db * past, MLA_KV_RANK),
                                   _pair_cols(None, [krx, krx]), wukv)
        oc_l = _attn_call(
            "lat_attn_c", (n_db, lat_len // 256), 256, lat_len, MLA_SCALE, lat_q(256), qp0, kp0, vc, lat_keys,
            q2=qp1, k2=kp1,
            extra=tuple(t.reshape(n_db, past, 256) for t in (kx0, kx1, vx)))
        rep_kv = lambda t: jnp.repeat(t, 2, axis=1)
        od_l = _attn_call(
            "lat_attn_d", (n_db, nb_swa // SWA_BLOCKS_PER_STEP), SWA_WIN, 3 * SWA_WIN, HEAD_DIM ** -0.5,
            lat_q(SWA_BLOCKS_PER_STEP * SWA_WIN), qd, kd16, vd16, lat_keys, n_sub=SWA_BLOCKS_PER_STEP,
            extra=(_heads_to_lanes(rep_kv(cache_swa_k[:, l])).astype(BF16),
                   _heads_to_lanes(rep_kv(cache_swa_v[:, l])).astype(BF16)),
            bias=swa_bias,
            bias_index=lambda n: jnp.where(n == 0, 0, jnp.where(n == nb_swa - 1, 2, 1)),
            sinks=swa_sinks[l],
            start_fn=lambda n: _swa_window_start(n, nb_swa))
        ob, oc, od = (ob_c, ob_l), (oc_c, oc_l), (od_c, od_l)

        x1 = _merge_call(oa, ob, oc, od, x, mod, out_norm_g[l][None], w_out[l].astype(BF16),
                         ln1_g[l][None], ln1_b[l][None], n_ctx, lat_len)
        h2, p0, te, e1 = _route_call(x1, mod, wq16, peer_sub_keys, l, n_ctx, lat_len)
        x = _expert_call(h2, u16, vt16, l, p0, te, e1, x1, mod,
                         ln2_g[l][None], ln2_b[l][None], n_ctx, lat_len)

        states[0].append(_lanes_to_heads(kb[:n_ctx], n_b, seq, N_HEADS))
        states[1].append(_lanes_to_heads(vb[:n_ctx], n_b, seq, N_HEADS))
        states[2].append(ckvn[:n_ctx].reshape(n_b, seq, MLA_KV_RANK))
        states[3].append(krp[:n_ctx, MLA_NOPE:PAIR_W].reshape(n_b, seq, MLA_ROPE))
        states[4].append(_lanes_to_heads(kd[:n_ctx], n_b, seq, N_HEADS)[:, ::2])
        states[5].append(_lanes_to_heads(vd[:n_ctx], n_b, seq, N_HEADS)[:, ::2])

    y_prompt = x[:n_ctx].reshape(n_b, seq, d)
    y_sample = x[n_ctx:].reshape(n_db, lat_len, d)
    return (y_prompt, y_sample) + tuple(jnp.stack(s, axis=1) for s in states)
```

```python
import functools
import math

import jax
import jax.numpy as jnp
import numpy as np
from jax import lax
from jax.experimental import pallas as pl
from jax.experimental.pallas import tpu as pltpu

F32 = jnp.float32
BF16 = jnp.bfloat16

D_MODEL = 1024
DEPTH = 2
GRID_W = 64
HEAD_DIM = 64
N_HEADS = 4
GROUP_W = 256
CHUNK = 128
NA_KH = 8
NA_KW = 16
MLA_Q_RANK = 256
MLA_KV_RANK = 128
MLA_NOPE = 64
MLA_ROPE = 32
MLA_SCALE = (MLA_NOPE + MLA_ROPE) ** -0.5
SWA_WIN = 128
PEER_HEADS = 8
PEER_KEYS = 128
PEER_TOPK = 16
N_EXPERTS = PEER_KEYS * PEER_KEYS
ROPE_BASE = 10000.0
LN_EPS = 1e-5
NEG_INF = -1e30
ALPHA = (2 * DEPTH) ** 0.25

V7X_VMEM_LIMIT_BYTES = 56 * 1024 * 1024
TM = 256
TB_ROUTE = 256
TB_EXP = 512
ATTN_KEY_CHUNK = 1024
SINK_PAD = 128
NA_ROWS_PER_STEP = 8
SWA_BLOCKS_PER_STEP = 4
EC = 2048
N_SLAB = EC // PEER_KEYS
EXP_PIECE = 256
_C_A = 0
_C_QB, _C_KB, _C_VB = 512, 768, 1024
_C_CQ, _C_CKV, _C_KR, _C_KRS = 1280, 1536, 1664, 1920
_C_QD, _C_QDS, _C_KD, _C_KDS, _C_VD = 2176, 2432, 2688, 2944, 3200
_C_END = 3456
PAIR_W = MLA_NOPE + MLA_ROPE


def _cparams(sem):
    return pltpu.CompilerParams(dimension_semantics=sem, vmem_limit_bytes=V7X_VMEM_LIMIT_BYTES)


def _dot(a, b):
    return jnp.dot(a, b, preferred_element_type=F32)


def _dot_nt(a, b):
    return lax.dot_general(a, b, (((1,), (1,)), ((), ())), preferred_element_type=F32)


def _layer_norm(x, g, b):
    mu = jnp.mean(x, axis=-1, keepdims=True)
    xc = x - mu
    var = jnp.mean(xc * xc, axis=-1, keepdims=True)
    return xc * lax.rsqrt(var + LN_EPS) * g + b


def _rms_norm(x, g):
    return x * lax.rsqrt(jnp.mean(x * x, axis=-1, keepdims=True) + LN_EPS) * g


def _ada_kernel(c_ref, w_ref, b_ref, o_ref):
    c = c_ref[...]
    a = c * jax.nn.sigmoid(c)
    a_hi = a.astype(BF16)
    a_lo = (a - a_hi.astype(F32)).astype(BF16)
    w = w_ref[0]
    w_hi = w.astype(BF16)
    w_lo = (w - w_hi.astype(F32)).astype(BF16)
    o_ref[0] = _dot(a_hi, w_hi) + _dot(a_hi, w_lo) + _dot(a_lo, w_hi) + b_ref[0]


def _ada_call(conds, w_ada, b_ada):
    tn = 1536
    n = w_ada.shape[-1]
    return pl.pallas_call(
        _ada_kernel,
        grid=(DEPTH, n // tn),
        in_specs=[pl.BlockSpec((8, D_MODEL), lambda l, j: (0, 0)),
                  pl.BlockSpec((1, D_MODEL, tn), lambda l, j: (l, 0, j)),
                  pl.BlockSpec((1, 1, tn), lambda l, j: (l, 0, j))],
        out_specs=pl.BlockSpec((1, 8, tn), lambda l, j: (l, 0, j)),
        out_shape=jax.ShapeDtypeStruct((DEPTH, 8, n), F32),
        compiler_params=_cparams(("parallel", "parallel")),
        name="ada_mod",
    )(conds, w_ada, b_ada.reshape(DEPTH, 1, n))


def _inproj_kernel(x_ref, mod_ref, w_ref, wuq_ref, wukv_ref, gq_ref, gkv_ref,
                   cosc_ref, sinc_ref, cosd_ref, sind_ref, ag_ref, ab_ref, aws_ref, abs_ref,
                   oa_ref, qb_ref, kb_ref, vb_ref, kb16_ref, vb16_ref,
                   qp0_ref, qp1_ref, ckvn_ref, krp_ref, kp0_ref, kp1_ref, vc_ref,
                   qd_ref, kd_ref, vd_ref, kd16_ref, vd16_ref):
    m = mod_ref[0]
    h = x_ref[...] * (1.0 + m[1:2]) + m[0:1]
    z = _dot(h.astype(BF16), w_ref[...])
    cosc, sinc = cosc_ref[...], sinc_ref[...]
    cosd, sind = cosd_ref[...], sind_ref[...]

    for c0 in range(0, z.shape[0], CHUNK):
        oa_ref[c0:c0 + CHUNK, :] = _chunk_mlp(z[c0:c0 + CHUNK, _C_A:_C_QB], ag_ref[...], ab_ref[...], aws_ref,
                                              abs_ref[...])
    qb_ref[...] = z[:, _C_QB:_C_KB].astype(BF16)
    kb = z[:, _C_KB:_C_VB]
    vb = z[:, _C_VB:_C_CQ]
    kb_ref[...] = kb
    vb_ref[...] = vb
    kb16_ref[...] = kb.astype(BF16)
    vb16_ref[...] = vb.astype(BF16)

    cqn = _rms_norm(z[:, _C_CQ:_C_CKV], gq_ref[...])
    q = _dot(cqn.astype(BF16), wuq_ref[...])
    qp0_ref[...] = (q[:, 0:256] * cosc + q[:, 512:768] * sinc).astype(BF16)
    qp1_ref[...] = (q[:, 256:512] * cosc + q[:, 768:1024] * sinc).astype(BF16)
    ckvn = _rms_norm(z[:, _C_CKV:_C_KR], gkv_ref[...])
    ckvn_ref[...] = ckvn
    kv = _dot(ckvn.astype(BF16), wukv_ref[...])
    kr = z[:, _C_KR:_C_KRS] * cosc + z[:, _C_KRS:_C_QD] * sinc
    krp_ref[...] = kr
    kp0_ref[...] = (kv[:, 0:256] + kr).astype(BF16)
    kp1_ref[...] = (kv[:, 256:512] + kr).astype(BF16)
    vc_ref[...] = kv[:, 512:768].astype(BF16)

    qd_ref[...] = (z[:, _C_QD:_C_QDS] * cosd + z[:, _C_QDS:_C_KD] * sind).astype(BF16)
    kd = z[:, _C_KD:_C_KDS] * cosd + z[:, _C_KDS:_C_VD] * sind
    vd = z[:, _C_VD:_C_END]
    kd_ref[...] = kd
    vd_ref[...] = vd
    kd16_ref[...] = kd.astype(BF16)
    vd16_ref[...] = vd.astype(BF16)


def _cond_of_tile(i, tile, n_ctx, lat_len):
    n_ctx_tiles = n_ctx // tile
    return jnp.where(i < n_ctx_tiles, 0, 1 + (i - n_ctx_tiles) // (lat_len // tile))


def _rope_block_of_tile(i, tile, n_ctx, lat_len):
    n_ctx_tiles = n_ctx // tile
    return jnp.where(i < n_ctx_tiles, 0, 1 + (i - n_ctx_tiles) % (lat_len // tile))


def _inproj_call(x, mod, wbig, wuq, wukv, gq, gkv, tabs, mixer_a, n_ctx, lat_len):
    nt = x.shape[0]
    cond = functools.partial(_cond_of_tile, tile=TM, n_ctx=n_ctx, lat_len=lat_len)
    rblk = functools.partial(_rope_block_of_tile, tile=TM, n_ctx=n_ctx, lat_len=lat_len)
    row = lambda w: pl.BlockSpec((TM, w), lambda i: (i, 0))
    const = lambda a: pl.BlockSpec(a.shape, lambda i: (0,) * a.ndim)
    tab = lambda w: pl.BlockSpec((TM, w), lambda i: (rblk(i), 0))
    outs = [(256, F32), (256, BF16), (256, F32), (256, F32), (256, BF16), (256, BF16),
            (256, BF16), (256, BF16), (128, F32), (256, F32), (256, BF16), (256, BF16), (256, BF16),
            (256, BF16), (256, F32), (256, F32), (256, BF16), (256, BF16)]
    return pl.pallas_call(
        _inproj_kernel,
        grid=(nt // TM,),
        in_specs=[row(D_MODEL),
                  pl.BlockSpec((1, 6, D_MODEL), lambda i: (cond(i), 0, 0)),
                  const(wbig), const(wuq), const(wukv), const(gq), const(gkv),
                  tab(256), tab(256), tab(256), tab(256)] + [const(a) for a in mixer_a],
        out_specs=[row(w) for w, _ in outs],
        out_shape=[jax.ShapeDtypeStruct((nt, w), dt) for w, dt in outs],
        compiler_params=_cparams(("parallel",)),
        name="inproj",
    )(x, mod, wbig, wuq, wukv, gq, gkv, *tabs, *mixer_a)


def _kvexp_kernel(c_ref, kr_ref, w_ref, k0_ref, k1_ref, v_ref):
    kv = _dot(c_ref[...].astype(BF16), w_ref[...])
    kr = kr_ref[...]
    k0_ref[...] = (kv[:, 0:256] + kr).astype(BF16)
    k1_ref[...] = (kv[:, 256:512] + kr).astype(BF16)
    v_ref[...] = kv[:, 512:768].astype(BF16)


def _kvexp_call(ckv, kr_pair, wukv):
    n = ckv.shape[0]
    whole = lambda a: pl.BlockSpec(a.shape, lambda i: (0, 0))
    return pl.pallas_call(
        _kvexp_kernel,
        grid=(1,),
        in_specs=[whole(ckv), whole(kr_pair), whole(wukv)],
        out_specs=[pl.BlockSpec((n, 256), lambda i: (0, 0))] * 3,
        out_shape=[jax.ShapeDtypeStruct((n, 256), BF16)] * 3,
        compiler_params=_cparams(("arbitrary",)),
        name="mla_cache_expand",
    )(ckv, kr_pair, wukv)


def _chunk_mlp(z, gain, bias, ws_ref, bs):
    g = jax.nn.gelu(z)
    u = g[:, 0:GROUP_W]
    v = _layer_norm(g[:, GROUP_W:2 * GROUP_W], gain, bias).astype(BF16)
    lane_head = lax.broadcasted_iota(jnp.int32, (1, GROUP_W), 1) // HEAD_DIM
    mixed = bs
    for hd in range(N_HEADS):
        mixed = mixed + jnp.where(lane_head == hd, _dot(ws_ref[hd], v), 0.0)
    return u * mixed


def _attn_kernel(*refs, tq, n_sub, **static):
    for sb in range(n_sub):
        _attn_tile(refs, pl.program_id(1) * n_sub + sb, slice(sb * tq, (sb + 1) * tq), tq=tq, **static)


def _attn_tile(refs, tile, q_rows, *, tq, wk, kc, scale, has_q2, has_extra, bias_heads, has_sink, start_fn, bias_index):
    refs = list(refs)
    xbias_ref = refs.pop(0) if has_sink else None
    q1_ref = refs.pop(0)
    q2_ref = refs.pop(0) if has_q2 else None
    k1_ref = refs.pop(0)
    k2_ref = refs.pop(0) if has_q2 else None
    v_ref = refs.pop(0)
    if has_extra:
        xk1_ref = refs.pop(0)
        xk2_ref = refs.pop(0) if has_q2 else None
        xv_ref = refs.pop(0)
    bias_ref = refs.pop(0) if bias_heads else None
    o_ref = refs.pop(0)

    def stack_heads(q, width, n):
        lane_head = lax.broadcasted_iota(jnp.int32, (1, q.shape[1]), 1) // width
        return jnp.concatenate([jnp.where(lane_head == hd, q, jnp.zeros_like(q)) for hd in range(n)], axis=0)

    if has_q2:
        q1s = stack_heads(q1_ref[q_rows, :], PAIR_W, 2)
        q2s = stack_heads(q2_ref[q_rows, :], PAIR_W, 2)
    else:
        q1s = stack_heads(q1_ref[q_rows, :], HEAD_DIM, N_HEADS)

    start = start_fn(tile)
    if not isinstance(start, int):
        start = pl.multiple_of(start, 64)
    bias_tile = bias_ref.at[bias_index(tile)] if bias_heads else None

    chunks = [("win", c0, min(kc, wk - c0)) for c0 in range(0, wk, kc)]
    if has_extra:
        chunks.append(("extra", 0, 0))
    m = denom = o = None
    for kind, c0, n in chunks:
        if kind == "win":
            rows = pl.ds(start if c0 == 0 else start + c0, n)
            k1c, vc = k1_ref[rows, :], v_ref[rows, :]
            k2c = k2_ref[rows, :] if has_q2 else None
        else:
            k1c, vc = xk1_ref[...], xv_ref[...]
            k2c = xk2_ref[...] if has_q2 else None
        s = _dot_nt(q1s, k1c)
        if has_q2:
            s = jnp.concatenate([s, _dot_nt(q2s, k2c)], axis=0)
        s = s * scale
        if kind == "win" and bias_heads == N_HEADS:
            s = s + bias_tile[:, c0:c0 + n]
        elif kind == "win" and bias_heads == 1:
            s = s + jnp.concatenate([bias_tile[:, c0:c0 + n]] * N_HEADS, axis=0)
        elif kind == "extra" and has_sink:
            s = s + xbias_ref[...]
        mc = jnp.max(s, axis=-1, keepdims=True)
        m_new = mc if m is None else jnp.maximum(m, mc)
        p = jnp.exp(s - m_new)
        pv = _dot(p.astype(BF16), vc)
        if m is None:
            denom, o = jnp.sum(p, axis=-1, keepdims=True), pv
        else:
            alpha = jnp.exp(m - m_new)
            denom = alpha * denom + jnp.sum(p, axis=-1, keepdims=True)
            o = alpha * o + pv
        m = m_new
    o = o / denom
    lane_head = lax.broadcasted_iota(jnp.int32, (1, N_HEADS * HEAD_DIM), 1) // HEAD_DIM
    out = jnp.zeros((tq, N_HEADS * HEAD_DIM), F32)
    for hd in range(N_HEADS):
        out = out + jnp.where(lane_head == hd, o[hd * tq:(hd + 1) * tq], 0.0)
    o_ref[q_rows, :] = out


def _attn_call(name, grid, tq, wk, scale, q_index, q1, k1, v, k_spec_fn, *, q2=None, k2=None, extra=None,
               bias=None, bias_index=None, sinks=None, start_fn=lambda t: 0, kc=ATTN_KEY_CHUNK, n_sub=1):
    tb = n_sub * tq
    has_q2 = q2 is not None
    bias_heads = 0 if bias is None else bias.shape[1] // tq
    args, specs = [], []
    extra_index = lambda b, j: (b, 0, 0)
    if sinks is not None:
        assert not has_q2
        if extra is None:
            extra = tuple(jnp.zeros((1, 0, 256), BF16) for _ in range(2))
            extra_index = lambda b, j: (0, 0, 0)
        n_real = extra[0].shape[1]
        extra = tuple(jnp.pad(a, ((0, 0), (0, SINK_PAD), (0, 0))) for a in extra)
        col = jnp.arange(n_real + SINK_PAD)[None, :]
        sink_rows = jnp.repeat(sinks.astype(F32), tq)[:, None]
        xbias = jnp.where(col < n_real, 0.0, jnp.where(col == n_real, sink_rows, NEG_INF))
        args.append(xbias)
        specs.append(pl.BlockSpec(xbias.shape, lambda b, j: (0, 0)))
    has_extra = extra is not None
    args.append(q1)
    specs.append(pl.BlockSpec((tb, 256), lambda b, j: (q_index(b, j), 0)))
    if has_q2:
        args.append(q2)
        specs.append(pl.BlockSpec((tb, 256), lambda b, j: (q_index(b, j), 0)))
    args.append(k1)
    specs.append(k_spec_fn(256))
    if has_q2:
        args.append(k2)
        specs.append(k_spec_fn(256))
    args.append(v)
    specs.append(k_spec_fn(256))
    if has_extra:
        for a in extra:
            args.append(a)
            specs.append(pl.BlockSpec((None,) + a.shape[1:], extra_index))
    if bias is not None:
        args.append(bias)
        specs.append(pl.BlockSpec(bias.shape, lambda b, j: (0, 0, 0)))
    kern = functools.partial(_attn_kernel, tq=tq, n_sub=n_sub, wk=wk, kc=kc, scale=scale, has_q2=has_q2,
                             has_extra=has_extra, bias_heads=bias_heads, has_sink=sinks is not None,
                             start_fn=start_fn, bias_index=bias_index)
    return pl.pallas_call(
        kern,
        grid=grid,
        in_specs=specs,
        out_specs=pl.BlockSpec((tb, 256), lambda b, j: (b * grid[1] + j, 0)),
        out_shape=jax.ShapeDtypeStruct((grid[0] * grid[1] * tb, 256), F32),
        compiler_params=_cparams(("parallel", "parallel")),
        name=name,
    )(*args)


def _merge_kernel(oa_ref, obc_ref, obl_ref, occ_ref, ocl_ref, odc_ref, odl_ref, x_ref, mod_ref, g_ref, w_ref,
                  lg_ref, lb_ref, o_ref, *, n_ctx_tiles):
    m = mod_ref[0]
    is_ctx = pl.program_id(0) < n_ctx_tiles
    pick = lambda c_ref, l_ref: jnp.where(is_ctx, c_ref[...], l_ref[...])
    groups = (oa_ref[...], pick(obc_ref, obl_ref), pick(occ_ref, ocl_ref), pick(odc_ref, odl_ref))
    acc = None
    for gi, o in enumerate(groups):
        og = _rms_norm(o, g_ref[:, gi * GROUP_W:(gi + 1) * GROUP_W]).astype(BF16)
        part = _dot(og, w_ref[gi * GROUP_W:(gi + 1) * GROUP_W, :])
        acc = part if acc is None else acc + part
    y = ALPHA * x_ref[...] + m[2:3] * acc
    o_ref[...] = _layer_norm(y, lg_ref[...], lb_ref[...])


def _merge_call(oa, ob, oc, od, x, mod, gout, wout16, lg, lb, n_ctx, lat_len):
    nt = x.shape[0]
    nct = n_ctx // TM
    cond = functools.partial(_cond_of_tile, tile=TM, n_ctx=n_ctx, lat_len=lat_len)
    row = lambda w: pl.BlockSpec((TM, w), lambda i: (i, 0))
    ctx_row = lambda: pl.BlockSpec((TM, 256), lambda i: (jnp.minimum(i, nct - 1), 0))
    lat_row = lambda: pl.BlockSpec((TM, 256), lambda i: (jnp.maximum(i - nct, 0), 0))
    const = lambda a: pl.BlockSpec(a.shape, lambda i: (0,) * a.ndim)
    return pl.pallas_call(
        functools.partial(_merge_kernel, n_ctx_tiles=nct),
        grid=(nt // TM,),
        in_specs=[row(256), ctx_row(), lat_row(), ctx_row(), lat_row(), ctx_row(), lat_row(), row(D_MODEL),
                  pl.BlockSpec((1, 6, D_MODEL), lambda i: (cond(i), 0, 0)),
                  const(gout), const(wout16), const(lg), const(lb)],
        out_specs=row(D_MODEL),
        out_shape=jax.ShapeDtypeStruct((nt, D_MODEL), F32),
        compiler_params=_cparams(("parallel",)),
        name="merge_out",
    )(oa, *ob, *oc, *od, x, mod, gout, wout16, lg, lb)


_N_TOP = PEER_TOPK + 1
_GELU_C0 = math.sqrt(2.0 / math.pi)
_GELU_C1 = 0.044715 * _GELU_C0


def _sort_network(n):
    pairs = []
    p = 1
    while p < n:
        k = p
        while k >= 1:
            for j in range(k % p, n - k, 2 * k):
                for i in range(min(k, n - j - k)):
                    if (i + j) // (2 * p) == (i + j + k) // (2 * p):
                        pairs.append((i + j, i + j + k))
            k //= 2
        p *= 2
    return pairs


def _merge_top(levels, n_top):
    levels = list(levels)
    sub = lax.broadcasted_iota(jnp.int32, levels[0].shape, 0).astype(F32)
    out = []
    for k in range(n_top):
        head = levels[0]
        m = jnp.max(head, axis=0, keepdims=True)
        out.append(m)
        first = jnp.min(jnp.where(head == m, sub, 8.0), axis=0, keepdims=True)
        pop = sub == first
        for v in range(n_top - 1 - k):
            nxt = levels[v + 1] if v + 1 < len(levels) else -jnp.inf
            levels[v] = jnp.where(pop, nxt, levels[v])
    return out


def _top_values(s, n_top):
    g = [s[8 * v:8 * v + 8] for v in range(s.shape[0] // 8)]
    for i, j in _sort_network(len(g)):
        g[i], g[j] = jnp.maximum(g[i], g[j]), jnp.minimum(g[i], g[j])
    return _merge_top(g, n_top)


_CAND_LEVELS = (17, 8, 5, 4, 13, 4, 1, 0)


def _candidate_levels(sv0, sv1):
    shape = (8,) + sv0[0].shape[1:]
    row = lax.broadcasted_iota(jnp.int32, shape, 0)
    pick = lambda vals, default: functools.reduce(
        lambda acc, rv: jnp.where(row == rv[0], rv[1], acc), vals, jnp.full(shape, default, F32))
    fixed0 = pick([(r, sv0[r]) for r in range(4)], 0.0)
    fixed1 = pick([(4 + r, sv1[r]) for r in range(3)], 0.0)
    n_valid = pick([(r, float(n)) for r, n in enumerate(_CAND_LEVELS)], 0.0)
    levels = []
    for v in range(max(_CAND_LEVELS)):
        moving0 = sv0[min(4 + v, len(sv0) - 1)]
        lv = jnp.where(row < 4, fixed0 + sv1[v], moving0 + fixed1)
        levels.append(jnp.where(n_valid > float(v), lv, -jnp.inf))
    return levels


def _route_kernel(x_ref, mod_ref, wq_ref, keys_ref, h2_ref, p0_ref, te_ref, e1_ref, st_scr):
    m = mod_ref[0]
    h2 = (x_ref[...] * (1.0 + m[4:5]) + m[3:4]).astype(BF16)
    h2_ref[...] = h2
    q = _dot(h2, wq_ref[...])
    tb = q.shape[0]
    for p in range(2):
        kp = keys_ref[p]
        k_hi = kp.astype(BF16)
        k_lo = (kp - k_hi.astype(F32)).astype(BF16)
        for hd in range(PEER_HEADS):
            c0 = (hd * 2 + p) * PEER_KEYS
            qs = q[:, c0:c0 + PEER_KEYS]
            q_hi = qs.astype(BF16)
            q_lo = (qs - q_hi.astype(F32)).astype(BF16)
            st_scr[hd * 2 + p] = _dot_nt(k_hi, q_hi) + _dot_nt(k_lo, q_hi) + _dot_nt(k_hi, q_lo)

    def per_head(hd, carry):
        s0 = st_scr[hd * 2]
        s1 = st_scr[hd * 2 + 1]
        sv0 = _top_values(s0, _N_TOP)
        sv1 = _top_values(s1, _N_TOP)
        c = _merge_top(_candidate_levels(sv0, sv1), _N_TOP)
        thr = 0.5 * (c[PEER_TOPK - 1] + c[PEER_TOPK])
        z = jnp.zeros_like(thr)
        for k in range(PEER_TOPK):
            z = z + jnp.exp(c[k] - c[0])
        p0_ref[hd] = jnp.exp(s0 - sv0[0]) * (0.5 / z)
        te_ref[hd] = jnp.exp((thr - sv1[0]) - s0)
        e1_ref[hd] = jnp.exp(s1 - sv1[0])
        return carry

    lax.fori_loop(0, PEER_HEADS, per_head, 0)


def _route_call(x1, mod, wq16, keys, layer, n_ctx, lat_len):
    nt = x1.shape[0]
    tb = TB_ROUTE
    cond = functools.partial(_cond_of_tile, tile=tb, n_ctx=n_ctx, lat_len=lat_len)
    fac = lambda: pl.BlockSpec((PEER_HEADS, PEER_KEYS, tb), lambda i: (0, 0, i))
    fshape = jax.ShapeDtypeStruct((PEER_HEADS, PEER_KEYS, nt), F32)
    return pl.pallas_call(
        _route_kernel,
        grid=(nt // tb,),
        in_specs=[pl.BlockSpec((tb, D_MODEL), lambda i: (i, 0)),
                  pl.BlockSpec((1, 6, D_MODEL), lambda i: (cond(i), 0, 0)),
                  pl.BlockSpec((None,) + wq16.shape[1:], lambda i: (layer, 0, 0)),
                  pl.BlockSpec((None,) + keys.shape[1:], lambda i: (layer, 0, 0, 0))],
        out_specs=[pl.BlockSpec((tb, D_MODEL), lambda i: (i, 0)), fac(), fac(), fac()],
        out_shape=[jax.ShapeDtypeStruct((nt, D_MODEL), BF16), fshape, fshape, fshape],
        scratch_shapes=[pltpu.VMEM((2 * PEER_HEADS, PEER_KEYS, tb), F32)],
        compiler_params=_cparams(("parallel",)),
        name="peer_route",
    )(x1, mod, wq16, keys)


def _expert_kernel(h2_ref, u_ref, vt_ref, p0_ref, te_ref, e1_ref, x_ref, mod_ref, lg_ref, lb_ref,
                   o_ref, acc_scr, act_scr):
    c = pl.program_id(1)
    n_lane = h2_ref.shape[0] // PEER_KEYS

    @pl.when(c == 0)
    def _():
        acc_scr[...] = jnp.zeros_like(acc_scr)

    st = _dot_nt(u_ref[...], h2_ref[...])
    th = jnp.tanh(st * (_GELU_C0 + _GELU_C1 * (st * st)))
    act_scr[...] = st + st * th
    for pc in range(EC // EXP_PIECE):
        slabs = range(pc * EXP_PIECE // PEER_KEYS, (pc + 1) * EXP_PIECE // PEER_KEYS)
        g_lanes = []
        for ln in range(n_lane):
            lanes = slice(ln * PEER_KEYS, (ln + 1) * PEER_KEYS)
            w = [None] * len(slabs)
            for hd in range(PEER_HEADS):
                e1 = e1_ref[hd, :, lanes]
                for k, sl in enumerate(slabs):
                    te = te_ref[hd, 0, sl:sl + 1, lanes]
                    e0 = p0_ref[hd, 0, sl:sl + 1, lanes]
                    term = e0 * jnp.where(e1 > te, e1, 0.0)
                    w[k] = term if w[k] is None else w[k] + term
            g_lanes.append(jnp.concatenate(
                [(w[k] * act_scr[sl * PEER_KEYS:(sl + 1) * PEER_KEYS, lanes]).astype(BF16)
                 for k, sl in enumerate(slabs)], axis=0))
        g = jnp.concatenate(g_lanes, axis=1)
        acc_scr[...] += _dot(vt_ref[:, pc * EXP_PIECE:(pc + 1) * EXP_PIECE], g)

    @pl.when(c == pl.num_programs(1) - 1)
    def _():
        m = mod_ref[0]
        y = ALPHA * x_ref[...] + m[5:6] * acc_scr[...].T
        o_ref[...] = _layer_norm(y, lg_ref[...], lb_ref[...])


def _expert_call(h2, u16, vt16, layer, p0, te, e1, x1, mod, lg, lb, n_ctx, lat_len):
    nt = x1.shape[0]
    tb = TB_EXP
    n_chunks = N_EXPERTS // EC
    cond = functools.partial(_cond_of_tile, tile=tb, n_ctx=n_ctx, lat_len=lat_len)
    p0r = p0.reshape(PEER_HEADS, n_chunks, N_SLAB, nt)
    ter = te.reshape(PEER_HEADS, n_chunks, N_SLAB, nt)
    slab = lambda: pl.BlockSpec((PEER_HEADS, 1, N_SLAB, tb), lambda i, c: (0, c, 0, i))
    full = lambda: pl.BlockSpec((PEER_HEADS, PEER_KEYS, tb), lambda i, c: (0, 0, i))
    const = lambda a: pl.BlockSpec(a.shape, lambda i, c: (0,) * a.ndim)
    return pl.pallas_call(
        _expert_kernel,
        grid=(nt // tb, n_chunks),
        in_specs=[pl.BlockSpec((tb, D_MODEL), lambda i, c: (i, 0)),
                  pl.BlockSpec((None, EC, D_MODEL), lambda i, c: (layer, c, 0)),
                  pl.BlockSpec((None, D_MODEL, EC), lambda i, c: (layer, 0, c)),
                  slab(), slab(), full(),
                  pl.BlockSpec((tb, D_MODEL), lambda i, c: (i, 0)),
                  pl.BlockSpec((1, 6, D_MODEL), lambda i, c: (cond(i), 0, 0)),
                  const(lg), const(lb)],
        out_specs=pl.BlockSpec((tb, D_MODEL), lambda i, c: (i, 0)),
        out_shape=jax.ShapeDtypeStruct((nt, D_MODEL), F32),
        scratch_shapes=[pltpu.VMEM((D_MODEL, tb), F32), pltpu.VMEM((EC, tb), F32)],
        compiler_params=_cparams(("parallel", "arbitrary")),
        name="peer_experts",
    )(h2, u16, vt16, p0r, ter, e1, x1, mod, lg, lb)


def _swap_halves(w, n_heads, rot):
    k = w.shape[0]
    w4 = w.reshape(k, n_heads, 2, rot // 2)
    return jnp.concatenate([w4[:, :, 1:2], w4[:, :, 0:1]], axis=2).reshape(k, n_heads * rot)


def _inproj_weight(w):
    a = w[:, 0:512]
    qb, kb, vb = w[:, 512:768], w[:, 768:1024], w[:, 1024:1280]
    cq, ckv, kr = w[:, 1280:1536], w[:, 1536:1664], w[:, 1664:1696]
    qd, kd, vd = w[:, 1696:1952], w[:, 1952:2080], w[:, 2080:2208]
    rep = lambda m: jnp.repeat(m.reshape(D_MODEL, 2, HEAD_DIM), 2, axis=1).reshape(D_MODEL, 256)
    kdr = rep(kd)
    cols = [a, qb, kb, vb, cq, ckv, _pair_cols(None, [kr, kr]), _pair_cols(None, [_swap_halves(kr, 1, MLA_ROPE)] * 2),
            qd, _swap_halves(qd, N_HEADS, HEAD_DIM), kdr, _swap_halves(kdr, N_HEADS, HEAD_DIM), rep(vd)]
    return jnp.concatenate(cols, axis=1).astype(BF16)


def _pair_cols(nope, rope):
    k = rope[0].shape[0]
    zeros = lambda w: jnp.zeros((k, w), rope[0].dtype)
    nope = nope if nope is not None else [zeros(MLA_NOPE)] * 2
    rope = rope if rope is not None else [zeros(MLA_ROPE)] * 2
    return jnp.concatenate([nope[0], rope[0], nope[1], rope[1], zeros(256 - 2 * PAIR_W)], axis=1)


def _mla_weights(w_uq, w_ukv):
    q3 = w_uq.reshape(MLA_Q_RANK, N_HEADS, PAIR_W)
    nope = [q3[:, hd, :MLA_NOPE] for hd in range(N_HEADS)]
    rope = [q3[:, hd, MLA_NOPE:] for hd in range(N_HEADS)]
    rope_sw = [_swap_halves(r, 1, MLA_ROPE) for r in rope]
    none64 = [jnp.zeros_like(nope[0])] * 2
    wuq = jnp.concatenate([_pair_cols(nope[0:2], rope[0:2]), _pair_cols(nope[2:4], rope[2:4]),
                           _pair_cols(none64, rope_sw[0:2]), _pair_cols(none64, rope_sw[2:4])], axis=1).astype(BF16)
    kv3 = w_ukv.reshape(MLA_KV_RANK, N_HEADS, MLA_NOPE + 64)
    knope = [kv3[:, hd, :MLA_NOPE] for hd in range(N_HEADS)]
    none32 = [jnp.zeros((MLA_KV_RANK, MLA_ROPE), w_ukv.dtype)] * 2
    wukv = jnp.concatenate([_pair_cols(knope[0:2], none32), _pair_cols(knope[2:4], none32),
                            kv3[:, :, MLA_NOPE:].reshape(MLA_KV_RANK, 256)], axis=1).astype(BF16)
    return wuq, wukv


def _pair_rope_tables(lat_len, tile):
    cos_t, sin_t = _rope_tables(lat_len, MLA_ROPE, tile)
    cos32, sin32 = cos_t[:, :MLA_ROPE], sin_t[:, :MLA_ROPE]
    n = cos_t.shape[0]
    one, zero = jnp.ones((n, MLA_NOPE), F32), jnp.zeros((n, MLA_NOPE), F32)
    return (jnp.concatenate([one, cos32, one, cos32, one], axis=1),
            jnp.concatenate([zero, sin32, zero, sin32, zero], axis=1))


def _rope_tables(lat_len, rot, tile):
    t = jnp.arange(lat_len)
    row = (t // GRID_W).astype(F32)
    col = (t % GRID_W).astype(F32)
    nf = rot // 4
    freqs = ROPE_BASE ** (-jnp.arange(nf, dtype=F32) / nf)
    ang = jnp.concatenate([row[:, None] * freqs, col[:, None] * freqs], -1)
    cos, sin = jnp.cos(ang), jnp.sin(ang)
    cos_t = jnp.tile(jnp.concatenate([cos, cos], -1), (1, N_HEADS))
    sin_t = jnp.tile(jnp.concatenate([-sin, sin], -1), (1, N_HEADS))
    w = N_HEADS * rot
    return (jnp.concatenate([jnp.ones((tile, w), F32), cos_t], 0),
            jnp.concatenate([jnp.zeros((tile, w), F32), sin_t], 0))


def _na_bias_tables(rpb, rows):
    kh = min(NA_KH, rows)
    qrow = np.array([0, 1, 2, 3, rows // 2, rows - 3, rows - 2, rows - 1])
    start = np.clip(qrow - kh // 2, 0, rows - kh)
    dr0 = start - qrow + NA_KH - 1
    qc = np.arange(GRID_W)
    kc = np.arange(GRID_W)
    cstart = np.clip(qc - NA_KW // 2, 0, GRID_W - NA_KW)
    ok = (kc[None, :] >= cstart[:, None]) & (kc[None, :] < cstart[:, None] + NA_KW)
    n_h, n_a, n_c = rpb.shape
    edge = GRID_W - NA_KW
    w = jnp.concatenate([jnp.broadcast_to(rpb[:, :, :1], (n_h, n_a, edge)), rpb,
                         jnp.broadcast_to(rpb[:, :, -1:], (n_h, n_a, edge + 1))], axis=-1)
    skew = jnp.tile(w, (1, 1, GRID_W))[:, :, :GRID_W * (2 * GRID_W - 1)].reshape(n_h, n_a, GRID_W, 2 * GRID_W - 1)
    toep = skew[:, :, :, GRID_W - 1:]
    tabs = []
    for d0 in dr0:
        b = jnp.where(ok[None, None], toep[:, d0:d0 + kh], NEG_INF)
        tabs.append(b.transpose(0, 2, 1, 3).reshape(N_HEADS * GRID_W, kh * GRID_W))
    return jnp.stack(tabs)


def _swa_bias_tables(lat_len):
    nb = lat_len // SWA_WIN
    qi = np.arange(SWA_WIN)
    kj = np.arange(3 * SWA_WIN)
    tabs = []
    for n in (0, 1, nb - 1):
        kpos = _swa_window_start(n, nb) + kj
        in_win = np.abs(kpos[None, :] - (n * SWA_WIN + qi[:, None])) <= SWA_WIN
        tabs.append(np.where(in_win, 0.0, NEG_INF))
    return jnp.asarray(np.stack(tabs), F32)


def _swa_window_start(n, nb):
    lo = n - 1
    lo = jnp.clip(lo, 0, nb - 3) if isinstance(n, jax.Array) else min(max(lo, 0), nb - 3)
    return lo * SWA_WIN


def _heads_to_lanes(t):
    b, h, s, d = t.shape
    return t.transpose(0, 2, 1, 3).reshape(b, s, h * d)


def _lanes_to_heads(t, b, s, h):
    return t.reshape(b, s, h, -1).transpose(0, 2, 1, 3)


def kernel(x_prompt, x_sample, cache_nat_k, cache_nat_v, cache_mla_ckv, cache_mla_krope, cache_swa_k, cache_swa_v, c, c_ctx, w_in, w_out, out_norm_g, w_ada, b_ada, ln1_g, ln1_b, ln2_g, ln2_b, a_norm_g, a_norm_b, a_w_s, a_b_s, nat_rpb, mla_q_norm_g, mla_w_uq, mla_kv_norm_g, mla_w_ukv, swa_sinks, peer_w_q, peer_sub_keys, peer_u, peer_v):
    n_b, seq, d = x_prompt.shape
    n_db, lat_len, _ = x_sample.shape
    past = cache_nat_k.shape[3]
    n_ctx = n_b * seq
    nt = n_ctx + n_db * lat_len
    rows = lat_len // GRID_W

    x = jnp.concatenate([x_prompt.reshape(n_ctx, d), x_sample.reshape(n_db * lat_len, d)], axis=0)
    conds = jnp.zeros((8, d), F32).at[0].set(c_ctx).at[1:1 + n_db].set(c)
    mods = _ada_call(conds, w_ada, b_ada).reshape(DEPTH, 8, 6, d)

    cos_c, sin_c = _pair_rope_tables(lat_len, TM)
    cos_d, sin_d = _rope_tables(lat_len, HEAD_DIM, TM)
    swa_bias = _swa_bias_tables(lat_len)
    nb_swa = lat_len // SWA_WIN

    q_ctx = lambda b, j: b
    lat_q = lambda tq: (lambda b, j: (n_ctx + b * lat_len) // tq + j)
    ctx_keys = lambda w: pl.BlockSpec((seq, w), lambda b, j: (b, 0))
    lat_keys = lambda w: pl.BlockSpec((lat_len, w), lambda b, j: (n_ctx // lat_len + b, 0))

    wq16 = peer_w_q.astype(BF16)
    u16 = peer_u.astype(BF16)
    vt16 = jnp.swapaxes(peer_v, 1, 2).astype(BF16)

    states = [[] for _ in range(6)]
    for l in range(DEPTH):
        mod = mods[l]
        wbig = _inproj_weight(w_in[l])
        wuq, wukv = _mla_weights(mla_w_uq[l], mla_w_ukv[l])
        mixer_a = (a_norm_g[l][None], a_norm_b[l][None], a_w_s[l].astype(BF16),
                   jnp.repeat(a_b_s[l].T, HEAD_DIM, axis=1))
        (oa, qb, kb, vb, kb16, vb16, qp0, qp1, ckvn, krp, kp0, kp1, vc,
         qd, kd, vd, kd16, vd16) = _inproj_call(
            x, mod, wbig, wuq, wukv, mla_q_norm_g[l][None], mla_kv_norm_g[l][None],
            (cos_c, sin_c, cos_d, sin_d), mixer_a, n_ctx, lat_len)

        ob_c = _attn_call("ctx_attn_b", (n_b, 1), seq, seq, HEAD_DIM ** -0.5, q_ctx, qb, kb16, vb16, ctx_keys)
        oc_c = _attn_call("ctx_attn_c", (n_b, 1), seq, seq, MLA_SCALE, q_ctx, qp0, kp0, vc, ctx_keys,
                          q2=qp1, k2=kp1)
        od_c = _attn_call("ctx_attn_d", (n_b, 1), seq, seq, HEAD_DIM ** -0.5, q_ctx, qd, kd16, vd16, ctx_keys,
                          sinks=swa_sinks[l])

        na_bias = _na_bias_tables(nat_rpb[l], rows)
        kh = min(NA_KH, rows)
        ob_l = _attn_call(
            "lat_attn_b", (n_db, rows // NA_ROWS_PER_STEP), GRID_W, kh * GRID_W, HEAD_DIM ** -0.5,
            lat_q(NA_ROWS_PER_STEP * GRID_W), qb, kb16, vb16, lat_keys, n_sub=NA_ROWS_PER_STEP,
            extra=(_heads_to_lanes(cache_nat_k[:, l]).astype(BF16), _heads_to_lanes(cache_nat_v[:, l]).astype(BF16)),
            bias=na_bias,
            bias_index=lambda r: jnp.where(r < 4, r, jnp.where(r > rows - 4, r - (rows - 8), 4)),
            start_fn=lambda r: jnp.clip(r - kh // 2, 0, rows - kh) * GRID_W)
        krx = cache_mla_krope[:, l].reshape(n_db * past, MLA_ROPE)
        kx0, kx1, vx = _kvexp_call(cache_mla_ckv[:, l].reshape(n_db * past, MLA_KV_RANK),
                                   _pair_cols(None, [krx, krx]), wukv)
        oc_l = _attn_call(
            "lat_attn_c", (n_db, lat_len // 256), 256, lat_len, MLA_SCALE, lat_q(256), qp0, kp0, vc, lat_keys,
            q2=qp1, k2=kp1,
            extra=tuple(t.reshape(n_db, past, 256) for t in (kx0, kx1, vx)))
        rep_kv = lambda t: jnp.repeat(t, 2, axis=1)
        od_l = _attn_call(
            "lat_attn_d", (n_db, nb_swa // SWA_BLOCKS_PER_STEP), SWA_WIN, 3 * SWA_WIN, HEAD_DIM ** -0.5,
            lat_q(SWA_BLOCKS_PER_STEP * SWA_WIN), qd, kd16, vd16, lat_keys, n_sub=SWA_BLOCKS_PER_STEP,
            extra=(_heads_to_lanes(rep_kv(cache_swa_k[:, l])).astype(BF16),
                   _heads_to_lanes(rep_kv(cache_swa_v[:, l])).astype(BF16)),
            bias=swa_bias,
            bias_index=lambda n: jnp.where(n == 0, 0, jnp.where(n == nb_swa - 1, 2, 1)),
            sinks=swa_sinks[l],
            start_fn=lambda n: _swa_window_start(n, nb_swa))
        ob, oc, od = (ob_c, ob_l), (oc_c, oc_l), (od_c, od_l)

        x1 = _merge_call(oa, ob, oc, od, x, mod, out_norm_g[l][None], w_out[l].astype(BF16),
                         ln1_g[l][None], ln1_b[l][None], n_ctx, lat_len)
        h2, p0, te, e1 = _route_call(x1, mod, wq16, peer_sub_keys, l, n_ctx, lat_len)
        x = _expert_call(h2, u16, vt16, l, p0, te, e1, x1, mod,
                         ln2_g[l][None], ln2_b[l][None], n_ctx, lat_len)

        states[0].append(_lanes_to_heads(kb[:n_ctx], n_b, seq, N_HEADS))
        states[1].append(_lanes_to_heads(vb[:n_ctx], n_b, seq, N_HEADS))
        states[2].append(ckvn[:n_ctx].reshape(n_b, seq, MLA_KV_RANK))
        states[3].append(krp[:n_ctx, MLA_NOPE:PAIR_W].reshape(n_b, seq, MLA_ROPE))
        states[4].append(_lanes_to_heads(kd[:n_ctx], n_b, seq, N_HEADS)[:, ::2])
        states[5].append(_lanes_to_heads(vd[:n_ctx], n_b, seq, N_HEADS)[:, ::2])

    y_prompt = x[:n_ctx].reshape(n_b, seq, d)
    y_sample = x[n_ctx:].reshape(n_db, lat_len, d)
    return (y_prompt, y_sample) + tuple(jnp.stack(s, axis=1) for s in states)
```

```python
import functools
import math

import jax
import jax.numpy as jnp
import numpy as np
from jax import lax
from jax.experimental import pallas as pl
from jax.experimental.pallas import tpu as pltpu

F32 = jnp.float32
BF16 = jnp.bfloat16

D_MODEL = 1024
DEPTH = 2
GRID_W = 64
HEAD_DIM = 64
N_HEADS = 4
GROUP_W = 256
CHUNK = 128
NA_KH = 8
NA_KW = 16
MLA_Q_RANK = 256
MLA_KV_RANK = 128
MLA_NOPE = 64
MLA_ROPE = 32
MLA_SCALE = (MLA_NOPE + MLA_ROPE) ** -0.5
SWA_WIN = 128
PEER_HEADS = 8
PEER_KEYS = 128
PEER_TOPK = 16
N_EXPERTS = PEER_KEYS * PEER_KEYS
ROPE_BASE = 10000.0
LN_EPS = 1e-5
NEG_INF = -1e30
ALPHA = (2 * DEPTH) ** 0.25

V7X_VMEM_LIMIT_BYTES = 56 * 1024 * 1024
TM = 256
TB_ROUTE = 256
TB_EXP = 1024
ATTN_KEY_CHUNK = 1024
SINK_PAD = 128
NA_ROWS_PER_STEP = 8
SWA_BLOCKS_PER_STEP = 4
EC = 1024
N_SLAB = EC // PEER_KEYS
EXP_PIECE = 256
_C_A = 0
_C_QB, _C_KB, _C_VB = 512, 768, 1024
_C_CQ, _C_CKV, _C_KR, _C_KRS = 1280, 1536, 1664, 1920
_C_QD, _C_QDS, _C_KD, _C_KDS, _C_VD = 2176, 2432, 2688, 2944, 3200
_C_END = 3456
PAIR_W = MLA_NOPE + MLA_ROPE


def _cparams(sem):
    return pltpu.CompilerParams(dimension_semantics=sem, vmem_limit_bytes=V7X_VMEM_LIMIT_BYTES)


def _dot(a, b):
    return jnp.dot(a, b, preferred_element_type=F32)


def _dot_nt(a, b):
    return lax.dot_general(a, b, (((1,), (1,)), ((), ())), preferred_element_type=F32)


def _layer_norm(x, g, b):
    mu = jnp.mean(x, axis=-1, keepdims=True)
    xc = x - mu
    var = jnp.mean(xc * xc, axis=-1, keepdims=True)
    return xc * lax.rsqrt(var + LN_EPS) * g + b


def _rms_norm(x, g):
    return x * lax.rsqrt(jnp.mean(x * x, axis=-1, keepdims=True) + LN_EPS) * g


def _ada_kernel(c_ref, w_ref, b_ref, o_ref):
    c = c_ref[...]
    a = c * jax.nn.sigmoid(c)
    a_hi = a.astype(BF16)
    a_lo = (a - a_hi.astype(F32)).astype(BF16)
    w = w_ref[0]
    w_hi = w.astype(BF16)
    w_lo = (w - w_hi.astype(F32)).astype(BF16)
    o_ref[0] = _dot(a_hi, w_hi) + _dot(a_hi, w_lo) + _dot(a_lo, w_hi) + b_ref[0]


def _ada_call(conds, w_ada, b_ada):
    tn = 1536
    n = w_ada.shape[-1]
    return pl.pallas_call(
        _ada_kernel,
        grid=(DEPTH, n // tn),
        in_specs=[pl.BlockSpec((8, D_MODEL), lambda l, j: (0, 0)),
                  pl.BlockSpec((1, D_MODEL, tn), lambda l, j: (l, 0, j)),
                  pl.BlockSpec((1, 1, tn), lambda l, j: (l, 0, j))],
        out_specs=pl.BlockSpec((1, 8, tn), lambda l, j: (l, 0, j)),
        out_shape=jax.ShapeDtypeStruct((DEPTH, 8, n), F32),
        compiler_params=_cparams(("parallel", "parallel")),
        name="ada_mod",
    )(conds, w_ada, b_ada.reshape(DEPTH, 1, n))


def _inproj_kernel(x_ref, mod_ref, w_ref, wuq_ref, wukv_ref, gq_ref, gkv_ref,
                   cosc_ref, sinc_ref, cosd_ref, sind_ref, ag_ref, ab_ref, aws_ref, abs_ref,
                   oa_ref, qb_ref, kb_ref, vb_ref, kb16_ref, vb16_ref,
                   qp0_ref, qp1_ref, ckvn_ref, krp_ref, kp0_ref, kp1_ref, vc_ref,
                   qd_ref, kd_ref, vd_ref, kd16_ref, vd16_ref):
    m = mod_ref[0]
    h = x_ref[...] * (1.0 + m[1:2]) + m[0:1]
    z = _dot(h.astype(BF16), w_ref[...])
    cosc, sinc = cosc_ref[...], sinc_ref[...]
    cosd, sind = cosd_ref[...], sind_ref[...]

    for c0 in range(0, z.shape[0], CHUNK):
        oa_ref[c0:c0 + CHUNK, :] = _chunk_mlp(z[c0:c0 + CHUNK, _C_A:_C_QB], ag_ref[...], ab_ref[...], aws_ref,
                                              abs_ref[...])
    qb_ref[...] = z[:, _C_QB:_C_KB].astype(BF16)
    kb = z[:, _C_KB:_C_VB]
    vb = z[:, _C_VB:_C_CQ]
    kb_ref[...] = kb
    vb_ref[...] = vb
    kb16_ref[...] = kb.astype(BF16)
    vb16_ref[...] = vb.astype(BF16)

    cqn = _rms_norm(z[:, _C_CQ:_C_CKV], gq_ref[...])
    q = _dot(cqn.astype(BF16), wuq_ref[...])
    qp0_ref[...] = (q[:, 0:256] * cosc + q[:, 512:768] * sinc).astype(BF16)
    qp1_ref[...] = (q[:, 256:512] * cosc + q[:, 768:1024] * sinc).astype(BF16)
    ckvn = _rms_norm(z[:, _C_CKV:_C_KR], gkv_ref[...])
    ckvn_ref[...] = ckvn
    kv = _dot(ckvn.astype(BF16), wukv_ref[...])
    kr = z[:, _C_KR:_C_KRS] * cosc + z[:, _C_KRS:_C_QD] * sinc
    krp_ref[...] = kr
    kp0_ref[...] = (kv[:, 0:256] + kr).astype(BF16)
    kp1_ref[...] = (kv[:, 256:512] + kr).astype(BF16)
    vc_ref[...] = kv[:, 512:768].astype(BF16)

    qd_ref[...] = (z[:, _C_QD:_C_QDS] * cosd + z[:, _C_QDS:_C_KD] * sind).astype(BF16)
    kd = z[:, _C_KD:_C_KDS] * cosd + z[:, _C_KDS:_C_VD] * sind
    vd = z[:, _C_VD:_C_END]
    kd_ref[...] = kd
    vd_ref[...] = vd
    kd16_ref[...] = kd.astype(BF16)
    vd16_ref[...] = vd.astype(BF16)


def _cond_of_tile(i, tile, n_ctx, lat_len):
    n_ctx_tiles = n_ctx // tile
    return jnp.where(i < n_ctx_tiles, 0, 1 + (i - n_ctx_tiles) // (lat_len // tile))


def _rope_block_of_tile(i, tile, n_ctx, lat_len):
    n_ctx_tiles = n_ctx // tile
    return jnp.where(i < n_ctx_tiles, 0, 1 + (i - n_ctx_tiles) % (lat_len // tile))


def _inproj_call(x, mod, wbig, wuq, wukv, gq, gkv, tabs, mixer_a, n_ctx, lat_len):
    nt = x.shape[0]
    cond = functools.partial(_cond_of_tile, tile=TM, n_ctx=n_ctx, lat_len=lat_len)
    rblk = functools.partial(_rope_block_of_tile, tile=TM, n_ctx=n_ctx, lat_len=lat_len)
    row = lambda w: pl.BlockSpec((TM, w), lambda i: (i, 0))
    const = lambda a: pl.BlockSpec(a.shape, lambda i: (0,) * a.ndim)
    tab = lambda w: pl.BlockSpec((TM, w), lambda i: (rblk(i), 0))
    outs = [(256, F32), (256, BF16), (256, F32), (256, F32), (256, BF16), (256, BF16),
            (256, BF16), (256, BF16), (128, F32), (256, F32), (256, BF16), (256, BF16), (256, BF16),
            (256, BF16), (256, F32), (256, F32), (256, BF16), (256, BF16)]
    return pl.pallas_call(
        _inproj_kernel,
        grid=(nt // TM,),
        in_specs=[row(D_MODEL),
                  pl.BlockSpec((1, 6, D_MODEL), lambda i: (cond(i), 0, 0)),
                  const(wbig), const(wuq), const(wukv), const(gq), const(gkv),
                  tab(256), tab(256), tab(256), tab(256)] + [const(a) for a in mixer_a],
        out_specs=[row(w) for w, _ in outs],
        out_shape=[jax.ShapeDtypeStruct((nt, w), dt) for w, dt in outs],
        compiler_params=_cparams(("parallel",)),
        name="inproj",
    )(x, mod, wbig, wuq, wukv, gq, gkv, *tabs, *mixer_a)


def _kvexp_kernel(c_ref, kr_ref, w_ref, k0_ref, k1_ref, v_ref):
    kv = _dot(c_ref[...].astype(BF16), w_ref[...])
    kr = kr_ref[...]
    k0_ref[...] = (kv[:, 0:256] + kr).astype(BF16)
    k1_ref[...] = (kv[:, 256:512] + kr).astype(BF16)
    v_ref[...] = kv[:, 512:768].astype(BF16)


def _kvexp_call(ckv, kr_pair, wukv):
    n = ckv.shape[0]
    whole = lambda a: pl.BlockSpec(a.shape, lambda i: (0, 0))
    return pl.pallas_call(
        _kvexp_kernel,
        grid=(1,),
        in_specs=[whole(ckv), whole(kr_pair), whole(wukv)],
        out_specs=[pl.BlockSpec((n, 256), lambda i: (0, 0))] * 3,
        out_shape=[jax.ShapeDtypeStruct((n, 256), BF16)] * 3,
        compiler_params=_cparams(("arbitrary",)),
        name="mla_cache_expand",
    )(ckv, kr_pair, wukv)


def _chunk_mlp(z, gain, bias, ws_ref, bs):
    g = jax.nn.gelu(z)
    u = g[:, 0:GROUP_W]
    v = _layer_norm(g[:, GROUP_W:2 * GROUP_W], gain, bias).astype(BF16)
    lane_head = lax.broadcasted_iota(jnp.int32, (1, GROUP_W), 1) // HEAD_DIM
    mixed = bs
    for hd in range(N_HEADS):
        mixed = mixed + jnp.where(lane_head == hd, _dot(ws_ref[hd], v), 0.0)
    return u * mixed


def _attn_kernel(*refs, tq, n_sub, **static):
    for sb in range(n_sub):
        _attn_tile(refs, pl.program_id(1) * n_sub + sb, slice(sb * tq, (sb + 1) * tq), tq=tq, **static)


def _attn_tile(refs, tile, q_rows, *, tq, wk, kc, scale, has_q2, has_extra, bias_heads, has_sink, start_fn, bias_index):
    refs = list(refs)
    xbias_ref = refs.pop(0) if has_sink else None
    q1_ref = refs.pop(0)
    q2_ref = refs.pop(0) if has_q2 else None
    k1_ref = refs.pop(0)
    k2_ref = refs.pop(0) if has_q2 else None
    v_ref = refs.pop(0)
    if has_extra:
        xk1_ref = refs.pop(0)
        xk2_ref = refs.pop(0) if has_q2 else None
        xv_ref = refs.pop(0)
    bias_ref = refs.pop(0) if bias_heads else None
    o_ref = refs.pop(0)

    def stack_heads(q, width, n):
        lane_head = lax.broadcasted_iota(jnp.int32, (1, q.shape[1]), 1) // width
        return jnp.concatenate([jnp.where(lane_head == hd, q, jnp.zeros_like(q)) for hd in range(n)], axis=0)

    if has_q2:
        q1s = stack_heads(q1_ref[q_rows, :], PAIR_W, 2)
        q2s = stack_heads(q2_ref[q_rows, :], PAIR_W, 2)
    else:
        q1s = stack_heads(q1_ref[q_rows, :], HEAD_DIM, N_HEADS)

    start = start_fn(tile)
    if not isinstance(start, int):
        start = pl.multiple_of(start, 64)
    bias_tile = bias_ref.at[bias_index(tile)] if bias_heads else None

    chunks = [("win", c0, min(kc, wk - c0)) for c0 in range(0, wk, kc)]
    if has_extra:
        chunks.append(("extra", 0, 0))
    m = denom = o = None
    for kind, c0, n in chunks:
        if kind == "win":
            rows = pl.ds(start if c0 == 0 else start + c0, n)
            k1c, vc = k1_ref[rows, :], v_ref[rows, :]
            k2c = k2_ref[rows, :] if has_q2 else None
        else:
            k1c, vc = xk1_ref[...], xv_ref[...]
            k2c = xk2_ref[...] if has_q2 else None
        s = _dot_nt(q1s, k1c)
        if has_q2:
            s = jnp.concatenate([s, _dot_nt(q2s, k2c)], axis=0)
        s = s * scale
        if kind == "win" and bias_heads == N_HEADS:
            s = s + bias_tile[:, c0:c0 + n]
        elif kind == "win" and bias_heads == 1:
            s = s + jnp.concatenate([bias_tile[:, c0:c0 + n]] * N_HEADS, axis=0)
        elif kind == "extra" and has_sink:
            s = s + xbias_ref[...]
        mc = jnp.max(s, axis=-1, keepdims=True)
        m_new = mc if m is None else jnp.maximum(m, mc)
        p = jnp.exp(s - m_new)
        pv = _dot(p.astype(BF16), vc)
        if m is None:
            denom, o = jnp.sum(p, axis=-1, keepdims=True), pv
        else:
            alpha = jnp.exp(m - m_new)
            denom = alpha * denom + jnp.sum(p, axis=-1, keepdims=True)
            o = alpha * o + pv
        m = m_new
    o = o / denom
    lane_head = lax.broadcasted_iota(jnp.int32, (1, N_HEADS * HEAD_DIM), 1) // HEAD_DIM
    out = jnp.zeros((tq, N_HEADS * HEAD_DIM), F32)
    for hd in range(N_HEADS):
        out = out + jnp.where(lane_head == hd, o[hd * tq:(hd + 1) * tq], 0.0)
    o_ref[q_rows, :] = out


def _attn_call(name, grid, tq, wk, scale, q_index, q1, k1, v, k_spec_fn, *, q2=None, k2=None, extra=None,
               bias=None, bias_index=None, sinks=None, start_fn=lambda t: 0, kc=ATTN_KEY_CHUNK, n_sub=1):
    tb = n_sub * tq
    has_q2 = q2 is not None
    bias_heads = 0 if bias is None else bias.shape[1] // tq
    args, specs = [], []
    extra_index = lambda b, j: (b, 0, 0)
    if sinks is not None:
        assert not has_q2
        if extra is None:
            extra = tuple(jnp.zeros((1, 0, 256), BF16) for _ in range(2))
            extra_index = lambda b, j: (0, 0, 0)
        n_real = extra[0].shape[1]
        extra = tuple(jnp.pad(a, ((0, 0), (0, SINK_PAD), (0, 0))) for a in extra)
        col = jnp.arange(n_real + SINK_PAD)[None, :]
        sink_rows = jnp.repeat(sinks.astype(F32), tq)[:, None]
        xbias = jnp.where(col < n_real, 0.0, jnp.where(col == n_real, sink_rows, NEG_INF))
        args.append(xbias)
        specs.append(pl.BlockSpec(xbias.shape, lambda b, j: (0, 0)))
    has_extra = extra is not None
    args.append(q1)
    specs.append(pl.BlockSpec((tb, 256), lambda b, j: (q_index(b, j), 0)))
    if has_q2:
        args.append(q2)
        specs.append(pl.BlockSpec((tb, 256), lambda b, j: (q_index(b, j), 0)))
    args.append(k1)
    specs.append(k_spec_fn(256))
    if has_q2:
        args.append(k2)
        specs.append(k_spec_fn(256))
    args.append(v)
    specs.append(k_spec_fn(256))
    if has_extra:
        for a in extra:
            args.append(a)
            specs.append(pl.BlockSpec((None,) + a.shape[1:], extra_index))
    if bias is not None:
        args.append(bias)
        specs.append(pl.BlockSpec(bias.shape, lambda b, j: (0, 0, 0)))
    kern = functools.partial(_attn_kernel, tq=tq, n_sub=n_sub, wk=wk, kc=kc, scale=scale, has_q2=has_q2,
                             has_extra=has_extra, bias_heads=bias_heads, has_sink=sinks is not None,
                             start_fn=start_fn, bias_index=bias_index)
    return pl.pallas_call(
        kern,
        grid=grid,
        in_specs=specs,
        out_specs=pl.BlockSpec((tb, 256), lambda b, j: (b * grid[1] + j, 0)),
        out_shape=jax.ShapeDtypeStruct((grid[0] * grid[1] * tb, 256), F32),
        compiler_params=_cparams(("parallel", "parallel")),
        name=name,
    )(*args)


def _merge_kernel(oa_ref, obc_ref, obl_ref, occ_ref, ocl_ref, odc_ref, odl_ref, x_ref, mod_ref, g_ref, w_ref,
                  lg_ref, lb_ref, o_ref, *, n_ctx_tiles):
    m = mod_ref[0]
    is_ctx = pl.program_id(0) < n_ctx_tiles
    pick = lambda c_ref, l_ref: jnp.where(is_ctx, c_ref[...], l_ref[...])
    groups = (oa_ref[...], pick(obc_ref, obl_ref), pick(occ_ref, ocl_ref), pick(odc_ref, odl_ref))
    acc = None
    for gi, o in enumerate(groups):
        og = _rms_norm(o, g_ref[:, gi * GROUP_W:(gi + 1) * GROUP_W]).astype(BF16)
        part = _dot(og, w_ref[gi * GROUP_W:(gi + 1) * GROUP_W, :])
        acc = part if acc is None else acc + part
    y = ALPHA * x_ref[...] + m[2:3] * acc
    o_ref[...] = _layer_norm(y, lg_ref[...], lb_ref[...])


def _merge_call(oa, ob, oc, od, x, mod, gout, wout16, lg, lb, n_ctx, lat_len):
    nt = x.shape[0]
    nct = n_ctx // TM
    cond = functools.partial(_cond_of_tile, tile=TM, n_ctx=n_ctx, lat_len=lat_len)
    row = lambda w: pl.BlockSpec((TM, w), lambda i: (i, 0))
    ctx_row = lambda: pl.BlockSpec((TM, 256), lambda i: (jnp.minimum(i, nct - 1), 0))
    lat_row = lambda: pl.BlockSpec((TM, 256), lambda i: (jnp.maximum(i - nct, 0), 0))
    const = lambda a: pl.BlockSpec(a.shape, lambda i: (0,) * a.ndim)
    return pl.pallas_call(
        functools.partial(_merge_kernel, n_ctx_tiles=nct),
        grid=(nt // TM,),
        in_specs=[row(256), ctx_row(), lat_row(), ctx_row(), lat_row(), ctx_row(), lat_row(), row(D_MODEL),
                  pl.BlockSpec((1, 6, D_MODEL), lambda i: (cond(i), 0, 0)),
                  const(gout), const(wout16), const(lg), const(lb)],
        out_specs=row(D_MODEL),
        out_shape=jax.ShapeDtypeStruct((nt, D_MODEL), F32),
        compiler_params=_cparams(("parallel",)),
        name="merge_out",
    )(oa, *ob, *oc, *od, x, mod, gout, wout16, lg, lb)


_N_TOP = PEER_TOPK + 1
_GELU_C0 = math.sqrt(2.0 / math.pi)
_GELU_C1 = 0.044715 * _GELU_C0


def _sort_network(n):
    pairs = []
    p = 1
    while p < n:
        k = p
        while k >= 1:
            for j in range(k % p, n - k, 2 * k):
                for i in range(min(k, n - j - k)):
                    if (i + j) // (2 * p) == (i + j + k) // (2 * p):
                        pairs.append((i + j, i + j + k))
            k //= 2
        p *= 2
    return pairs


def _merge_top(levels, n_top):
    levels = list(levels)
    sub = lax.broadcasted_iota(jnp.int32, levels[0].shape, 0).astype(F32)
    out = []
    for k in range(n_top):
        head = levels[0]
        m = jnp.max(head, axis=0, keepdims=True)
        out.append(m)
        first = jnp.min(jnp.where(head == m, sub, 8.0), axis=0, keepdims=True)
        pop = sub == first
        for v in range(n_top - 1 - k):
            nxt = levels[v + 1] if v + 1 < len(levels) else -jnp.inf
            levels[v] = jnp.where(pop, nxt, levels[v])
    return out


def _top_values(s, n_top):
    g = [s[8 * v:8 * v + 8] for v in range(s.shape[0] // 8)]
    for i, j in _sort_network(len(g)):
        g[i], g[j] = jnp.maximum(g[i], g[j]), jnp.minimum(g[i], g[j])
    return _merge_top(g, n_top)


_CAND_LEVELS = (17, 8, 5, 4, 13, 4, 1, 0)


def _candidate_levels(sv0, sv1):
    shape = (8,) + sv0[0].shape[1:]
    row = lax.broadcasted_iota(jnp.int32, shape, 0)
    pick = lambda vals, default: functools.reduce(
        lambda acc, rv: jnp.where(row == rv[0], rv[1], acc), vals, jnp.full(shape, default, F32))
    fixed0 = pick([(r, sv0[r]) for r in range(4)], 0.0)
    fixed1 = pick([(4 + r, sv1[r]) for r in range(3)], 0.0)
    n_valid = pick([(r, float(n)) for r, n in enumerate(_CAND_LEVELS)], 0.0)
    levels = []
    for v in range(max(_CAND_LEVELS)):
        moving0 = sv0[min(4 + v, len(sv0) - 1)]
        lv = jnp.where(row < 4, fixed0 + sv1[v], moving0 + fixed1)
        levels.append(jnp.where(n_valid > float(v), lv, -jnp.inf))
    return levels


def _route_kernel(x_ref, mod_ref, wq_ref, keys_ref, h2_ref, p0_ref, te_ref, e1_ref, st_scr):
    m = mod_ref[0]
    h2 = (x_ref[...] * (1.0 + m[4:5]) + m[3:4]).astype(BF16)
    h2_ref[...] = h2
    q = _dot(h2, wq_ref[...])
    tb = q.shape[0]
    for p in range(2):
        kp = keys_ref[p]
        k_hi = kp.astype(BF16)
        k_lo = (kp - k_hi.astype(F32)).astype(BF16)
        for hd in range(PEER_HEADS):
            c0 = (hd * 2 + p) * PEER_KEYS
            qs = q[:, c0:c0 + PEER_KEYS]
            q_hi = qs.astype(BF16)
            q_lo = (qs - q_hi.astype(F32)).astype(BF16)
            st_scr[hd * 2 + p] = _dot_nt(k_hi, q_hi) + _dot_nt(k_lo, q_hi) + _dot_nt(k_hi, q_lo)

    def per_head(hd, carry):
        s0 = st_scr[hd * 2]
        s1 = st_scr[hd * 2 + 1]
        sv0 = _top_values(s0, _N_TOP)
        sv1 = _top_values(s1, _N_TOP)
        c = _merge_top(_candidate_levels(sv0, sv1), _N_TOP)
        thr = 0.5 * (c[PEER_TOPK - 1] + c[PEER_TOPK])
        z = jnp.zeros_like(thr)
        for k in range(PEER_TOPK):
            z = z + jnp.exp(c[k] - c[0])
        p0_ref[hd] = jnp.exp(s0 - sv0[0]) * (0.5 / z)
        te_ref[hd] = jnp.exp((thr - sv1[0]) - s0)
        e1_ref[hd] = jnp.exp(s1 - sv1[0])
        return carry

    lax.fori_loop(0, PEER_HEADS, per_head, 0)


def _route_call(x1, mod, wq16, keys, layer, n_ctx, lat_len):
    nt = x1.shape[0]
    tb = TB_ROUTE
    cond = functools.partial(_cond_of_tile, tile=tb, n_ctx=n_ctx, lat_len=lat_len)
    fac = lambda: pl.BlockSpec((PEER_HEADS, PEER_KEYS, tb), lambda i: (0, 0, i))
    fshape = jax.ShapeDtypeStruct((PEER_HEADS, PEER_KEYS, nt), F32)
    return pl.pallas_call(
        _route_kernel,
        grid=(nt // tb,),
        in_specs=[pl.BlockSpec((tb, D_MODEL), lambda i: (i, 0)),
                  pl.BlockSpec((1, 6, D_MODEL), lambda i: (cond(i), 0, 0)),
                  pl.BlockSpec((None,) + wq16.shape[1:], lambda i: (layer, 0, 0)),
                  pl.BlockSpec((None,) + keys.shape[1:], lambda i: (layer, 0, 0, 0))],
        out_specs=[pl.BlockSpec((tb, D_MODEL), lambda i: (i, 0)), fac(), fac(), fac()],
        out_shape=[jax.ShapeDtypeStruct((nt, D_MODEL), BF16), fshape, fshape, fshape],
        scratch_shapes=[pltpu.VMEM((2 * PEER_HEADS, PEER_KEYS, tb), F32)],
        compiler_params=_cparams(("parallel",)),
        name="peer_route",
    )(x1, mod, wq16, keys)


def _expert_kernel(h2_ref, u_ref, vt_ref, p0_ref, te_ref, e1_ref, x_ref, mod_ref, lg_ref, lb_ref,
                   o_ref, acc_scr, act_scr):
    c = pl.program_id(1)
    n_lane = h2_ref.shape[0] // PEER_KEYS

    @pl.when(c == 0)
    def _():
        acc_scr[...] = jnp.zeros_like(acc_scr)

    st = _dot_nt(u_ref[...], h2_ref[...])
    th = jnp.tanh(st * (_GELU_C0 + _GELU_C1 * (st * st)))
    act_scr[...] = st + st * th
    for pc in range(EC // EXP_PIECE):
        slabs = range(pc * EXP_PIECE // PEER_KEYS, (pc + 1) * EXP_PIECE // PEER_KEYS)
        g_lanes = []
        for ln in range(n_lane):
            lanes = slice(ln * PEER_KEYS, (ln + 1) * PEER_KEYS)
            w = [None] * len(slabs)
            for hd in range(PEER_HEADS):
                e1 = e1_ref[hd, :, lanes]
                for k, sl in enumerate(slabs):
                    te = te_ref[hd, 0, sl:sl + 1, lanes]
                    e0 = p0_ref[hd, 0, sl:sl + 1, lanes]
                    term = e0 * jnp.where(e1 > te, e1, 0.0)
                    w[k] = term if w[k] is None else w[k] + term
            g_lanes.append(jnp.concatenate(
                [(w[k] * act_scr[sl * PEER_KEYS:(sl + 1) * PEER_KEYS, lanes]).astype(BF16)
                 for k, sl in enumerate(slabs)], axis=0))
        g = jnp.concatenate(g_lanes, axis=1)
        acc_scr[...] += _dot(vt_ref[:, pc * EXP_PIECE:(pc + 1) * EXP_PIECE], g)

    @pl.when(c == pl.num_programs(1) - 1)
    def _():
        m = mod_ref[0]
        y = ALPHA * x_ref[...] + m[5:6] * acc_scr[...].T
        o_ref[...] = _layer_norm(y, lg_ref[...], lb_ref[...])


def _expert_call(h2, u16, vt16, layer, p0, te, e1, x1, mod, lg, lb, n_ctx, lat_len, tok0=0, n_tok=None):
    nt = x1.shape[0]
    n_tok = nt if n_tok is None else n_tok
    tb = TB_EXP
    t0 = tok0 // tb
    n_chunks = N_EXPERTS // EC
    cond = functools.partial(_cond_of_tile, tile=tb, n_ctx=n_ctx, lat_len=lat_len)
    p0r = p0.reshape(PEER_HEADS, n_chunks, N_SLAB, nt)
    ter = te.reshape(PEER_HEADS, n_chunks, N_SLAB, nt)
    slab = lambda: pl.BlockSpec((PEER_HEADS, 1, N_SLAB, tb), lambda i, c: (0, c, 0, t0 + i))
    full = lambda: pl.BlockSpec((PEER_HEADS, PEER_KEYS, tb), lambda i, c: (0, 0, t0 + i))
    const = lambda a: pl.BlockSpec(a.shape, lambda i, c: (0,) * a.ndim)
    return pl.pallas_call(
        _expert_kernel,
        grid=(n_tok // tb, n_chunks),
        in_specs=[pl.BlockSpec((tb, D_MODEL), lambda i, c: (t0 + i, 0)),
                  pl.BlockSpec((None, EC, D_MODEL), lambda i, c: (layer, c, 0)),
                  pl.BlockSpec((None, D_MODEL, EC), lambda i, c: (layer, 0, c)),
                  slab(), slab(), full(),
                  pl.BlockSpec((tb, D_MODEL), lambda i, c: (t0 + i, 0)),
                  pl.BlockSpec((1, 6, D_MODEL), lambda i, c: (cond(t0 + i), 0, 0)),
                  const(lg), const(lb)],
        out_specs=pl.BlockSpec((tb, D_MODEL), lambda i, c: (i, 0)),
        out_shape=jax.ShapeDtypeStruct((n_tok, D_MODEL), F32),
        scratch_shapes=[pltpu.VMEM((D_MODEL, tb), F32), pltpu.VMEM((EC, tb), F32)],
        compiler_params=_cparams(("parallel", "arbitrary")),
        name="peer_experts",
    )(h2, u16, vt16, p0r, ter, e1, x1, mod, lg, lb)


def _swap_halves(w, n_heads, rot):
    k = w.shape[0]
    w4 = w.reshape(k, n_heads, 2, rot // 2)
    return jnp.concatenate([w4[:, :, 1:2], w4[:, :, 0:1]], axis=2).reshape(k, n_heads * rot)


def _inproj_weight(w):
    a = w[:, 0:512]
    qb, kb, vb = w[:, 512:768], w[:, 768:1024], w[:, 1024:1280]
    cq, ckv, kr = w[:, 1280:1536], w[:, 1536:1664], w[:, 1664:1696]
    qd, kd, vd = w[:, 1696:1952], w[:, 1952:2080], w[:, 2080:2208]
    rep = lambda m: jnp.repeat(m.reshape(D_MODEL, 2, HEAD_DIM), 2, axis=1).reshape(D_MODEL, 256)
    kdr = rep(kd)
    cols = [a, qb, kb, vb, cq, ckv, _pair_cols(None, [kr, kr]), _pair_cols(None, [_swap_halves(kr, 1, MLA_ROPE)] * 2),
            qd, _swap_halves(qd, N_HEADS, HEAD_DIM), kdr, _swap_halves(kdr, N_HEADS, HEAD_DIM), rep(vd)]
    return jnp.concatenate(cols, axis=1).astype(BF16)


def _pair_cols(nope, rope):
    k = rope[0].shape[0]
    zeros = lambda w: jnp.zeros((k, w), rope[0].dtype)
    nope = nope if nope is not None else [zeros(MLA_NOPE)] * 2
    rope = rope if rope is not None else [zeros(MLA_ROPE)] * 2
    return jnp.concatenate([nope[0], rope[0], nope[1], rope[1], zeros(256 - 2 * PAIR_W)], axis=1)


def _mla_weights(w_uq, w_ukv):
    q3 = w_uq.reshape(MLA_Q_RANK, N_HEADS, PAIR_W)
    nope = [q3[:, hd, :MLA_NOPE] for hd in range(N_HEADS)]
    rope = [q3[:, hd, MLA_NOPE:] for hd in range(N_HEADS)]
    rope_sw = [_swap_halves(r, 1, MLA_ROPE) for r in rope]
    none64 = [jnp.zeros_like(nope[0])] * 2
    wuq = jnp.concatenate([_pair_cols(nope[0:2], rope[0:2]), _pair_cols(nope[2:4], rope[2:4]),
                           _pair_cols(none64, rope_sw[0:2]), _pair_cols(none64, rope_sw[2:4])], axis=1).astype(BF16)
    kv3 = w_ukv.reshape(MLA_KV_RANK, N_HEADS, MLA_NOPE + 64)
    knope = [kv3[:, hd, :MLA_NOPE] for hd in range(N_HEADS)]
    none32 = [jnp.zeros((MLA_KV_RANK, MLA_ROPE), w_ukv.dtype)] * 2
    wukv = jnp.concatenate([_pair_cols(knope[0:2], none32), _pair_cols(knope[2:4], none32),
                            kv3[:, :, MLA_NOPE:].reshape(MLA_KV_RANK, 256)], axis=1).astype(BF16)
    return wuq, wukv


def _pair_rope_tables(lat_len, tile):
    cos_t, sin_t = _rope_tables(lat_len, MLA_ROPE, tile)
    cos32, sin32 = cos_t[:, :MLA_ROPE], sin_t[:, :MLA_ROPE]
    n = cos_t.shape[0]
    one, zero = jnp.ones((n, MLA_NOPE), F32), jnp.zeros((n, MLA_NOPE), F32)
    return (jnp.concatenate([one, cos32, one, cos32, one], axis=1),
            jnp.concatenate([zero, sin32, zero, sin32, zero], axis=1))


def _rope_tables(lat_len, rot, tile):
    t = jnp.arange(lat_len)
    row = (t // GRID_W).astype(F32)
    col = (t % GRID_W).astype(F32)
    nf = rot // 4
    freqs = ROPE_BASE ** (-jnp.arange(nf, dtype=F32) / nf)
    ang = jnp.concatenate([row[:, None] * freqs, col[:, None] * freqs], -1)
    cos, sin = jnp.cos(ang), jnp.sin(ang)
    cos_t = jnp.tile(jnp.concatenate([cos, cos], -1), (1, N_HEADS))
    sin_t = jnp.tile(jnp.concatenate([-sin, sin], -1), (1, N_HEADS))
    w = N_HEADS * rot
    return (jnp.concatenate([jnp.ones((tile, w), F32), cos_t], 0),
            jnp.concatenate([jnp.zeros((tile, w), F32), sin_t], 0))


def _na_bias_tables(rpb, rows):
    kh = min(NA_KH, rows)
    qrow = np.array([0, 1, 2, 3, rows // 2, rows - 3, rows - 2, rows - 1])
    start = np.clip(qrow - kh // 2, 0, rows - kh)
    dr0 = start - qrow + NA_KH - 1
    qc = np.arange(GRID_W)
    kc = np.arange(GRID_W)
    cstart = np.clip(qc - NA_KW // 2, 0, GRID_W - NA_KW)
    ok = (kc[None, :] >= cstart[:, None]) & (kc[None, :] < cstart[:, None] + NA_KW)
    n_h, n_a, n_c = rpb.shape
    edge = GRID_W - NA_KW
    w = jnp.concatenate([jnp.broadcast_to(rpb[:, :, :1], (n_h, n_a, edge)), rpb,
                         jnp.broadcast_to(rpb[:, :, -1:], (n_h, n_a, edge + 1))], axis=-1)
    skew = jnp.tile(w, (1, 1, GRID_W))[:, :, :GRID_W * (2 * GRID_W - 1)].reshape(n_h, n_a, GRID_W, 2 * GRID_W - 1)
    toep = skew[:, :, :, GRID_W - 1:]
    tabs = []
    for d0 in dr0:
        b = jnp.where(ok[None, None], toep[:, d0:d0 + kh], NEG_INF)
        tabs.append(b.transpose(0, 2, 1, 3).reshape(N_HEADS * GRID_W, kh * GRID_W))
    return jnp.stack(tabs)


def _swa_bias_tables(lat_len):
    nb = lat_len // SWA_WIN
    qi = np.arange(SWA_WIN)
    kj = np.arange(3 * SWA_WIN)
    tabs = []
    for n in (0, 1, nb - 1):
        kpos = _swa_window_start(n, nb) + kj
        in_win = np.abs(kpos[None, :] - (n * SWA_WIN + qi[:, None])) <= SWA_WIN
        tabs.append(np.where(in_win, 0.0, NEG_INF))
    return jnp.asarray(np.stack(tabs), F32)


def _swa_window_start(n, nb):
    lo = n - 1
    lo = jnp.clip(lo, 0, nb - 3) if isinstance(n, jax.Array) else min(max(lo, 0), nb - 3)
    return lo * SWA_WIN


def _heads_to_lanes(t):
    b, h, s, d = t.shape
    return t.transpose(0, 2, 1, 3).reshape(b, s, h * d)


def _lanes_to_heads(t, b, s, h):
    return t.reshape(b, s, h, -1).transpose(0, 2, 1, 3)


def kernel(x_prompt, x_sample, cache_nat_k, cache_nat_v, cache_mla_ckv, cache_mla_krope, cache_swa_k, cache_swa_v, c, c_ctx, w_in, w_out, out_norm_g, w_ada, b_ada, ln1_g, ln1_b, ln2_g, ln2_b, a_norm_g, a_norm_b, a_w_s, a_b_s, nat_rpb, mla_q_norm_g, mla_w_uq, mla_kv_norm_g, mla_w_ukv, swa_sinks, peer_w_q, peer_sub_keys, peer_u, peer_v):
    n_b, seq, d = x_prompt.shape
    n_db, lat_len, _ = x_sample.shape
    past = cache_nat_k.shape[3]
    n_ctx = n_b * seq
    nt = n_ctx + n_db * lat_len
    rows = lat_len // GRID_W

    x = jnp.concatenate([x_prompt.reshape(n_ctx, d), x_sample.reshape(n_db * lat_len, d)], axis=0)
    conds = jnp.zeros((8, d), F32).at[0].set(c_ctx).at[1:1 + n_db].set(c)
    mods = _ada_call(conds, w_ada, b_ada).reshape(DEPTH, 8, 6, d)

    cos_c, sin_c = _pair_rope_tables(lat_len, TM)
    cos_d, sin_d = _rope_tables(lat_len, HEAD_DIM, TM)
    swa_bias = _swa_bias_tables(lat_len)
    nb_swa = lat_len // SWA_WIN

    q_ctx = lambda b, j: b
    lat_q = lambda tq: (lambda b, j: (n_ctx + b * lat_len) // tq + j)
    ctx_keys = lambda w: pl.BlockSpec((seq, w), lambda b, j: (b, 0))
    lat_keys = lambda w: pl.BlockSpec((lat_len, w), lambda b, j: (n_ctx // lat_len + b, 0))

    wq16 = peer_w_q.astype(BF16)
    u16 = peer_u.astype(BF16)
    vt16 = jnp.swapaxes(peer_v, 1, 2).astype(BF16)

    states = [[] for _ in range(6)]
    for l in range(DEPTH):
        mod = mods[l]
        wbig = _inproj_weight(w_in[l])
        wuq, wukv = _mla_weights(mla_w_uq[l], mla_w_ukv[l])
        mixer_a = (a_norm_g[l][None], a_norm_b[l][None], a_w_s[l].astype(BF16),
                   jnp.repeat(a_b_s[l].T, HEAD_DIM, axis=1))
        (oa, qb, kb, vb, kb16, vb16, qp0, qp1, ckvn, krp, kp0, kp1, vc,
         qd, kd, vd, kd16, vd16) = _inproj_call(
            x, mod, wbig, wuq, wukv, mla_q_norm_g[l][None], mla_kv_norm_g[l][None],
            (cos_c, sin_c, cos_d, sin_d), mixer_a, n_ctx, lat_len)

        ob_c = _attn_call("ctx_attn_b", (n_b, 1), seq, seq, HEAD_DIM ** -0.5, q_ctx, qb, kb16, vb16, ctx_keys)
        oc_c = _attn_call("ctx_attn_c", (n_b, 1), seq, seq, MLA_SCALE, q_ctx, qp0, kp0, vc, ctx_keys,
                          q2=qp1, k2=kp1)
        od_c = _attn_call("ctx_attn_d", (n_b, 1), seq, seq, HEAD_DIM ** -0.5, q_ctx, qd, kd16, vd16, ctx_keys,
                          sinks=swa_sinks[l])

        na_bias = _na_bias_tables(nat_rpb[l], rows)
        kh = min(NA_KH, rows)
        ob_l = _attn_call(
            "lat_attn_b", (n_db, rows // NA_ROWS_PER_STEP), GRID_W, kh * GRID_W, HEAD_DIM ** -0.5,
            lat_q(NA_ROWS_PER_STEP * GRID_W), qb, kb16, vb16, lat_keys, n_sub=NA_ROWS_PER_STEP,
            extra=(_heads_to_lanes(cache_nat_k[:, l]).astype(BF16), _heads_to_lanes(cache_nat_v[:, l]).astype(BF16)),
            bias=na_bias,
            bias_index=lambda r: jnp.where(r < 4, r, jnp.where(r > rows - 4, r - (rows - 8), 4)),
            start_fn=lambda r: jnp.clip(r - kh // 2, 0, rows - kh) * GRID_W)
        krx = cache_mla_krope[:, l].reshape(n_db * past, MLA_ROPE)
        kx0, kx1, vx = _kvexp_call(cache_mla_ckv[:, l].reshape(n_db * past, MLA_KV_RANK),
                                   _pair_cols(None, [krx, krx]), wukv)
        oc_l = _attn_call(
            "lat_attn_c", (n_db, lat_len // 256), 256, lat_len, MLA_SCALE, lat_q(256), qp0, kp0, vc, lat_keys,
            q2=qp1, k2=kp1,
            extra=tuple(t.reshape(n_db, past, 256) for t in (kx0, kx1, vx)))
        rep_kv = lambda t: jnp.repeat(t, 2, axis=1)
        od_l = _attn_call(
            "lat_attn_d", (n_db, nb_swa // SWA_BLOCKS_PER_STEP), SWA_WIN, 3 * SWA_WIN, HEAD_DIM ** -0.5,
            lat_q(SWA_BLOCKS_PER_STEP * SWA_WIN), qd, kd16, vd16, lat_keys, n_sub=SWA_BLOCKS_PER_STEP,
            extra=(_heads_to_lanes(rep_kv(cache_swa_k[:, l])).astype(BF16),
                   _heads_to_lanes(rep_kv(cache_swa_v[:, l])).astype(BF16)),
            bias=swa_bias,
            bias_index=lambda n: jnp.where(n == 0, 0, jnp.where(n == nb_swa - 1, 2, 1)),
            sinks=swa_sinks[l],
            start_fn=lambda n: _swa_window_start(n, nb_swa))
        ob, oc, od = (ob_c, ob_l), (oc_c, oc_l), (od_c, od_l)

        x1 = _merge_call(oa, ob, oc, od, x, mod, out_norm_g[l][None], w_out[l].astype(BF16),
                         ln1_g[l][None], ln1_b[l][None], n_ctx, lat_len)
        h2, p0, te, e1 = _route_call(x1, mod, wq16, peer_sub_keys, l, n_ctx, lat_len)
        experts = functools.partial(_expert_call, h2, u16, vt16, l, p0, te, e1, x1, mod,
                                    ln2_g[l][None], ln2_b[l][None], n_ctx, lat_len)
        if l + 1 < DEPTH:
            x = experts()
        else:
            y_prompt = experts(tok0=0, n_tok=n_ctx).reshape(n_b, seq, d)
            y_sample = experts(tok0=n_ctx, n_tok=n_db * lat_len).reshape(n_db, lat_len, d)

        states[0].append(_lanes_to_heads(kb[:n_ctx], n_b, seq, N_HEADS))
        states[1].append(_lanes_to_heads(vb[:n_ctx], n_b, seq, N_HEADS))
        states[2].append(ckvn[:n_ctx].reshape(n_b, seq, MLA_KV_RANK))
        states[3].append(krp[:n_ctx, MLA_NOPE:PAIR_W].reshape(n_b, seq, MLA_ROPE))
        states[4].append(_lanes_to_heads(kd[:n_ctx], n_b, seq, N_HEADS)[:, ::2])
        states[5].append(_lanes_to_heads(vd[:n_ctx], n_b, seq, N_HEADS)[:, ::2])

    return (y_prompt, y_sample) + tuple(jnp.stack(s, axis=1) for s in states)
```

```python
import functools
import math

import jax
import jax.numpy as jnp
import numpy as np
from jax import lax
from jax.experimental import pallas as pl
from jax.experimental.pallas import tpu as pltpu

F32 = jnp.float32
BF16 = jnp.bfloat16

D_MODEL = 1024
DEPTH = 2
GRID_W = 64
HEAD_DIM = 64
N_HEADS = 4
GROUP_W = 256
CHUNK = 128
NA_KH = 8
NA_KW = 16
MLA_Q_RANK = 256
MLA_KV_RANK = 128
MLA_NOPE = 64
MLA_ROPE = 32
MLA_SCALE = (MLA_NOPE + MLA_ROPE) ** -0.5
SWA_WIN = 128
PEER_HEADS = 8
PEER_KEYS = 128
PEER_TOPK = 16
N_EXPERTS = PEER_KEYS * PEER_KEYS
ROPE_BASE = 10000.0
LN_EPS = 1e-5
NEG_INF = -1e30
ALPHA = (2 * DEPTH) ** 0.25

V7X_VMEM_LIMIT_BYTES = 56 * 1024 * 1024
TM = 256
TB_ROUTE = 512
TB_EXP = 1024
ATTN_KEY_CHUNK = 1024
SINK_PAD = 128
NA_ROWS_PER_STEP = 8
SWA_BLOCKS_PER_STEP = 4
EC = 1024
N_SLAB = EC // PEER_KEYS
EXP_PIECE = 256
_C_A = 0
_C_QB, _C_KB, _C_VB = 512, 768, 1024
_C_CQ, _C_CKV, _C_KR, _C_KRS = 1280, 1536, 1664, 1920
_C_QD, _C_QDS, _C_KD, _C_KDS, _C_VD = 2176, 2432, 2688, 2944, 3200
_C_END = 3456
PAIR_W = MLA_NOPE + MLA_ROPE


def _cparams(sem):
    return pltpu.CompilerParams(dimension_semantics=sem, vmem_limit_bytes=V7X_VMEM_LIMIT_BYTES)


def _dot(a, b):
    return jnp.dot(a, b, preferred_element_type=F32)


def _dot_nt(a, b):
    return lax.dot_general(a, b, (((1,), (1,)), ((), ())), preferred_element_type=F32)


def _layer_norm(x, g, b):
    mu = jnp.mean(x, axis=-1, keepdims=True)
    xc = x - mu
    var = jnp.mean(xc * xc, axis=-1, keepdims=True)
    return xc * lax.rsqrt(var + LN_EPS) * g + b


def _rms_norm(x, g):
    return x * lax.rsqrt(jnp.mean(x * x, axis=-1, keepdims=True) + LN_EPS) * g


def _ada_kernel(c_ref, w_ref, b_ref, o_ref):
    c = c_ref[...]
    a = c * jax.nn.sigmoid(c)
    a_hi = a.astype(BF16)
    a_lo = (a - a_hi.astype(F32)).astype(BF16)
    w = w_ref[0]
    w_hi = w.astype(BF16)
    w_lo = (w - w_hi.astype(F32)).astype(BF16)
    o_ref[0] = _dot(a_hi, w_hi) + _dot(a_hi, w_lo) + _dot(a_lo, w_hi) + b_ref[0]


def _ada_call(conds, w_ada, b_ada):
    tn = 1536
    n = w_ada.shape[-1]
    return pl.pallas_call(
        _ada_kernel,
        grid=(DEPTH, n // tn),
        in_specs=[pl.BlockSpec((8, D_MODEL), lambda l, j: (0, 0)),
                  pl.BlockSpec((1, D_MODEL, tn), lambda l, j: (l, 0, j)),
                  pl.BlockSpec((1, 1, tn), lambda l, j: (l, 0, j))],
        out_specs=pl.BlockSpec((1, 8, tn), lambda l, j: (l, 0, j)),
        out_shape=jax.ShapeDtypeStruct((DEPTH, 8, n), F32),
        compiler_params=_cparams(("parallel", "parallel")),
        name="ada_mod",
    )(conds, w_ada, b_ada.reshape(DEPTH, 1, n))


def _token_rows(x, n_ctx):
    nct = n_ctx // TM
    if isinstance(x, tuple):
        lat_block = lambda i: jnp.maximum(i - nct, 0)
    else:
        x, lat_block = (x, x), (lambda i: jnp.maximum(i, nct))
    d = x[0].shape[1]
    return x, [pl.BlockSpec((TM, d), lambda i: (jnp.minimum(i, nct - 1), 0)),
               pl.BlockSpec((TM, d), lambda i: (lat_block(i), 0))]


def _inproj_kernel(xc_ref, xl_ref, mod_ref, w_ref, wuq_ref, wukv_ref, gq_ref, gkv_ref,
                   cosc_ref, sinc_ref, cosd_ref, sind_ref, ag_ref, ab_ref, aws_ref, abs_ref,
                   oa_ref, qb_ref, kb_ref, vb_ref, kb16_ref, vb16_ref,
                   qp0_ref, qp1_ref, ckvn_ref, krp_ref, kp0_ref, kp1_ref, vc_ref,
                   qd_ref, kd_ref, vd_ref, kd16_ref, vd16_ref, *, n_ctx_tiles):
    m = mod_ref[0]
    x = jnp.where(pl.program_id(0) < n_ctx_tiles, xc_ref[...], xl_ref[...])
    h = x * (1.0 + m[1:2]) + m[0:1]
    z = _dot(h.astype(BF16), w_ref[...])
    cosc, sinc = cosc_ref[...], sinc_ref[...]
    cosd, sind = cosd_ref[...], sind_ref[...]

    for c0 in range(0, z.shape[0], CHUNK):
        oa_ref[c0:c0 + CHUNK, :] = _chunk_mlp(z[c0:c0 + CHUNK, _C_A:_C_QB], ag_ref[...], ab_ref[...], aws_ref,
                                              abs_ref[...])
    qb_ref[...] = z[:, _C_QB:_C_KB].astype(BF16)
    kb = z[:, _C_KB:_C_VB]
    vb = z[:, _C_VB:_C_CQ]
    kb_ref[...] = kb
    vb_ref[...] = vb
    kb16_ref[...] = kb.astype(BF16)
    vb16_ref[...] = vb.astype(BF16)

    cqn = _rms_norm(z[:, _C_CQ:_C_CKV], gq_ref[...])
    q = _dot(cqn.astype(BF16), wuq_ref[...])
    qp0_ref[...] = (q[:, 0:256] * cosc + q[:, 512:768] * sinc).astype(BF16)
    qp1_ref[...] = (q[:, 256:512] * cosc + q[:, 768:1024] * sinc).astype(BF16)
    ckvn = _rms_norm(z[:, _C_CKV:_C_KR], gkv_ref[...])
    ckvn_ref[...] = ckvn
    kv = _dot(ckvn.astype(BF16), wukv_ref[...])
    kr = z[:, _C_KR:_C_KRS] * cosc + z[:, _C_KRS:_C_QD] * sinc
    krp_ref[...] = kr
    kp0_ref[...] = (kv[:, 0:256] + kr).astype(BF16)
    kp1_ref[...] = (kv[:, 256:512] + kr).astype(BF16)
    vc_ref[...] = kv[:, 512:768].astype(BF16)

    qd_ref[...] = (z[:, _C_QD:_C_QDS] * cosd + z[:, _C_QDS:_C_KD] * sind).astype(BF16)
    kd = z[:, _C_KD:_C_KDS] * cosd + z[:, _C_KDS:_C_VD] * sind
    vd = z[:, _C_VD:_C_END]
    kd_ref[...] = kd
    vd_ref[...] = vd
    kd16_ref[...] = kd.astype(BF16)
    vd16_ref[...] = vd.astype(BF16)


def _cond_of_tile(i, tile, n_ctx, lat_len):
    n_ctx_tiles = n_ctx // tile
    return jnp.where(i < n_ctx_tiles, 0, 1 + (i - n_ctx_tiles) // (lat_len // tile))


def _rope_block_of_tile(i, tile, n_ctx, lat_len):
    n_ctx_tiles = n_ctx // tile
    return jnp.where(i < n_ctx_tiles, 0, 1 + (i - n_ctx_tiles) % (lat_len // tile))


def _inproj_call(x, mod, wbig, wuq, wukv, gq, gkv, tabs, mixer_a, n_ctx, lat_len):
    nt = sum(a.shape[0] for a in x) if isinstance(x, tuple) else x.shape[0]
    xs, x_specs = _token_rows(x, n_ctx)
    cond = functools.partial(_cond_of_tile, tile=TM, n_ctx=n_ctx, lat_len=lat_len)
    rblk = functools.partial(_rope_block_of_tile, tile=TM, n_ctx=n_ctx, lat_len=lat_len)
    row = lambda w: pl.BlockSpec((TM, w), lambda i: (i, 0))
    const = lambda a: pl.BlockSpec(a.shape, lambda i: (0,) * a.ndim)
    tab = lambda w: pl.BlockSpec((TM, w), lambda i: (rblk(i), 0))
    outs = [(256, F32), (256, BF16), (256, F32), (256, F32), (256, BF16), (256, BF16),
            (256, BF16), (256, BF16), (128, F32), (256, F32), (256, BF16), (256, BF16), (256, BF16),
            (256, BF16), (256, F32), (256, F32), (256, BF16), (256, BF16)]
    return pl.pallas_call(
        functools.partial(_inproj_kernel, n_ctx_tiles=n_ctx // TM),
        grid=(nt // TM,),
        in_specs=x_specs + [
                  pl.BlockSpec((1, 6, D_MODEL), lambda i: (cond(i), 0, 0)),
                  const(wbig), const(wuq), const(wukv), const(gq), const(gkv),
                  tab(256), tab(256), tab(256), tab(256)] + [const(a) for a in mixer_a],
        out_specs=[row(w) for w, _ in outs],
        out_shape=[jax.ShapeDtypeStruct((nt, w), dt) for w, dt in outs],
        compiler_params=_cparams(("parallel",)),
        name="inproj",
    )(*xs, mod, wbig, wuq, wukv, gq, gkv, *tabs, *mixer_a)


def _kvexp_kernel(c_ref, kr_ref, w_ref, k0_ref, k1_ref, v_ref):
    kv = _dot(c_ref[...].astype(BF16), w_ref[...])
    kr = kr_ref[...]
    k0_ref[...] = (kv[:, 0:256] + kr).astype(BF16)
    k1_ref[...] = (kv[:, 256:512] + kr).astype(BF16)
    v_ref[...] = kv[:, 512:768].astype(BF16)


def _kvexp_call(ckv, kr_pair, wukv):
    n = ckv.shape[0]
    whole = lambda a: pl.BlockSpec(a.shape, lambda i: (0, 0))
    return pl.pallas_call(
        _kvexp_kernel,
        grid=(1,),
        in_specs=[whole(ckv), whole(kr_pair), whole(wukv)],
        out_specs=[pl.BlockSpec((n, 256), lambda i: (0, 0))] * 3,
        out_shape=[jax.ShapeDtypeStruct((n, 256), BF16)] * 3,
        compiler_params=_cparams(("arbitrary",)),
        name="mla_cache_expand",
    )(ckv, kr_pair, wukv)


def _chunk_mlp(z, gain, bias, ws_ref, bs):
    g = jax.nn.gelu(z)
    u = g[:, 0:GROUP_W]
    v = _layer_norm(g[:, GROUP_W:2 * GROUP_W], gain, bias).astype(BF16)
    lane_head = lax.broadcasted_iota(jnp.int32, (1, GROUP_W), 1) // HEAD_DIM
    mixed = bs
    for hd in range(N_HEADS):
        mixed = mixed + jnp.where(lane_head == hd, _dot(ws_ref[hd], v), 0.0)
    return u * mixed


def _attn_kernel(*refs, tq, n_sub, **static):
    for sb in range(n_sub):
        _attn_tile(refs, pl.program_id(1) * n_sub + sb, slice(sb * tq, (sb + 1) * tq), tq=tq, **static)


def _attn_tile(refs, tile, q_rows, *, tq, wk, kc, scale, has_q2, has_extra, bias_heads, has_sink, start_fn, bias_index):
    refs = list(refs)
    xbias_ref = refs.pop(0) if has_sink else None
    q1_ref = refs.pop(0)
    q2_ref = refs.pop(0) if has_q2 else None
    k1_ref = refs.pop(0)
    k2_ref = refs.pop(0) if has_q2 else None
    v_ref = refs.pop(0)
    if has_extra:
        xk1_ref = refs.pop(0)
        xk2_ref = refs.pop(0) if has_q2 else None
        xv_ref = refs.pop(0)
    bias_ref = refs.pop(0) if bias_heads else None
    o_ref = refs.pop(0)

    def stack_heads(q, width, n):
        lane_head = lax.broadcasted_iota(jnp.int32, (1, q.shape[1]), 1) // width
        return jnp.concatenate([jnp.where(lane_head == hd, q, jnp.zeros_like(q)) for hd in range(n)], axis=0)

    if has_q2:
        q1s = stack_heads(q1_ref[q_rows, :], PAIR_W, 2)
        q2s = stack_heads(q2_ref[q_rows, :], PAIR_W, 2)
    else:
        q1s = stack_heads(q1_ref[q_rows, :], HEAD_DIM, N_HEADS)

    start = start_fn(tile)
    if not isinstance(start, int):
        start = pl.multiple_of(start, 64)
    bias_tile = bias_ref.at[bias_index(tile)] if bias_heads else None

    chunks = [("win", c0, min(kc, wk - c0)) for c0 in range(0, wk, kc)]
    if has_extra:
        chunks.append(("extra", 0, 0))
    m = denom = o = None
    for kind, c0, n in chunks:
        if kind == "win":
            rows = pl.ds(start if c0 == 0 else start + c0, n)
            k1c, vc = k1_ref[rows, :], v_ref[rows, :]
            k2c = k2_ref[rows, :] if has_q2 else None
        else:
            k1c, vc = xk1_ref[...], xv_ref[...]
            k2c = xk2_ref[...] if has_q2 else None
        s = _dot_nt(q1s, k1c)
        if has_q2:
            s = jnp.concatenate([s, _dot_nt(q2s, k2c)], axis=0)
        s = s * scale
        if kind == "win" and bias_heads == N_HEADS:
            s = s + bias_tile[:, c0:c0 + n]
        elif kind == "win" and bias_heads == 1:
            s = s + jnp.concatenate([bias_tile[:, c0:c0 + n]] * N_HEADS, axis=0)
        elif kind == "extra" and has_sink:
            s = s + xbias_ref[...]
        mc = jnp.max(s, axis=-1, keepdims=True)
        m_new = mc if m is None else jnp.maximum(m, mc)
        p = jnp.exp(s - m_new)
        pv = _dot(p.astype(BF16), vc)
        if m is None:
            denom, o = jnp.sum(p, axis=-1, keepdims=True), pv
        else:
            alpha = jnp.exp(m - m_new)
            denom = alpha * denom + jnp.sum(p, axis=-1, keepdims=True)
            o = alpha * o + pv
        m = m_new
    o = o / denom
    lane_head = lax.broadcasted_iota(jnp.int32, (1, N_HEADS * HEAD_DIM), 1) // HEAD_DIM
    out = jnp.zeros((tq, N_HEADS * HEAD_DIM), F32)
    for hd in range(N_HEADS):
        out = out + jnp.where(lane_head == hd, o[hd * tq:(hd + 1) * tq], 0.0)
    o_ref[q_rows, :] = out


def _attn_call(name, grid, tq, wk, scale, q_index, q1, k1, v, k_spec_fn, *, q2=None, k2=None, extra=None,
               bias=None, bias_index=None, sinks=None, start_fn=lambda t: 0, kc=ATTN_KEY_CHUNK, n_sub=1):
    tb = n_sub * tq
    has_q2 = q2 is not None
    bias_heads = 0 if bias is None else bias.shape[1] // tq
    args, specs = [], []
    extra_index = lambda b, j: (b, 0, 0)
    if sinks is not None:
        assert not has_q2
        if extra is None:
            extra = tuple(jnp.zeros((1, 0, 256), BF16) for _ in range(2))
            extra_index = lambda b, j: (0, 0, 0)
        n_real = extra[0].shape[1]
        extra = tuple(jnp.pad(a, ((0, 0), (0, SINK_PAD), (0, 0))) for a in extra)
        col = jnp.arange(n_real + SINK_PAD)[None, :]
        sink_rows = jnp.repeat(sinks.astype(F32), tq)[:, None]
        xbias = jnp.where(col < n_real, 0.0, jnp.where(col == n_real, sink_rows, NEG_INF))
        args.append(xbias)
        specs.append(pl.BlockSpec(xbias.shape, lambda b, j: (0, 0)))
    has_extra = extra is not None
    args.append(q1)
    specs.append(pl.BlockSpec((tb, 256), lambda b, j: (q_index(b, j), 0)))
    if has_q2:
        args.append(q2)
        specs.append(pl.BlockSpec((tb, 256), lambda b, j: (q_index(b, j), 0)))
    args.append(k1)
    specs.append(k_spec_fn(256))
    if has_q2:
        args.append(k2)
        specs.append(k_spec_fn(256))
    args.append(v)
    specs.append(k_spec_fn(256))
    if has_extra:
        for a in extra:
            args.append(a)
            specs.append(pl.BlockSpec((None,) + a.shape[1:], extra_index))
    if bias is not None:
        args.append(bias)
        specs.append(pl.BlockSpec(bias.shape, lambda b, j: (0, 0, 0)))
    kern = functools.partial(_attn_kernel, tq=tq, n_sub=n_sub, wk=wk, kc=kc, scale=scale, has_q2=has_q2,
                             has_extra=has_extra, bias_heads=bias_heads, has_sink=sinks is not None,
                             start_fn=start_fn, bias_index=bias_index)
    return pl.pallas_call(
        kern,
        grid=grid,
        in_specs=specs,
        out_specs=pl.BlockSpec((tb, 256), lambda b, j: (b * grid[1] + j, 0)),
        out_shape=jax.ShapeDtypeStruct((grid[0] * grid[1] * tb, 256), F32),
        compiler_params=_cparams(("parallel", "parallel")),
        name=name,
    )(*args)


def _merge_kernel(oa_ref, obc_ref, obl_ref, occ_ref, ocl_ref, odc_ref, odl_ref, xc_ref, xl_ref, mod_ref, g_ref, w_ref,
                  lg_ref, lb_ref, o_ref, *, n_ctx_tiles):
    m = mod_ref[0]
    is_ctx = pl.program_id(0) < n_ctx_tiles
    pick = lambda c_ref, l_ref: jnp.where(is_ctx, c_ref[...], l_ref[...])
    groups = (oa_ref[...], pick(obc_ref, obl_ref), pick(occ_ref, ocl_ref), pick(odc_ref, odl_ref))
    acc = None
    for gi, o in enumerate(groups):
        og = _rms_norm(o, g_ref[:, gi * GROUP_W:(gi + 1) * GROUP_W]).astype(BF16)
        part = _dot(og, w_ref[gi * GROUP_W:(gi + 1) * GROUP_W, :])
        acc = part if acc is None else acc + part
    y = ALPHA * pick(xc_ref, xl_ref) + m[2:3] * acc
    o_ref[...] = _layer_norm(y, lg_ref[...], lb_ref[...])


def _merge_call(oa, ob, oc, od, x, mod, gout, wout16, lg, lb, n_ctx, lat_len):
    nt = sum(a.shape[0] for a in x) if isinstance(x, tuple) else x.shape[0]
    xs, x_specs = _token_rows(x, n_ctx)
    nct = n_ctx // TM
    cond = functools.partial(_cond_of_tile, tile=TM, n_ctx=n_ctx, lat_len=lat_len)
    row = lambda w: pl.BlockSpec((TM, w), lambda i: (i, 0))
    ctx_row = lambda: pl.BlockSpec((TM, 256), lambda i: (jnp.minimum(i, nct - 1), 0))
    lat_row = lambda: pl.BlockSpec((TM, 256), lambda i: (jnp.maximum(i - nct, 0), 0))
    const = lambda a: pl.BlockSpec(a.shape, lambda i: (0,) * a.ndim)
    return pl.pallas_call(
        functools.partial(_merge_kernel, n_ctx_tiles=nct),
        grid=(nt // TM,),
        in_specs=[row(256), ctx_row(), lat_row(), ctx_row(), lat_row(), ctx_row(), lat_row()] + x_specs + [
                  pl.BlockSpec((1, 6, D_MODEL), lambda i: (cond(i), 0, 0)),
                  const(gout), const(wout16), const(lg), const(lb)],
        out_specs=row(D_MODEL),
        out_shape=jax.ShapeDtypeStruct((nt, D_MODEL), F32),
        compiler_params=_cparams(("parallel",)),
        name="merge_out",
    )(oa, *ob, *oc, *od, *xs, mod, gout, wout16, lg, lb)


_N_TOP = PEER_TOPK + 1
_GELU_C0 = math.sqrt(2.0 / math.pi)
_GELU_C1 = 0.044715 * _GELU_C0


def _sort_network(n):
    pairs = []
    p = 1
    while p < n:
        k = p
        while k >= 1:
            for j in range(k % p, n - k, 2 * k):
                for i in range(min(k, n - j - k)):
                    if (i + j) // (2 * p) == (i + j + k) // (2 * p):
                        pairs.append((i + j, i + j + k))
            k //= 2
        p *= 2
    return pairs


def _merge_top(levels, n_top):
    levels = list(levels)
    sub = lax.broadcasted_iota(jnp.int32, levels[0].shape, 0).astype(F32)
    out = []
    for k in range(n_top):
        head = levels[0]
        m = jnp.max(head, axis=0, keepdims=True)
        out.append(m)
        first = jnp.min(jnp.where(head == m, sub, 8.0), axis=0, keepdims=True)
        pop = sub == first
        for v in range(n_top - 1 - k):
            nxt = levels[v + 1] if v + 1 < len(levels) else -jnp.inf
            levels[v] = jnp.where(pop, nxt, levels[v])
    return out


def _top_values(s, n_top):
    g = [s[8 * v:8 * v + 8] for v in range(s.shape[0] // 8)]
    for i, j in _sort_network(len(g)):
        g[i], g[j] = jnp.maximum(g[i], g[j]), jnp.minimum(g[i], g[j])
    return _merge_top(g, n_top)


_CAND_LEVELS = (17, 8, 5, 4, 13, 4, 1, 0)


def _candidate_levels(sv0, sv1):
    shape = (8,) + sv0[0].shape[1:]
    row = lax.broadcasted_iota(jnp.int32, shape, 0)
    pick = lambda vals, default: functools.reduce(
        lambda acc, rv: jnp.where(row == rv[0], rv[1], acc), vals, jnp.full(shape, default, F32))
    fixed0 = pick([(r, sv0[r]) for r in range(4)], 0.0)
    fixed1 = pick([(4 + r, sv1[r]) for r in range(3)], 0.0)
    n_valid = pick([(r, float(n)) for r, n in enumerate(_CAND_LEVELS)], 0.0)
    levels = []
    for v in range(max(_CAND_LEVELS)):
        moving0 = sv0[min(4 + v, len(sv0) - 1)]
        lv = jnp.where(row < 4, fixed0 + sv1[v], moving0 + fixed1)
        levels.append(jnp.where(n_valid > float(v), lv, -jnp.inf))
    return levels


def _route_kernel(x_ref, mod_ref, wq_ref, keys_ref, h2_ref, p0_ref, te_ref, e1_ref, st_scr):
    m = mod_ref[0]
    h2 = (x_ref[...] * (1.0 + m[4:5]) + m[3:4]).astype(BF16)
    h2_ref[...] = h2
    q = _dot(h2, wq_ref[...])
    tb = q.shape[0]
    for p in range(2):
        kp = keys_ref[p]
        k_hi = kp.astype(BF16)
        k_lo = (kp - k_hi.astype(F32)).astype(BF16)
        for hd in range(PEER_HEADS):
            c0 = (hd * 2 + p) * PEER_KEYS
            qs = q[:, c0:c0 + PEER_KEYS]
            q_hi = qs.astype(BF16)
            q_lo = (qs - q_hi.astype(F32)).astype(BF16)
            st_scr[hd * 2 + p] = _dot_nt(k_hi, q_hi) + _dot_nt(k_lo, q_hi) + _dot_nt(k_hi, q_lo)

    def per_head(hd, carry):
        s0 = st_scr[hd * 2]
        s1 = st_scr[hd * 2 + 1]
        sv0 = _top_values(s0, _N_TOP)
        sv1 = _top_values(s1, _N_TOP)
        c = _merge_top(_candidate_levels(sv0, sv1), _N_TOP)
        thr = 0.5 * (c[PEER_TOPK - 1] + c[PEER_TOPK])
        z = jnp.zeros_like(thr)
        for k in range(PEER_TOPK):
            z = z + jnp.exp(c[k] - c[0])
        p0_ref[hd] = jnp.exp(s0 - sv0[0]) * (0.5 / z)
        te_ref[hd] = jnp.exp((thr - sv1[0]) - s0)
        e1_ref[hd] = jnp.exp(s1 - sv1[0])
        return carry

    lax.fori_loop(0, PEER_HEADS, per_head, 0)


def _route_call(x1, mod, wq16, keys, layer, n_ctx, lat_len):
    nt = x1.shape[0]
    tb = TB_ROUTE
    cond = functools.partial(_cond_of_tile, tile=tb, n_ctx=n_ctx, lat_len=lat_len)
    fac = lambda: pl.BlockSpec((PEER_HEADS, PEER_KEYS, tb), lambda i: (0, 0, i))
    fshape = jax.ShapeDtypeStruct((PEER_HEADS, PEER_KEYS, nt), F32)
    return pl.pallas_call(
        _route_kernel,
        grid=(nt // tb,),
        in_specs=[pl.BlockSpec((tb, D_MODEL), lambda i: (i, 0)),
                  pl.BlockSpec((1, 6, D_MODEL), lambda i: (cond(i), 0, 0)),
                  pl.BlockSpec((None,) + wq16.shape[1:], lambda i: (layer, 0, 0)),
                  pl.BlockSpec((None,) + keys.shape[1:], lambda i: (layer, 0, 0, 0))],
        out_specs=[pl.BlockSpec((tb, D_MODEL), lambda i: (i, 0)), fac(), fac(), fac()],
        out_shape=[jax.ShapeDtypeStruct((nt, D_MODEL), BF16), fshape, fshape, fshape],
        scratch_shapes=[pltpu.VMEM((2 * PEER_HEADS, PEER_KEYS, tb), F32)],
        compiler_params=_cparams(("parallel",)),
        name="peer_route",
    )(x1, mod, wq16, keys)


def _expert_kernel(h2_ref, u_ref, vt_ref, p0_ref, te_ref, e1_ref, x_ref, mod_ref, lg_ref, lb_ref,
                   o_ref, acc_scr, act_scr):
    c = pl.program_id(1)
    n_lane = h2_ref.shape[0] // PEER_KEYS

    @pl.when(c == 0)
    def _():
        acc_scr[...] = jnp.zeros_like(acc_scr)

    st = _dot_nt(u_ref[...], h2_ref[...])
    th = jnp.tanh(st * (_GELU_C0 + _GELU_C1 * (st * st)))
    act_scr[...] = st + st * th
    for pc in range(EC // EXP_PIECE):
        slabs = range(pc * EXP_PIECE // PEER_KEYS, (pc + 1) * EXP_PIECE // PEER_KEYS)
        g_lanes = []
        for ln in range(n_lane):
            lanes = slice(ln * PEER_KEYS, (ln + 1) * PEER_KEYS)
            w = [None] * len(slabs)
            for hd in range(PEER_HEADS):
                e1 = e1_ref[hd, :, lanes]
                for k, sl in enumerate(slabs):
                    te = te_ref[hd, 0, sl:sl + 1, lanes]
                    e0 = p0_ref[hd, 0, sl:sl + 1, lanes]
                    term = e0 * jnp.where(e1 > te, e1, 0.0)
                    w[k] = term if w[k] is None else w[k] + term
            g_lanes.append(jnp.concatenate(
                [(w[k] * act_scr[sl * PEER_KEYS:(sl + 1) * PEER_KEYS, lanes]).astype(BF16)
                 for k, sl in enumerate(slabs)], axis=0))
        g = jnp.concatenate(g_lanes, axis=1)
        acc_scr[...] += _dot(vt_ref[:, pc * EXP_PIECE:(pc + 1) * EXP_PIECE], g)

    @pl.when(c == pl.num_programs(1) - 1)
    def _():
        m = mod_ref[0]
        y = ALPHA * x_ref[...] + m[5:6] * acc_scr[...].T
        o_ref[...] = _layer_norm(y, lg_ref[...], lb_ref[...])


def _expert_call(h2, u16, vt16, layer, p0, te, e1, x1, mod, lg, lb, n_ctx, lat_len, tok0=0, n_tok=None):
    nt = x1.shape[0]
    n_tok = nt if n_tok is None else n_tok
    tb = TB_EXP
    t0 = tok0 // tb
    n_chunks = N_EXPERTS // EC
    cond = functools.partial(_cond_of_tile, tile=tb, n_ctx=n_ctx, lat_len=lat_len)
    p0r = p0.reshape(PEER_HEADS, n_chunks, N_SLAB, nt)
    ter = te.reshape(PEER_HEADS, n_chunks, N_SLAB, nt)
    slab = lambda: pl.BlockSpec((PEER_HEADS, 1, N_SLAB, tb), lambda i, c: (0, c, 0, t0 + i))
    full = lambda: pl.BlockSpec((PEER_HEADS, PEER_KEYS, tb), lambda i, c: (0, 0, t0 + i))
    const = lambda a: pl.BlockSpec(a.shape, lambda i, c: (0,) * a.ndim)
    return pl.pallas_call(
        _expert_kernel,
        grid=(n_tok // tb, n_chunks),
        in_specs=[pl.BlockSpec((tb, D_MODEL), lambda i, c: (t0 + i, 0)),
                  pl.BlockSpec((None, EC, D_MODEL), lambda i, c: (layer, c, 0)),
                  pl.BlockSpec((None, D_MODEL, EC), lambda i, c: (layer, 0, c)),
                  slab(), slab(), full(),
                  pl.BlockSpec((tb, D_MODEL), lambda i, c: (t0 + i, 0)),
                  pl.BlockSpec((1, 6, D_MODEL), lambda i, c: (cond(t0 + i), 0, 0)),
                  const(lg), const(lb)],
        out_specs=pl.BlockSpec((tb, D_MODEL), lambda i, c: (i, 0)),
        out_shape=jax.ShapeDtypeStruct((n_tok, D_MODEL), F32),
        scratch_shapes=[pltpu.VMEM((D_MODEL, tb), F32), pltpu.VMEM((EC, tb), F32)],
        compiler_params=_cparams(("parallel", "arbitrary")),
        name="peer_experts",
    )(h2, u16, vt16, p0r, ter, e1, x1, mod, lg, lb)


def _swap_halves(w, n_heads, rot):
    k = w.shape[0]
    w4 = w.reshape(k, n_heads, 2, rot // 2)
    return jnp.concatenate([w4[:, :, 1:2], w4[:, :, 0:1]], axis=2).reshape(k, n_heads * rot)


def _inproj_weight(w):
    a = w[:, 0:512]
    qb, kb, vb = w[:, 512:768], w[:, 768:1024], w[:, 1024:1280]
    cq, ckv, kr = w[:, 1280:1536], w[:, 1536:1664], w[:, 1664:1696]
    qd, kd, vd = w[:, 1696:1952], w[:, 1952:2080], w[:, 2080:2208]
    rep = lambda m: jnp.repeat(m.reshape(D_MODEL, 2, HEAD_DIM), 2, axis=1).reshape(D_MODEL, 256)
    kdr = rep(kd)
    cols = [a, qb, kb, vb, cq, ckv, _pair_cols(None, [kr, kr]), _pair_cols(None, [_swap_halves(kr, 1, MLA_ROPE)] * 2),
            qd, _swap_halves(qd, N_HEADS, HEAD_DIM), kdr, _swap_halves(kdr, N_HEADS, HEAD_DIM), rep(vd)]
    return jnp.concatenate(cols, axis=1).astype(BF16)


def _pair_cols(nope, rope):
    k = rope[0].shape[0]
    zeros = lambda w: jnp.zeros((k, w), rope[0].dtype)
    nope = nope if nope is not None else [zeros(MLA_NOPE)] * 2
    rope = rope if rope is not None else [zeros(MLA_ROPE)] * 2
    return jnp.concatenate([nope[0], rope[0], nope[1], rope[1], zeros(256 - 2 * PAIR_W)], axis=1)


def _mla_weights(w_uq, w_ukv):
    q3 = w_uq.reshape(MLA_Q_RANK, N_HEADS, PAIR_W)
    nope = [q3[:, hd, :MLA_NOPE] for hd in range(N_HEADS)]
    rope = [q3[:, hd, MLA_NOPE:] for hd in range(N_HEADS)]
    rope_sw = [_swap_halves(r, 1, MLA_ROPE) for r in rope]
    none64 = [jnp.zeros_like(nope[0])] * 2
    wuq = jnp.concatenate([_pair_cols(nope[0:2], rope[0:2]), _pair_cols(nope[2:4], rope[2:4]),
                           _pair_cols(none64, rope_sw[0:2]), _pair_cols(none64, rope_sw[2:4])], axis=1).astype(BF16)
    kv3 = w_ukv.reshape(MLA_KV_RANK, N_HEADS, MLA_NOPE + 64)
    knope = [kv3[:, hd, :MLA_NOPE] for hd in range(N_HEADS)]
    none32 = [jnp.zeros((MLA_KV_RANK, MLA_ROPE), w_ukv.dtype)] * 2
    wukv = jnp.concatenate([_pair_cols(knope[0:2], none32), _pair_cols(knope[2:4], none32),
                            kv3[:, :, MLA_NOPE:].reshape(MLA_KV_RANK, 256)], axis=1).astype(BF16)
    return wuq, wukv


def _pair_rope_tables(lat_len, tile):
    cos_t, sin_t = _rope_tables(lat_len, MLA_ROPE, tile)
    cos32, sin32 = cos_t[:, :MLA_ROPE], sin_t[:, :MLA_ROPE]
    n = cos_t.shape[0]
    one, zero = jnp.ones((n, MLA_NOPE), F32), jnp.zeros((n, MLA_NOPE), F32)
    return (jnp.concatenate([one, cos32, one, cos32, one], axis=1),
            jnp.concatenate([zero, sin32, zero, sin32, zero], axis=1))


def _rope_tables(lat_len, rot, tile):
    t = jnp.arange(lat_len)
    row = (t // GRID_W).astype(F32)
    col = (t % GRID_W).astype(F32)
    nf = rot // 4
    freqs = ROPE_BASE ** (-jnp.arange(nf, dtype=F32) / nf)
    ang = jnp.concatenate([row[:, None] * freqs, col[:, None] * freqs], -1)
    cos, sin = jnp.cos(ang), jnp.sin(ang)
    cos_t = jnp.tile(jnp.concatenate([cos, cos], -1), (1, N_HEADS))
    sin_t = jnp.tile(jnp.concatenate([-sin, sin], -1), (1, N_HEADS))
    w = N_HEADS * rot
    return (jnp.concatenate([jnp.ones((tile, w), F32), cos_t], 0),
            jnp.concatenate([jnp.zeros((tile, w), F32), sin_t], 0))


def _na_bias_tables(rpb, rows):
    kh = min(NA_KH, rows)
    qrow = np.array([0, 1, 2, 3, rows // 2, rows - 3, rows - 2, rows - 1])
    start = np.clip(qrow - kh // 2, 0, rows - kh)
    dr0 = start - qrow + NA_KH - 1
    qc = np.arange(GRID_W)
    kc = np.arange(GRID_W)
    cstart = np.clip(qc - NA_KW // 2, 0, GRID_W - NA_KW)
    ok = (kc[None, :] >= cstart[:, None]) & (kc[None, :] < cstart[:, None] + NA_KW)
    n_h, n_a, n_c = rpb.shape
    edge = GRID_W - NA_KW
    w = jnp.concatenate([jnp.broadcast_to(rpb[:, :, :1], (n_h, n_a, edge)), rpb,
                         jnp.broadcast_to(rpb[:, :, -1:], (n_h, n_a, edge + 1))], axis=-1)
    skew = jnp.tile(w, (1, 1, GRID_W))[:, :, :GRID_W * (2 * GRID_W - 1)].reshape(n_h, n_a, GRID_W, 2 * GRID_W - 1)
    toep = skew[:, :, :, GRID_W - 1:]
    tabs = []
    for d0 in dr0:
        b = jnp.where(ok[None, None], toep[:, d0:d0 + kh], NEG_INF)
        tabs.append(b.transpose(0, 2, 1, 3).reshape(N_HEADS * GRID_W, kh * GRID_W))
    return jnp.stack(tabs)


def _swa_bias_tables(lat_len):
    nb = lat_len // SWA_WIN
    qi = np.arange(SWA_WIN)
    kj = np.arange(3 * SWA_WIN)
    tabs = []
    for n in (0, 1, nb - 1):
        kpos = _swa_window_start(n, nb) + kj
        in_win = np.abs(kpos[None, :] - (n * SWA_WIN + qi[:, None])) <= SWA_WIN
        tabs.append(np.where(in_win, 0.0, NEG_INF))
    return jnp.asarray(np.stack(tabs), F32)


def _swa_window_start(n, nb):
    lo = n - 1
    lo = jnp.clip(lo, 0, nb - 3) if isinstance(n, jax.Array) else min(max(lo, 0), nb - 3)
    return lo * SWA_WIN


def _heads_to_lanes(t):
    b, h, s, d = t.shape
    return t.transpose(0, 2, 1, 3).reshape(b, s, h * d)


def _lanes_to_heads(t, b, s, h):
    return t.reshape(b, s, h, -1).transpose(0, 2, 1, 3)


def kernel(x_prompt, x_sample, cache_nat_k, cache_nat_v, cache_mla_ckv, cache_mla_krope, cache_swa_k, cache_swa_v, c, c_ctx, w_in, w_out, out_norm_g, w_ada, b_ada, ln1_g, ln1_b, ln2_g, ln2_b, a_norm_g, a_norm_b, a_w_s, a_b_s, nat_rpb, mla_q_norm_g, mla_w_uq, mla_kv_norm_g, mla_w_ukv, swa_sinks, peer_w_q, peer_sub_keys, peer_u, peer_v):
    n_b, seq, d = x_prompt.shape
    n_db, lat_len, _ = x_sample.shape
    past = cache_nat_k.shape[3]
    n_ctx = n_b * seq
    nt = n_ctx + n_db * lat_len
    rows = lat_len // GRID_W

    x = (x_prompt.reshape(n_ctx, d), x_sample.reshape(n_db * lat_len, d))
    conds = jnp.zeros((8, d), F32).at[0].set(c_ctx).at[1:1 + n_db].set(c)
    mods = _ada_call(conds, w_ada, b_ada).reshape(DEPTH, 8, 6, d)

    cos_c, sin_c = _pair_rope_tables(lat_len, TM)
    cos_d, sin_d = _rope_tables(lat_len, HEAD_DIM, TM)
    swa_bias = _swa_bias_tables(lat_len)
    nb_swa = lat_len // SWA_WIN

    q_ctx = lambda b, j: b
    lat_q = lambda tq: (lambda b, j: (n_ctx + b * lat_len) // tq + j)
    ctx_keys = lambda w: pl.BlockSpec((seq, w), lambda b, j: (b, 0))
    lat_keys = lambda w: pl.BlockSpec((lat_len, w), lambda b, j: (n_ctx // lat_len + b, 0))

    wq16 = peer_w_q.astype(BF16)
    u16 = peer_u.astype(BF16)
    vt16 = jnp.swapaxes(peer_v, 1, 2).astype(BF16)

    states = [[] for _ in range(6)]
    for l in range(DEPTH):
        mod = mods[l]
        wbig = _inproj_weight(w_in[l])
        wuq, wukv = _mla_weights(mla_w_uq[l], mla_w_ukv[l])
        mixer_a = (a_norm_g[l][None], a_norm_b[l][None], a_w_s[l].astype(BF16),
                   jnp.repeat(a_b_s[l].T, HEAD_DIM, axis=1))
        (oa, qb, kb, vb, kb16, vb16, qp0, qp1, ckvn, krp, kp0, kp1, vc,
         qd, kd, vd, kd16, vd16) = _inproj_call(
            x, mod, wbig, wuq, wukv, mla_q_norm_g[l][None], mla_kv_norm_g[l][None],
            (cos_c, sin_c, cos_d, sin_d), mixer_a, n_ctx, lat_len)

        ob_c = _attn_call("ctx_attn_b", (n_b, 1), seq, seq, HEAD_DIM ** -0.5, q_ctx, qb, kb16, vb16, ctx_keys)
        oc_c = _attn_call("ctx_attn_c", (n_b, 1), seq, seq, MLA_SCALE, q_ctx, qp0, kp0, vc, ctx_keys,
                          q2=qp1, k2=kp1)
        od_c = _attn_call("ctx_attn_d", (n_b, 1), seq, seq, HEAD_DIM ** -0.5, q_ctx, qd, kd16, vd16, ctx_keys,
                          sinks=swa_sinks[l])

        na_bias = _na_bias_tables(nat_rpb[l], rows)
        kh = min(NA_KH, rows)
        ob_l = _attn_call(
            "lat_attn_b", (n_db, rows // NA_ROWS_PER_STEP), GRID_W, kh * GRID_W, HEAD_DIM ** -0.5,
            lat_q(NA_ROWS_PER_STEP * GRID_W), qb, kb16, vb16, lat_keys, n_sub=NA_ROWS_PER_STEP,
            extra=(_heads_to_lanes(cache_nat_k[:, l]).astype(BF16), _heads_to_lanes(cache_nat_v[:, l]).astype(BF16)),
            bias=na_bias,
            bias_index=lambda r: jnp.where(r < 4, r, jnp.where(r > rows - 4, r - (rows - 8), 4)),
            start_fn=lambda r: jnp.clip(r - kh // 2, 0, rows - kh) * GRID_W)
        krx = cache_mla_krope[:, l].reshape(n_db * past, MLA_ROPE)
        kx0, kx1, vx = _kvexp_call(cache_mla_ckv[:, l].reshape(n_db * past, MLA_KV_RANK),
                                   _pair_cols(None, [krx, krx]), wukv)
        oc_l = _attn_call(
            "lat_attn_c", (n_db, lat_len // 512), 256, lat_len, MLA_SCALE, lat_q(512), qp0, kp0, vc, lat_keys, n_sub=2,
            q2=qp1, k2=kp1,
            extra=tuple(t.reshape(n_db, past, 256) for t in (kx0, kx1, vx)))
        rep_kv = lambda t: jnp.repeat(t, 2, axis=1)
        od_l = _attn_call(
            "lat_attn_d", (n_db, nb_swa // SWA_BLOCKS_PER_STEP), SWA_WIN, 3 * SWA_WIN, HEAD_DIM ** -0.5,
            lat_q(SWA_BLOCKS_PER_STEP * SWA_WIN), qd, kd16, vd16, lat_keys, n_sub=SWA_BLOCKS_PER_STEP,
            extra=(_heads_to_lanes(rep_kv(cache_swa_k[:, l])).astype(BF16),
                   _heads_to_lanes(rep_kv(cache_swa_v[:, l])).astype(BF16)),
            bias=swa_bias,
            bias_index=lambda n: jnp.where(n == 0, 0, jnp.where(n == nb_swa - 1, 2, 1)),
            sinks=swa_sinks[l],
            start_fn=lambda n: _swa_window_start(n, nb_swa))
        ob, oc, od = (ob_c, ob_l), (oc_c, oc_l), (od_c, od_l)

        x1 = _merge_call(oa, ob, oc, od, x, mod, out_norm_g[l][None], w_out[l].astype(BF16),
                         ln1_g[l][None], ln1_b[l][None], n_ctx, lat_len)
        h2, p0, te, e1 = _route_call(x1, mod, wq16, peer_sub_keys, l, n_ctx, lat_len)
        experts = functools.partial(_expert_call, h2, u16, vt16, l, p0, te, e1, x1, mod,
                                    ln2_g[l][None], ln2_b[l][None], n_ctx, lat_len)
        if l + 1 < DEPTH:
            x = experts()
        else:
            y_prompt = experts(tok0=0, n_tok=n_ctx).reshape(n_b, seq, d)
            y_sample = experts(tok0=n_ctx, n_tok=n_db * lat_len).reshape(n_db, lat_len, d)

        states[0].append(_lanes_to_heads(kb[:n_ctx], n_b, seq, N_HEADS))
        states[1].append(_lanes_to_heads(vb[:n_ctx], n_b, seq, N_HEADS))
        states[2].append(ckvn[:n_ctx].reshape(n_b, seq, MLA_KV_RANK))
        states[3].append(krp[:n_ctx, MLA_NOPE:PAIR_W].reshape(n_b, seq, MLA_ROPE))
        states[4].append(_lanes_to_heads(kd[:n_ctx], n_b, seq, N_HEADS)[:, ::2])
        states[5].append(_lanes_to_heads(vd[:n_ctx], n_b, seq, N_HEADS)[:, ::2])

    return (y_prompt, y_sample) + tuple(jnp.stack(s, axis=1) for s in states)
```

```python
import functools
import math

import jax
import jax.numpy as jnp
import numpy as np
from jax import lax
from jax.experimental import pallas as pl
from jax.experimental.pallas import tpu as pltpu

F32 = jnp.float32
BF16 = jnp.bfloat16

D_MODEL = 1024
DEPTH = 2
GRID_W = 64
HEAD_DIM = 64
N_HEADS = 4
GROUP_W = 256
CHUNK = 128
NA_KH = 8
NA_KW = 16
MLA_Q_RANK = 256
MLA_KV_RANK = 128
MLA_NOPE = 64
MLA_ROPE = 32
MLA_SCALE = (MLA_NOPE + MLA_ROPE) ** -0.5
SWA_WIN = 128
PEER_HEADS = 8
PEER_KEYS = 128
PEER_TOPK = 16
N_EXPERTS = PEER_KEYS * PEER_KEYS
ROPE_BASE = 10000.0
LN_EPS = 1e-5
NEG_INF = -1e30
ALPHA = (2 * DEPTH) ** 0.25

V7X_VMEM_LIMIT_BYTES = 56 * 1024 * 1024
TM = 512
TB_ROUTE = 512
TB_EXP = 1024
ATTN_KEY_CHUNK = 1024
SINK_PAD = 128
NA_ROWS_PER_STEP = 8
SWA_BLOCKS_PER_STEP = 4
EC = 1024
N_SLAB = EC // PEER_KEYS
EXP_PIECE = 256
_C_A = 0
_C_QB, _C_KB, _C_VB = 512, 768, 1024
_C_CQ, _C_CKV, _C_KR, _C_KRS = 1280, 1536, 1664, 1920
_C_QD, _C_QDS, _C_KD, _C_KDS, _C_VD = 2176, 2432, 2688, 2944, 3200
_C_END = 3456
PAIR_W = MLA_NOPE + MLA_ROPE


def _cparams(sem):
    return pltpu.CompilerParams(dimension_semantics=sem, vmem_limit_bytes=V7X_VMEM_LIMIT_BYTES)


def _dot(a, b):
    return jnp.dot(a, b, preferred_element_type=F32)


def _dot_nt(a, b):
    return lax.dot_general(a, b, (((1,), (1,)), ((), ())), preferred_element_type=F32)


def _layer_norm(x, g, b):
    mu = jnp.mean(x, axis=-1, keepdims=True)
    xc = x - mu
    var = jnp.mean(xc * xc, axis=-1, keepdims=True)
    return xc * lax.rsqrt(var + LN_EPS) * g + b


def _rms_norm(x, g):
    return x * lax.rsqrt(jnp.mean(x * x, axis=-1, keepdims=True) + LN_EPS) * g


def _ada_kernel(c_ref, w_ref, b_ref, o_ref):
    c = c_ref[...]
    a = c * jax.nn.sigmoid(c)
    a_hi = a.astype(BF16)
    a_lo = (a - a_hi.astype(F32)).astype(BF16)
    w = w_ref[0]
    w_hi = w.astype(BF16)
    w_lo = (w - w_hi.astype(F32)).astype(BF16)
    o_ref[0] = _dot(a_hi, w_hi) + _dot(a_hi, w_lo) + _dot(a_lo, w_hi) + b_ref[0]


def _ada_call(conds, w_ada, b_ada):
    tn = 1536
    n = w_ada.shape[-1]
    return pl.pallas_call(
        _ada_kernel,
        grid=(DEPTH, n // tn),
        in_specs=[pl.BlockSpec((8, D_MODEL), lambda l, j: (0, 0)),
                  pl.BlockSpec((1, D_MODEL, tn), lambda l, j: (l, 0, j)),
                  pl.BlockSpec((1, 1, tn), lambda l, j: (l, 0, j))],
        out_specs=pl.BlockSpec((1, 8, tn), lambda l, j: (l, 0, j)),
        out_shape=jax.ShapeDtypeStruct((DEPTH, 8, n), F32),
        compiler_params=_cparams(("parallel", "parallel")),
        name="ada_mod",
    )(conds, w_ada, b_ada.reshape(DEPTH, 1, n))


def _token_rows(x, n_ctx):
    nct = n_ctx // TM
    if isinstance(x, tuple):
        lat_block = lambda i: jnp.maximum(i - nct, 0)
    else:
        x, lat_block = (x, x), (lambda i: jnp.maximum(i, nct))
    d = x[0].shape[1]
    return x, [pl.BlockSpec((TM, d), lambda i: (jnp.minimum(i, nct - 1), 0)),
               pl.BlockSpec((TM, d), lambda i: (lat_block(i), 0))]


def _inproj_kernel(xc_ref, xl_ref, mod_ref, w_ref, wuq_ref, wukv_ref, gq_ref, gkv_ref,
                   cosc_ref, sinc_ref, cosd_ref, sind_ref, ag_ref, ab_ref, aws_ref, abs_ref,
                   oa_ref, qb_ref, kb_ref, vb_ref, kb16_ref, vb16_ref,
                   qp0_ref, qp1_ref, ckvn_ref, krp_ref, kp0_ref, kp1_ref, vc_ref,
                   qd_ref, kd_ref, vd_ref, kd16_ref, vd16_ref, *, n_ctx_tiles):
    m = mod_ref[0]
    x = jnp.where(pl.program_id(0) < n_ctx_tiles, xc_ref[...], xl_ref[...])
    h = x * (1.0 + m[1:2]) + m[0:1]
    z = _dot(h.astype(BF16), w_ref[...])
    cosc, sinc = cosc_ref[...], sinc_ref[...]
    cosd, sind = cosd_ref[...], sind_ref[...]

    for c0 in range(0, z.shape[0], CHUNK):
        oa_ref[c0:c0 + CHUNK, :] = _chunk_mlp(z[c0:c0 + CHUNK, _C_A:_C_QB], ag_ref[...], ab_ref[...], aws_ref,
                                              abs_ref[...])
    qb_ref[...] = z[:, _C_QB:_C_KB].astype(BF16)
    kb = z[:, _C_KB:_C_VB]
    vb = z[:, _C_VB:_C_CQ]
    kb_ref[...] = kb
    vb_ref[...] = vb
    kb16_ref[...] = kb.astype(BF16)
    vb16_ref[...] = vb.astype(BF16)

    cqn = _rms_norm(z[:, _C_CQ:_C_CKV], gq_ref[...])
    q = _dot(cqn.astype(BF16), wuq_ref[...])
    qp0_ref[...] = (q[:, 0:256] * cosc + q[:, 512:768] * sinc).astype(BF16)
    qp1_ref[...] = (q[:, 256:512] * cosc + q[:, 768:1024] * sinc).astype(BF16)
    ckvn = _rms_norm(z[:, _C_CKV:_C_KR], gkv_ref[...])
    ckvn_ref[...] = ckvn
    kv = _dot(ckvn.astype(BF16), wukv_ref[...])
    kr = z[:, _C_KR:_C_KRS] * cosc + z[:, _C_KRS:_C_QD] * sinc
    krp_ref[...] = kr
    kp0_ref[...] = (kv[:, 0:256] + kr).astype(BF16)
    kp1_ref[...] = (kv[:, 256:512] + kr).astype(BF16)
    vc_ref[...] = kv[:, 512:768].astype(BF16)

    qd_ref[...] = (z[:, _C_QD:_C_QDS] * cosd + z[:, _C_QDS:_C_KD] * sind).astype(BF16)
    kd = z[:, _C_KD:_C_KDS] * cosd + z[:, _C_KDS:_C_VD] * sind
    vd = z[:, _C_VD:_C_END]
    kd_ref[...] = kd
    vd_ref[...] = vd
    kd16_ref[...] = kd.astype(BF16)
    vd16_ref[...] = vd.astype(BF16)


def _cond_of_tile(i, tile, n_ctx, lat_len):
    n_ctx_tiles = n_ctx // tile
    return jnp.where(i < n_ctx_tiles, 0, 1 + (i - n_ctx_tiles) // (lat_len // tile))


def _rope_block_of_tile(i, tile, n_ctx, lat_len):
    n_ctx_tiles = n_ctx // tile
    return jnp.where(i < n_ctx_tiles, 0, 1 + (i - n_ctx_tiles) % (lat_len // tile))


def _inproj_call(x, mod, wbig, wuq, wukv, gq, gkv, tabs, mixer_a, n_ctx, lat_len):
    nt = sum(a.shape[0] for a in x) if isinstance(x, tuple) else x.shape[0]
    xs, x_specs = _token_rows(x, n_ctx)
    cond = functools.partial(_cond_of_tile, tile=TM, n_ctx=n_ctx, lat_len=lat_len)
    rblk = functools.partial(_rope_block_of_tile, tile=TM, n_ctx=n_ctx, lat_len=lat_len)
    row = lambda w: pl.BlockSpec((TM, w), lambda i: (i, 0))
    const = lambda a: pl.BlockSpec(a.shape, lambda i: (0,) * a.ndim)
    tab = lambda w: pl.BlockSpec((TM, w), lambda i: (rblk(i), 0))
    outs = [(256, F32), (256, BF16), (256, F32), (256, F32), (256, BF16), (256, BF16),
            (256, BF16), (256, BF16), (128, F32), (256, F32), (256, BF16), (256, BF16), (256, BF16),
            (256, BF16), (256, F32), (256, F32), (256, BF16), (256, BF16)]
    return pl.pallas_call(
        functools.partial(_inproj_kernel, n_ctx_tiles=n_ctx // TM),
        grid=(nt // TM,),
        in_specs=x_specs + [
                  pl.BlockSpec((1, 6, D_MODEL), lambda i: (cond(i), 0, 0)),
                  const(wbig), const(wuq), const(wukv), const(gq), const(gkv),
                  tab(256), tab(256), tab(256), tab(256)] + [const(a) for a in mixer_a],
        out_specs=[row(w) for w, _ in outs],
        out_shape=[jax.ShapeDtypeStruct((nt, w), dt) for w, dt in outs],
        compiler_params=_cparams(("parallel",)),
        name="inproj",
    )(*xs, mod, wbig, wuq, wukv, gq, gkv, *tabs, *mixer_a)


def _kvexp_kernel(c_ref, kr_ref, w_ref, k0_ref, k1_ref, v_ref):
    kv = _dot(c_ref[...].astype(BF16), w_ref[...])
    kr = kr_ref[...]
    k0_ref[...] = (kv[:, 0:256] + kr).astype(BF16)
    k1_ref[...] = (kv[:, 256:512] + kr).astype(BF16)
    v_ref[...] = kv[:, 512:768].astype(BF16)


def _kvexp_call(ckv, kr_pair, wukv):
    n = ckv.shape[0]
    whole = lambda a: pl.BlockSpec(a.shape, lambda i: (0, 0))
    return pl.pallas_call(
        _kvexp_kernel,
        grid=(1,),
        in_specs=[whole(ckv), whole(kr_pair), whole(wukv)],
        out_specs=[pl.BlockSpec((n, 256), lambda i: (0, 0))] * 3,
        out_shape=[jax.ShapeDtypeStruct((n, 256), BF16)] * 3,
        compiler_params=_cparams(("arbitrary",)),
        name="mla_cache_expand",
    )(ckv, kr_pair, wukv)


def _chunk_mlp(z, gain, bias, ws_ref, bs):
    g = jax.nn.gelu(z)
    u = g[:, 0:GROUP_W]
    v = _layer_norm(g[:, GROUP_W:2 * GROUP_W], gain, bias).astype(BF16)
    lane_head = lax.broadcasted_iota(jnp.int32, (1, GROUP_W), 1) // HEAD_DIM
    mixed = bs
    for hd in range(N_HEADS):
        mixed = mixed + jnp.where(lane_head == hd, _dot(ws_ref[hd], v), 0.0)
    return u * mixed


def _attn_kernel(*refs, tq, n_sub, **static):
    for sb in range(n_sub):
        _attn_tile(refs, pl.program_id(1) * n_sub + sb, slice(sb * tq, (sb + 1) * tq), tq=tq, **static)


def _attn_tile(refs, tile, q_rows, *, tq, wk, kc, scale, has_q2, has_extra, bias_heads, has_sink, start_fn, bias_index):
    refs = list(refs)
    xbias_ref = refs.pop(0) if has_sink else None
    q1_ref = refs.pop(0)
    q2_ref = refs.pop(0) if has_q2 else None
    k1_ref = refs.pop(0)
    k2_ref = refs.pop(0) if has_q2 else None
    v_ref = refs.pop(0)
    if has_extra:
        xk1_ref = refs.pop(0)
        xk2_ref = refs.pop(0) if has_q2 else None
        xv_ref = refs.pop(0)
    bias_ref = refs.pop(0) if bias_heads else None
    o_ref = refs.pop(0)

    def stack_heads(q, width, n):
        lane_head = lax.broadcasted_iota(jnp.int32, (1, q.shape[1]), 1) // width
        return jnp.concatenate([jnp.where(lane_head == hd, q, jnp.zeros_like(q)) for hd in range(n)], axis=0)

    if has_q2:
        q1s = stack_heads(q1_ref[q_rows, :], PAIR_W, 2)
        q2s = stack_heads(q2_ref[q_rows, :], PAIR_W, 2)
    else:
        q1s = stack_heads(q1_ref[q_rows, :], HEAD_DIM, N_HEADS)

    start = start_fn(tile)
    if not isinstance(start, int):
        start = pl.multiple_of(start, 64)
    bias_tile = bias_ref.at[bias_index(tile)] if bias_heads else None

    chunks = [("win", c0, min(kc, wk - c0)) for c0 in range(0, wk, kc)]
    if has_extra:
        chunks.append(("extra", 0, 0))
    m = denom = o = None
    for kind, c0, n in chunks:
        if kind == "win":
            rows = pl.ds(start if c0 == 0 else start + c0, n)
            k1c, vc = k1_ref[rows, :], v_ref[rows, :]
            k2c = k2_ref[rows, :] if has_q2 else None
        else:
            k1c, vc = xk1_ref[...], xv_ref[...]
            k2c = xk2_ref[...] if has_q2 else None
        s = _dot_nt(q1s, k1c)
        if has_q2:
            s = jnp.concatenate([s, _dot_nt(q2s, k2c)], axis=0)
        s = s * scale
        if kind == "win" and bias_heads == N_HEADS:
            s = s + bias_tile[:, c0:c0 + n]
        elif kind == "win" and bias_heads == 1:
            s = s + jnp.concatenate([bias_tile[:, c0:c0 + n]] * N_HEADS, axis=0)
        elif kind == "extra" and has_sink:
            s = s + xbias_ref[...]
        mc = jnp.max(s, axis=-1, keepdims=True)
        m_new = mc if m is None else jnp.maximum(m, mc)
        p = jnp.exp(s - m_new)
        pv = _dot(p.astype(BF16), vc)
        if m is None:
            denom, o = jnp.sum(p, axis=-1, keepdims=True), pv
        else:
            alpha = jnp.exp(m - m_new)
            denom = alpha * denom + jnp.sum(p, axis=-1, keepdims=True)
            o = alpha * o + pv
        m = m_new
    o = o / denom
    lane_head = lax.broadcasted_iota(jnp.int32, (1, N_HEADS * HEAD_DIM), 1) // HEAD_DIM
    out = jnp.zeros((tq, N_HEADS * HEAD_DIM), F32)
    for hd in range(N_HEADS):
        out = out + jnp.where(lane_head == hd, o[hd * tq:(hd + 1) * tq], 0.0)
    o_ref[q_rows, :] = out


def _attn_call(name, grid, tq, wk, scale, q_index, q1, k1, v, k_spec_fn, *, q2=None, k2=None, extra=None,
               bias=None, bias_index=None, sinks=None, start_fn=lambda t: 0, kc=ATTN_KEY_CHUNK, n_sub=1):
    tb = n_sub * tq
    has_q2 = q2 is not None
    bias_heads = 0 if bias is None else bias.shape[1] // tq
    args, specs = [], []
    extra_index = lambda b, j: (b, 0, 0)
    if sinks is not None:
        assert not has_q2
        if extra is None:
            extra = tuple(jnp.zeros((1, 0, 256), BF16) for _ in range(2))
            extra_index = lambda b, j: (0, 0, 0)
        n_real = extra[0].shape[1]
        extra = tuple(jnp.pad(a, ((0, 0), (0, SINK_PAD), (0, 0))) for a in extra)
        col = jnp.arange(n_real + SINK_PAD)[None, :]
        sink_rows = jnp.repeat(sinks.astype(F32), tq)[:, None]
        xbias = jnp.where(col < n_real, 0.0, jnp.where(col == n_real, sink_rows, NEG_INF))
        args.append(xbias)
        specs.append(pl.BlockSpec(xbias.shape, lambda b, j: (0, 0)))
    has_extra = extra is not None
    args.append(q1)
    specs.append(pl.BlockSpec((tb, 256), lambda b, j: (q_index(b, j), 0)))
    if has_q2:
        args.append(q2)
        specs.append(pl.BlockSpec((tb, 256), lambda b, j: (q_index(b, j), 0)))
    args.append(k1)
    specs.append(k_spec_fn(256))
    if has_q2:
        args.append(k2)
        specs.append(k_spec_fn(256))
    args.append(v)
    specs.append(k_spec_fn(256))
    if has_extra:
        for a in extra:
            args.append(a)
            specs.append(pl.BlockSpec((None,) + a.shape[1:], extra_index))
    if bias is not None:
        args.append(bias)
        specs.append(pl.BlockSpec(bias.shape, lambda b, j: (0, 0, 0)))
    kern = functools.partial(_attn_kernel, tq=tq, n_sub=n_sub, wk=wk, kc=kc, scale=scale, has_q2=has_q2,
                             has_extra=has_extra, bias_heads=bias_heads, has_sink=sinks is not None,
                             start_fn=start_fn, bias_index=bias_index)
    return pl.pallas_call(
        kern,
        grid=grid,
        in_specs=specs,
        out_specs=pl.BlockSpec((tb, 256), lambda b, j: (b * grid[1] + j, 0)),
        out_shape=jax.ShapeDtypeStruct((grid[0] * grid[1] * tb, 256), F32),
        compiler_params=_cparams(("parallel", "parallel")),
        name=name,
    )(*args)


def _merge_kernel(oa_ref, obc_ref, obl_ref, occ_ref, ocl_ref, odc_ref, odl_ref, xc_ref, xl_ref, mod_ref, g_ref, w_ref,
                  lg_ref, lb_ref, o_ref, *, n_ctx_tiles):
    m = mod_ref[0]
    is_ctx = pl.program_id(0) < n_ctx_tiles
    pick = lambda c_ref, l_ref: jnp.where(is_ctx, c_ref[...], l_ref[...])
    groups = (oa_ref[...], pick(obc_ref, obl_ref), pick(occ_ref, ocl_ref), pick(odc_ref, odl_ref))
    acc = None
    for gi, o in enumerate(groups):
        og = _rms_norm(o, g_ref[:, gi * GROUP_W:(gi + 1) * GROUP_W]).astype(BF16)
        part = _dot(og, w_ref[gi * GROUP_W:(gi + 1) * GROUP_W, :])
        acc = part if acc is None else acc + part
    y = ALPHA * pick(xc_ref, xl_ref) + m[2:3] * acc
    o_ref[...] = _layer_norm(y, lg_ref[...], lb_ref[...])


def _merge_call(oa, ob, oc, od, x, mod, gout, wout16, lg, lb, n_ctx, lat_len):
    nt = sum(a.shape[0] for a in x) if isinstance(x, tuple) else x.shape[0]
    xs, x_specs = _token_rows(x, n_ctx)
    nct = n_ctx // TM
    cond = functools.partial(_cond_of_tile, tile=TM, n_ctx=n_ctx, lat_len=lat_len)
    row = lambda w: pl.BlockSpec((TM, w), lambda i: (i, 0))
    ctx_row = lambda: pl.BlockSpec((TM, 256), lambda i: (jnp.minimum(i, nct - 1), 0))
    lat_row = lambda: pl.BlockSpec((TM, 256), lambda i: (jnp.maximum(i - nct, 0), 0))
    const = lambda a: pl.BlockSpec(a.shape, lambda i: (0,) * a.ndim)
    return pl.pallas_call(
        functools.partial(_merge_kernel, n_ctx_tiles=nct),
        grid=(nt // TM,),
        in_specs=[row(256), ctx_row(), lat_row(), ctx_row(), lat_row(), ctx_row(), lat_row()] + x_specs + [
                  pl.BlockSpec((1, 6, D_MODEL), lambda i: (cond(i), 0, 0)),
                  const(gout), const(wout16), const(lg), const(lb)],
        out_specs=row(D_MODEL),
        out_shape=jax.ShapeDtypeStruct((nt, D_MODEL), F32),
        compiler_params=_cparams(("parallel",)),
        name="merge_out",
    )(oa, *ob, *oc, *od, *xs, mod, gout, wout16, lg, lb)


_N_TOP = PEER_TOPK + 1
_GELU_C0 = math.sqrt(2.0 / math.pi)
_GELU_C1 = 0.044715 * _GELU_C0


def _sort_network(n):
    pairs = []
    p = 1
    while p < n:
        k = p
        while k >= 1:
            for j in range(k % p, n - k, 2 * k):
                for i in range(min(k, n - j - k)):
                    if (i + j) // (2 * p) == (i + j + k) // (2 * p):
                        pairs.append((i + j, i + j + k))
            k //= 2
        p *= 2
    return pairs


def _merge_top(levels, n_top):
    levels = list(levels)
    sub = lax.broadcasted_iota(jnp.int32, levels[0].shape, 0).astype(F32)
    out = []
    for k in range(n_top):
        head = levels[0]
        m = jnp.max(head, axis=0, keepdims=True)
        out.append(m)
        first = jnp.min(jnp.where(head == m, sub, 8.0), axis=0, keepdims=True)
        pop = sub == first
        for v in range(n_top - 1 - k):
            nxt = levels[v + 1] if v + 1 < len(levels) else -jnp.inf
            levels[v] = jnp.where(pop, nxt, levels[v])
    return out


def _top_values(s, n_top):
    g = [s[8 * v:8 * v + 8] for v in range(s.shape[0] // 8)]
    for i, j in _sort_network(len(g)):
        g[i], g[j] = jnp.maximum(g[i], g[j]), jnp.minimum(g[i], g[j])
    return _merge_top(g, n_top)


_CAND_LEVELS = (17, 8, 5, 4, 13, 4, 1, 0)


def _candidate_levels(sv0, sv1):
    shape = (8,) + sv0[0].shape[1:]
    row = lax.broadcasted_iota(jnp.int32, shape, 0)
    pick = lambda vals, default: functools.reduce(
        lambda acc, rv: jnp.where(row == rv[0], rv[1], acc), vals, jnp.full(shape, default, F32))
    fixed0 = pick([(r, sv0[r]) for r in range(4)], 0.0)
    fixed1 = pick([(4 + r, sv1[r]) for r in range(3)], 0.0)
    n_valid = pick([(r, float(n)) for r, n in enumerate(_CAND_LEVELS)], 0.0)
    levels = []
    for v in range(max(_CAND_LEVELS)):
        moving0 = sv0[min(4 + v, len(sv0) - 1)]
        lv = jnp.where(row < 4, fixed0 + sv1[v], moving0 + fixed1)
        levels.append(jnp.where(n_valid > float(v), lv, -jnp.inf))
    return levels


def _route_kernel(x_ref, mod_ref, wq_ref, keys_ref, h2_ref, p0_ref, te_ref, e1_ref, st_scr):
    m = mod_ref[0]
    h2 = (x_ref[...] * (1.0 + m[4:5]) + m[3:4]).astype(BF16)
    h2_ref[...] = h2
    q = _dot(h2, wq_ref[...])
    tb = q.shape[0]
    for p in range(2):
        kp = keys_ref[p]
        k_hi = kp.astype(BF16)
        k_lo = (kp - k_hi.astype(F32)).astype(BF16)
        for hd in range(PEER_HEADS):
            c0 = (hd * 2 + p) * PEER_KEYS
            qs = q[:, c0:c0 + PEER_KEYS]
            q_hi = qs.astype(BF16)
            q_lo = (qs - q_hi.astype(F32)).astype(BF16)
            st_scr[hd * 2 + p] = _dot_nt(k_hi, q_hi) + _dot_nt(k_lo, q_hi) + _dot_nt(k_hi, q_lo)

    def per_head(hd, carry):
        s0 = st_scr[hd * 2]
        s1 = st_scr[hd * 2 + 1]
        sv0 = _top_values(s0, _N_TOP)
        sv1 = _top_values(s1, _N_TOP)
        c = _merge_top(_candidate_levels(sv0, sv1), _N_TOP)
        thr = 0.5 * (c[PEER_TOPK - 1] + c[PEER_TOPK])
        z = jnp.zeros_like(thr)
        for k in range(PEER_TOPK):
            z = z + jnp.exp(c[k] - c[0])
        p0_ref[hd] = jnp.exp(s0 - sv0[0]) * (0.5 / z)
        te_ref[hd] = jnp.exp((thr - sv1[0]) - s0)
        e1_ref[hd] = jnp.exp(s1 - sv1[0])
        return carry

    lax.fori_loop(0, PEER_HEADS, per_head, 0)


def _route_call(x1, mod, wq16, keys, layer, n_ctx, lat_len):
    nt = x1.shape[0]
    tb = TB_ROUTE
    cond = functools.partial(_cond_of_tile, tile=tb, n_ctx=n_ctx, lat_len=lat_len)
    fac = lambda: pl.BlockSpec((PEER_HEADS, PEER_KEYS, tb), lambda i: (0, 0, i))
    fshape = jax.ShapeDtypeStruct((PEER_HEADS, PEER_KEYS, nt), F32)
    return pl.pallas_call(
        _route_kernel,
        grid=(nt // tb,),
        in_specs=[pl.BlockSpec((tb, D_MODEL), lambda i: (i, 0)),
                  pl.BlockSpec((1, 6, D_MODEL), lambda i: (cond(i), 0, 0)),
                  pl.BlockSpec((None,) + wq16.shape[1:], lambda i: (layer, 0, 0)),
                  pl.BlockSpec((None,) + keys.shape[1:], lambda i: (layer, 0, 0, 0))],
        out_specs=[pl.BlockSpec((tb, D_MODEL), lambda i: (i, 0)), fac(), fac(), fac()],
        out_shape=[jax.ShapeDtypeStruct((nt, D_MODEL), BF16), fshape, fshape, fshape],
        scratch_shapes=[pltpu.VMEM((2 * PEER_HEADS, PEER_KEYS, tb), F32)],
        compiler_params=_cparams(("parallel",)),
        name="peer_route",
    )(x1, mod, wq16, keys)


def _expert_kernel(h2_ref, u_ref, vt_ref, p0_ref, te_ref, e1_ref, x_ref, mod_ref, lg_ref, lb_ref,
                   o_ref, acc_scr, act_scr):
    c = pl.program_id(1)
    n_lane = h2_ref.shape[0] // PEER_KEYS

    @pl.when(c == 0)
    def _():
        acc_scr[...] = jnp.zeros_like(acc_scr)

    st = _dot_nt(u_ref[...], h2_ref[...])
    th = jnp.tanh(st * (_GELU_C0 + _GELU_C1 * (st * st)))
    act_scr[...] = st + st * th
    for pc in range(EC // EXP_PIECE):
        slabs = range(pc * EXP_PIECE // PEER_KEYS, (pc + 1) * EXP_PIECE // PEER_KEYS)
        g_lanes = []
        for ln in range(n_lane):
            lanes = slice(ln * PEER_KEYS, (ln + 1) * PEER_KEYS)
            w = [None] * len(slabs)
            for hd in range(PEER_HEADS):
                e1 = e1_ref[hd, :, lanes]
                for k, sl in enumerate(slabs):
                    te = te_ref[hd, 0, sl:sl + 1, lanes]
                    e0 = p0_ref[hd, 0, sl:sl + 1, lanes]
                    term = e0 * jnp.where(e1 > te, e1, 0.0)
                    w[k] = term if w[k] is None else w[k] + term
            g_lanes.append(jnp.concatenate(
                [(w[k] * act_scr[sl * PEER_KEYS:(sl + 1) * PEER_KEYS, lanes]).astype(BF16)
                 for k, sl in enumerate(slabs)], axis=0))
        g = jnp.concatenate(g_lanes, axis=1)
        acc_scr[...] += _dot(vt_ref[:, pc * EXP_PIECE:(pc + 1) * EXP_PIECE], g)

    @pl.when(c == pl.num_programs(1) - 1)
    def _():
        m = mod_ref[0]
        y = ALPHA * x_ref[...] + m[5:6] * acc_scr[...].T
        o_ref[...] = _layer_norm(y, lg_ref[...], lb_ref[...])


def _expert_call(h2, u16, vt16, layer, p0, te, e1, x1, mod, lg, lb, n_ctx, lat_len, tok0=0, n_tok=None):
    nt = x1.shape[0]
    n_tok = nt if n_tok is None else n_tok
    tb = TB_EXP
    t0 = tok0 // tb
    n_chunks = N_EXPERTS // EC
    cond = functools.partial(_cond_of_tile, tile=tb, n_ctx=n_ctx, lat_len=lat_len)
    p0r = p0.reshape(PEER_HEADS, n_chunks, N_SLAB, nt)
    ter = te.reshape(PEER_HEADS, n_chunks, N_SLAB, nt)
    slab = lambda: pl.BlockSpec((PEER_HEADS, 1, N_SLAB, tb), lambda i, c: (0, c, 0, t0 + i))
    full = lambda: pl.BlockSpec((PEER_HEADS, PEER_KEYS, tb), lambda i, c: (0, 0, t0 + i))
    const = lambda a: pl.BlockSpec(a.shape, lambda i, c: (0,) * a.ndim)
    return pl.pallas_call(
        _expert_kernel,
        grid=(n_tok // tb, n_chunks),
        in_specs=[pl.BlockSpec((tb, D_MODEL), lambda i, c: (t0 + i, 0)),
                  pl.BlockSpec((None, EC, D_MODEL), lambda i, c: (layer, c, 0)),
                  pl.BlockSpec((None, D_MODEL, EC), lambda i, c: (layer, 0, c)),
                  slab(), slab(), full(),
                  pl.BlockSpec((tb, D_MODEL), lambda i, c: (t0 + i, 0)),
                  pl.BlockSpec((1, 6, D_MODEL), lambda i, c: (cond(t0 + i), 0, 0)),
                  const(lg), const(lb)],
        out_specs=pl.BlockSpec((tb, D_MODEL), lambda i, c: (i, 0)),
        out_shape=jax.ShapeDtypeStruct((n_tok, D_MODEL), F32),
        scratch_shapes=[pltpu.VMEM((D_MODEL, tb), F32), pltpu.VMEM((EC, tb), F32)],
        compiler_params=_cparams(("parallel", "arbitrary")),
        name="peer_experts",
    )(h2, u16, vt16, p0r, ter, e1, x1, mod, lg, lb)


def _swap_halves(w, n_heads, rot):
    k = w.shape[0]
    w4 = w.reshape(k, n_heads, 2, rot // 2)
    return jnp.concatenate([w4[:, :, 1:2], w4[:, :, 0:1]], axis=2).reshape(k, n_heads * rot)


def _inproj_weight(w):
    a = w[:, 0:512]
    qb, kb, vb = w[:, 512:768], w[:, 768:1024], w[:, 1024:1280]
    cq, ckv, kr = w[:, 1280:1536], w[:, 1536:1664], w[:, 1664:1696]
    qd, kd, vd = w[:, 1696:1952], w[:, 1952:2080], w[:, 2080:2208]
    rep = lambda m: jnp.repeat(m.reshape(D_MODEL, 2, HEAD_DIM), 2, axis=1).reshape(D_MODEL, 256)
    kdr = rep(kd)
    cols = [a, qb, kb, vb, cq, ckv, _pair_cols(None, [kr, kr]), _pair_cols(None, [_swap_halves(kr, 1, MLA_ROPE)] * 2),
            qd, _swap_halves(qd, N_HEADS, HEAD_DIM), kdr, _swap_halves(kdr, N_HEADS, HEAD_DIM), rep(vd)]
    return jnp.concatenate(cols, axis=1).astype(BF16)


def _pair_cols(nope, rope):
    k = rope[0].shape[0]
    zeros = lambda w: jnp.zeros((k, w), rope[0].dtype)
    nope = nope if nope is not None else [zeros(MLA_NOPE)] * 2
    rope = rope if rope is not None else [zeros(MLA_ROPE)] * 2
    return jnp.concatenate([nope[0], rope[0], nope[1], rope[1], zeros(256 - 2 * PAIR_W)], axis=1)


def _mla_weights(w_uq, w_ukv):
    q3 = w_uq.reshape(MLA_Q_RANK, N_HEADS, PAIR_W)
    nope = [q3[:, hd, :MLA_NOPE] for hd in range(N_HEADS)]
    rope = [q3[:, hd, MLA_NOPE:] for hd in range(N_HEADS)]
    rope_sw = [_swap_halves(r, 1, MLA_ROPE) for r in rope]
    none64 = [jnp.zeros_like(nope[0])] * 2
    wuq = jnp.concatenate([_pair_cols(nope[0:2], rope[0:2]), _pair_cols(nope[2:4], rope[2:4]),
                           _pair_cols(none64, rope_sw[0:2]), _pair_cols(none64, rope_sw[2:4])], axis=1).astype(BF16)
    kv3 = w_ukv.reshape(MLA_KV_RANK, N_HEADS, MLA_NOPE + 64)
    knope = [kv3[:, hd, :MLA_NOPE] for hd in range(N_HEADS)]
    none32 = [jnp.zeros((MLA_KV_RANK, MLA_ROPE), w_ukv.dtype)] * 2
    wukv = jnp.concatenate([_pair_cols(knope[0:2], none32), _pair_cols(knope[2:4], none32),
                            kv3[:, :, MLA_NOPE:].reshape(MLA_KV_RANK, 256)], axis=1).astype(BF16)
    return wuq, wukv


def _pair_rope_tables(lat_len, tile):
    cos_t, sin_t = _rope_tables(lat_len, MLA_ROPE, tile)
    cos32, sin32 = cos_t[:, :MLA_ROPE], sin_t[:, :MLA_ROPE]
    n = cos_t.shape[0]
    one, zero = jnp.ones((n, MLA_NOPE), F32), jnp.zeros((n, MLA_NOPE), F32)
    return (jnp.concatenate([one, cos32, one, cos32, one], axis=1),
            jnp.concatenate([zero, sin32, zero, sin32, zero], axis=1))


def _rope_tables(lat_len, rot, tile):
    t = jnp.arange(lat_len)
    row = (t // GRID_W).astype(F32)
    col = (t % GRID_W).astype(F32)
    nf = rot // 4
    freqs = ROPE_BASE ** (-jnp.arange(nf, dtype=F32) / nf)
    ang = jnp.concatenate([row[:, None] * freqs, col[:, None] * freqs], -1)
    cos, sin = jnp.cos(ang), jnp.sin(ang)
    cos_t = jnp.tile(jnp.concatenate([cos, cos], -1), (1, N_HEADS))
    sin_t = jnp.tile(jnp.concatenate([-sin, sin], -1), (1, N_HEADS))
    w = N_HEADS * rot
    return (jnp.concatenate([jnp.ones((tile, w), F32), cos_t], 0),
            jnp.concatenate([jnp.zeros((tile, w), F32), sin_t], 0))


def _na_bias_tables(rpb, rows):
    kh = min(NA_KH, rows)
    qrow = np.array([0, 1, 2, 3, rows // 2, rows - 3, rows - 2, rows - 1])
    start = np.clip(qrow - kh // 2, 0, rows - kh)
    dr0 = start - qrow + NA_KH - 1
    qc = np.arange(GRID_W)
    kc = np.arange(GRID_W)
    cstart = np.clip(qc - NA_KW // 2, 0, GRID_W - NA_KW)
    ok = (kc[None, :] >= cstart[:, None]) & (kc[None, :] < cstart[:, None] + NA_KW)
    n_h, n_a, n_c = rpb.shape
    edge = GRID_W - NA_KW
    w = jnp.concatenate([jnp.broadcast_to(rpb[:, :, :1], (n_h, n_a, edge)), rpb,
                         jnp.broadcast_to(rpb[:, :, -1:], (n_h, n_a, edge + 1))], axis=-1)
    skew = jnp.tile(w, (1, 1, GRID_W))[:, :, :GRID_W * (2 * GRID_W - 1)].reshape(n_h, n_a, GRID_W, 2 * GRID_W - 1)
    toep = skew[:, :, :, GRID_W - 1:]
    tabs = []
    for d0 in dr0:
        b = jnp.where(ok[None, None], toep[:, d0:d0 + kh], NEG_INF)
        tabs.append(b.transpose(0, 2, 1, 3).reshape(N_HEADS * GRID_W, kh * GRID_W))
    return jnp.stack(tabs)


def _swa_bias_tables(lat_len):
    nb = lat_len // SWA_WIN
    qi = np.arange(SWA_WIN)
    kj = np.arange(3 * SWA_WIN)
    tabs = []
    for n in (0, 1, nb - 1):
        kpos = _swa_window_start(n, nb) + kj
        in_win = np.abs(kpos[None, :] - (n * SWA_WIN + qi[:, None])) <= SWA_WIN
        tabs.append(np.where(in_win, 0.0, NEG_INF))
    return jnp.asarray(np.stack(tabs), F32)


def _swa_window_start(n, nb):
    lo = n - 1
    lo = jnp.clip(lo, 0, nb - 3) if isinstance(n, jax.Array) else min(max(lo, 0), nb - 3)
    return lo * SWA_WIN


def _heads_to_lanes(t):
    b, h, s, d = t.shape
    return t.transpose(0, 2, 1, 3).reshape(b, s, h * d)


def _lanes_to_heads(t, b, s, h):
    return t.reshape(b, s, h, -1).transpose(0, 2, 1, 3)


def kernel(x_prompt, x_sample, cache_nat_k, cache_nat_v, cache_mla_ckv, cache_mla_krope, cache_swa_k, cache_swa_v, c, c_ctx, w_in, w_out, out_norm_g, w_ada, b_ada, ln1_g, ln1_b, ln2_g, ln2_b, a_norm_g, a_norm_b, a_w_s, a_b_s, nat_rpb, mla_q_norm_g, mla_w_uq, mla_kv_norm_g, mla_w_ukv, swa_sinks, peer_w_q, peer_sub_keys, peer_u, peer_v):
    n_b, seq, d = x_prompt.shape
    n_db, lat_len, _ = x_sample.shape
    past = cache_nat_k.shape[3]
    n_ctx = n_b * seq
    nt = n_ctx + n_db * lat_len
    rows = lat_len // GRID_W

    x = (x_prompt.reshape(n_ctx, d), x_sample.reshape(n_db * lat_len, d))
    conds = jnp.zeros((8, d), F32).at[0].set(c_ctx).at[1:1 + n_db].set(c)
    mods = _ada_call(conds, w_ada, b_ada).reshape(DEPTH, 8, 6, d)

    cos_c, sin_c = _pair_rope_tables(lat_len, TM)
    cos_d, sin_d = _rope_tables(lat_len, HEAD_DIM, TM)
    swa_bias = _swa_bias_tables(lat_len)
    nb_swa = lat_len // SWA_WIN

    q_ctx = lambda b, j: b
    lat_q = lambda tq: (lambda b, j: (n_ctx + b * lat_len) // tq + j)
    ctx_keys = lambda w: pl.BlockSpec((seq, w), lambda b, j: (b, 0))
    lat_keys = lambda w: pl.BlockSpec((lat_len, w), lambda b, j: (n_ctx // lat_len + b, 0))

    wq16 = peer_w_q.astype(BF16)
    u16 = peer_u.astype(BF16)
    vt16 = jnp.swapaxes(peer_v, 1, 2).astype(BF16)

    states = [[] for _ in range(6)]
    for l in range(DEPTH):
        mod = mods[l]
        wbig = _inproj_weight(w_in[l])
        wuq, wukv = _mla_weights(mla_w_uq[l], mla_w_ukv[l])
        mixer_a = (a_norm_g[l][None], a_norm_b[l][None], a_w_s[l].astype(BF16),
                   jnp.repeat(a_b_s[l].T, HEAD_DIM, axis=1))
        (oa, qb, kb, vb, kb16, vb16, qp0, qp1, ckvn, krp, kp0, kp1, vc,
         qd, kd, vd, kd16, vd16) = _inproj_call(
            x, mod, wbig, wuq, wukv, mla_q_norm_g[l][None], mla_kv_norm_g[l][None],
            (cos_c, sin_c, cos_d, sin_d), mixer_a, n_ctx, lat_len)

        ob_c = _attn_call("ctx_attn_b", (n_b, 1), seq, seq, HEAD_DIM ** -0.5, q_ctx, qb, kb16, vb16, ctx_keys)
        oc_c = _attn_call("ctx_attn_c", (n_b, 1), seq, seq, MLA_SCALE, q_ctx, qp0, kp0, vc, ctx_keys,
                          q2=qp1, k2=kp1)
        od_c = _attn_call("ctx_attn_d", (n_b, 1), seq, seq, HEAD_DIM ** -0.5, q_ctx, qd, kd16, vd16, ctx_keys,
                          sinks=swa_sinks[l])

        na_bias = _na_bias_tables(nat_rpb[l], rows)
        kh = min(NA_KH, rows)
        ob_l = _attn_call(
            "lat_attn_b", (n_db, rows // NA_ROWS_PER_STEP), GRID_W, kh * GRID_W, HEAD_DIM ** -0.5,
            lat_q(NA_ROWS_PER_STEP * GRID_W), qb, kb16, vb16, lat_keys, n_sub=NA_ROWS_PER_STEP,
            extra=(_heads_to_lanes(cache_nat_k[:, l]).astype(BF16), _heads_to_lanes(cache_nat_v[:, l]).astype(BF16)),
            bias=na_bias,
            bias_index=lambda r: jnp.where(r < 4, r, jnp.where(r > rows - 4, r - (rows - 8), 4)),
            start_fn=lambda r: jnp.clip(r - kh // 2, 0, rows - kh) * GRID_W)
        krx = cache_mla_krope[:, l].reshape(n_db * past, MLA_ROPE)
        kx0, kx1, vx = _kvexp_call(cache_mla_ckv[:, l].reshape(n_db * past, MLA_KV_RANK),
                                   _pair_cols(None, [krx, krx]), wukv)
        oc_l = _attn_call(
            "lat_attn_c", (n_db, lat_len // 512), 256, lat_len, MLA_SCALE, lat_q(512), qp0, kp0, vc, lat_keys, n_sub=2,
            q2=qp1, k2=kp1,
            extra=tuple(t.reshape(n_db, past, 256) for t in (kx0, kx1, vx)))
        rep_kv = lambda t: jnp.repeat(t, 2, axis=1)
        od_l = _attn_call(
            "lat_attn_d", (n_db, nb_swa // SWA_BLOCKS_PER_STEP), SWA_WIN, 3 * SWA_WIN, HEAD_DIM ** -0.5,
            lat_q(SWA_BLOCKS_PER_STEP * SWA_WIN), qd, kd16, vd16, lat_keys, n_sub=SWA_BLOCKS_PER_STEP,
            extra=(_heads_to_lanes(rep_kv(cache_swa_k[:, l])).astype(BF16),
                   _heads_to_lanes(rep_kv(cache_swa_v[:, l])).astype(BF16)),
            bias=swa_bias,
            bias_index=lambda n: jnp.where(n == 0, 0, jnp.where(n == nb_swa - 1, 2, 1)),
            sinks=swa_sinks[l],
            start_fn=lambda n: _swa_window_start(n, nb_swa))
        ob, oc, od = (ob_c, ob_l), (oc_c, oc_l), (od_c, od_l)

        x1 = _merge_call(oa, ob, oc, od, x, mod, out_norm_g[l][None], w_out[l].astype(BF16),
                         ln1_g[l][None], ln1_b[l][None], n_ctx, lat_len)
        h2, p0, te, e1 = _route_call(x1, mod, wq16, peer_sub_keys, l, n_ctx, lat_len)
        experts = functools.partial(_expert_call, h2, u16, vt16, l, p0, te, e1, x1, mod,
                                    ln2_g[l][None], ln2_b[l][None], n_ctx, lat_len)
        if l + 1 < DEPTH:
            x = experts()
        else:
            y_prompt = experts(tok0=0, n_tok=n_ctx).reshape(n_b, seq, d)
            y_sample = experts(tok0=n_ctx, n_tok=n_db * lat_len).reshape(n_db, lat_len, d)

        states[0].append(_lanes_to_heads(kb[:n_ctx], n_b, seq, N_HEADS))
        states[1].append(_lanes_to_heads(vb[:n_ctx], n_b, seq, N_HEADS))
        states[2].append(ckvn[:n_ctx].reshape(n_b, seq, MLA_KV_RANK))
        states[3].append(krp[:n_ctx, MLA_NOPE:PAIR_W].reshape(n_b, seq, MLA_ROPE))
        states[4].append(_lanes_to_heads(kd[:n_ctx], n_b, seq, N_HEADS)[:, ::2])
        states[5].append(_lanes_to_heads(vd[:n_ctx], n_b, seq, N_HEADS)[:, ::2])

    return (y_prompt, y_sample) + tuple(jnp.stack(s, axis=1) for s in states)
```
